```python
import jax, jax.numpy as jnp
from jax import lax
import numpy as np

D_MODEL = 2048
BATCH = 4
SEQ = 2048
DEPTH = 2

PLE_DIM = 256
LRU_WIDTH = 1024
LRU_BLOCKS = 8
LRU_BLOCK_W = LRU_WIDTH // LRU_BLOCKS
LRU_C = 8.0
CONV_WIDTH = 4
ATT_HEADS = 8
ATT_HEAD_DIM = 128
IDX_HEADS = 16
IDX_HEAD_DIM = 64
TOPK_MAX = 256
SPARSE_Q_BLOCK = 64
MLA_HEADS = 8
MLA_Q_LORA = 768
MLA_KV_LORA = 512
MLA_NOPE = 128
MLA_ROPE = 64
MLA_QK = MLA_NOPE + MLA_ROPE
MLA_V = 128
DENSE_Q_BLOCK = 128
N_BRANCH = 3
BRANCH_WIDTH = 1024
D_FF = 5632
N_EXPERTS = 8
TOP_K = 2
D_FF_EXPERT = 7168
MOE_ROW_BLOCK = 256
ROPE_THETA = 10000.0
EPS = 1e-6
N_DENSE = (DEPTH + 1) // 2
N_MOE = DEPTH // 2
IN_SPLITS = (LRU_WIDTH, LRU_WIDTH,
             ATT_HEADS * ATT_HEAD_DIM, ATT_HEADS * ATT_HEAD_DIM, ATT_HEADS * ATT_HEAD_DIM,
             IDX_HEADS * IDX_HEAD_DIM, IDX_HEAD_DIM, IDX_HEADS,
             MLA_Q_LORA, MLA_KV_LORA, MLA_ROPE,
             N_BRANCH * D_MODEL)
D_IN = sum(IN_SPLITS)

kernel_name = 'hybrid_rglru_dsa_mla_moe_block'


def rms_norm(x, g):
    x32 = x.astype(jnp.float32)
    y = x32 * lax.rsqrt(jnp.mean(x32 * x32, axis=-1, keepdims=True) + EPS)
    return (y * g.astype(jnp.float32)).astype(x.dtype)


def rope(x, positions):
    d = x.shape[-1]
    inv_freq = ROPE_THETA ** (-jnp.arange(0, d, 2, dtype=jnp.float32) / d)
    ang = positions.astype(jnp.float32)[:, :, None, None] * inv_freq
    cos, sin = jnp.cos(ang), jnp.sin(ang)
    x32 = x.astype(jnp.float32)
    x1, x2 = x32[..., :d // 2], x32[..., d // 2:]
    return jnp.concatenate([x1 * cos - x2 * sin, x2 * cos + x1 * sin], axis=-1).astype(x.dtype)


def rglru_branch(xb, gb, conv_w, conv_b, wa, ba, wx, bx, lam):
    B_, T_, C = xb.shape
    xc = lax.conv_general_dilated(xb, conv_w[:, None, :].astype(xb.dtype), window_strides=(1,),
                                  padding=[(CONV_WIDTH - 1, 0)],
                                  dimension_numbers=('NWC', 'WIO', 'NWC'),
                                  feature_group_count=C) + conv_b
    xblk = xc.reshape(B_, T_, LRU_BLOCKS, LRU_BLOCK_W)
    r = jax.nn.sigmoid(jnp.einsum('btnc,ncd->btnd', xblk, wa).reshape(B_, T_, C) + ba)
    i = jax.nn.sigmoid(jnp.einsum('btnc,ncd->btnd', xblk, wx).reshape(B_, T_, C) + bx)
    log_a = -LRU_C * r.astype(jnp.float32) * jax.nn.softplus(-lam.astype(jnp.float32))
    a = jnp.exp(log_a)
    b = jnp.sqrt(-jnp.expm1(2.0 * log_a)) * (i * xc).astype(jnp.float32)

    def combine(left, right):
        a1, b1 = left
        a2, b2 = right
        return a1 * a2, a2 * b1 + b2

    _, h = lax.associative_scan(combine, (a, b), axis=1)
    return h.astype(xb.dtype) * jax.nn.gelu(gb, approximate=True)


def sparse_index_attention(q, k, v, q_idx, k_idx, w_idx, n_sel):
    B_, T_, H, d = q.shape
    nb = T_ // SPARSE_Q_BLOCK
    key_pos = jnp.arange(T_)
    idx_scale = IDX_HEADS ** -0.5 * IDX_HEAD_DIM ** -0.5

    def blockify(t):
        return t.reshape((B_, nb, SPARSE_Q_BLOCK) + t.shape[2:]).swapaxes(0, 1)

    def one_block(args):
        qb, qib, wb, q0 = args
        qpos = q0 + jnp.arange(SPARSE_Q_BLOCK)
        s = jax.nn.relu(jnp.einsum('bqhc,bkc->bqkh', qib, k_idx).astype(jnp.float32))
        score = jnp.einsum('bqkh,bqh->bqk', s, wb.astype(jnp.float32)) * idx_scale
        causal = key_pos[None, :] <= qpos[:, None]
        score = jnp.where(causal[None], score, -jnp.inf)
        _, sel = lax.top_k(score, n_sel)
        k_sel = jax.vmap(lambda kb, ib: kb[ib])(k, sel)
        v_sel = jax.vmap(lambda vb, ib: vb[ib])(v, sel)
        logits = jnp.einsum('bqhd,bqnhd->bhqn', qb, k_sel).astype(jnp.float32) * d ** -0.5
        valid = sel <= qpos[None, :, None]
        logits = jnp.where(valid[:, None], logits, -jnp.inf)
        prob = jax.nn.softmax(logits, axis=-1).astype(v.dtype)
        return jnp.einsum('bhqn,bqnhd->bqhd', prob, v_sel)

    starts = jnp.arange(nb) * SPARSE_Q_BLOCK
    out = lax.map(one_block, (blockify(q), blockify(q_idx), blockify(w_idx), starts))
    return out.swapaxes(0, 1).reshape(B_, T_, H * d)


def causal_block_attention(q, k, v):
    B_, T_, H, dq = q.shape
    nb = T_ // DENSE_Q_BLOCK
    key_pos = jnp.arange(T_)
    qblocks = q.reshape(B_, nb, DENSE_Q_BLOCK, H, dq).swapaxes(0, 1)

    def one_block(args):
        qb, q0 = args
        qpos = q0 + jnp.arange(DENSE_Q_BLOCK)
        logits = jnp.einsum('bqhd,bkhd->bhqk', qb, k).astype(jnp.float32) * dq ** -0.5
        logits = jnp.where(key_pos[None, :] <= qpos[:, None], logits, -jnp.inf)
        prob = jax.nn.softmax(logits, axis=-1).astype(v.dtype)
        return jnp.einsum('bhqk,bkhd->bqhd', prob, v)

    out = lax.map(one_block, (qblocks, jnp.arange(nb) * DENSE_Q_BLOCK))
    return out.swapaxes(0, 1).reshape(B_, T_, H * v.shape[-1])


def mla_branch(c_q, c_kv, k_rope, positions, qa_norm, kva_norm, w_uq, w_ukv, qn, kn):
    B_, T_, _ = c_q.shape
    q = (rms_norm(c_q, qa_norm) @ w_uq).reshape(B_, T_, MLA_HEADS, MLA_QK)
    kv = (rms_norm(c_kv, kva_norm) @ w_ukv).reshape(B_, T_, MLA_HEADS, MLA_NOPE + MLA_V)
    k_nope, v = kv[..., :MLA_NOPE], kv[..., MLA_NOPE:]
    k_r = jnp.broadcast_to(k_rope[:, :, None, :], (B_, T_, MLA_HEADS, MLA_ROPE))
    k = jnp.concatenate([k_nope, k_r], axis=-1)
    q = rms_norm(q, qn)
    k = rms_norm(k, kn)
    q = jnp.concatenate([q[..., :MLA_NOPE], rope(q[..., MLA_NOPE:], positions)], axis=-1)
    k = jnp.concatenate([k[..., :MLA_NOPE], rope(k[..., MLA_NOPE:], positions)], axis=-1)
    return causal_block_attention(q, k, v)


def dense_swiglu(h2, wg, wu, wd):
    return (jax.nn.silu(h2 @ wg) * (h2 @ wu)) @ wd


def moe_swiglu(h2, router, w_gate, w_up, w_down):
    T_, D_ = h2.shape
    logits = (h2 @ router).astype(jnp.float32)
    top_val, top_idx = lax.top_k(logits, TOP_K)
    gate = jax.nn.softmax(top_val, axis=-1)
    n_assign = T_ * TOP_K
    e_flat = top_idx.reshape(n_assign)
    tok_flat = jnp.repeat(jnp.arange(T_, dtype=jnp.int32), TOP_K)
    g_flat = gate.reshape(n_assign)
    order = jnp.argsort(e_flat)
    e_s, tok_s, g_s = e_flat[order], tok_flat[order], g_flat[order]
    counts = jnp.bincount(e_flat, length=N_EXPERTS)
    padded = (counts + MOE_ROW_BLOCK - 1) // MOE_ROW_BLOCK * MOE_ROW_BLOCK
    pad_ends = jnp.cumsum(padded)
    pad_starts = pad_ends - padded
    starts = jnp.cumsum(counts) - counts
    dest = pad_starts[e_s] + (jnp.arange(n_assign) - starts[e_s])
    n_blocks = -(-n_assign // MOE_ROW_BLOCK) + N_EXPERTS
    n_rows = n_blocks * MOE_ROW_BLOCK
    row_tok = jnp.full((n_rows,), T_, jnp.int32).at[dest].set(tok_s)
    row_gate = jnp.zeros((n_rows,), jnp.float32).at[dest].set(g_s)
    blk_expert = jnp.minimum(
        jnp.searchsorted(pad_ends, jnp.arange(n_blocks) * MOE_ROW_BLOCK, side='right'),
        N_EXPERTS - 1)
    h_pad = jnp.concatenate([h2, jnp.zeros((1, D_), h2.dtype)], axis=0)
    xs = h_pad[row_tok].reshape(n_blocks, MOE_ROW_BLOCK, D_)

    def expert_block(args):
        xb, e = args
        return (jax.nn.silu(xb @ w_gate[e]) * (xb @ w_up[e])) @ w_down[e]

    ys = lax.map(expert_block, (xs, blk_expert)).reshape(n_rows, D_)
    ys = ys * row_gate[:, None].astype(ys.dtype)
    return jnp.zeros((T_ + 1, D_), ys.dtype).at[row_tok].add(ys)[:T_]


def setup_inputs(seed: int = 0) -> dict:
    key = jax.random.key(seed)
    ks = jax.random.split(key, 40)
    f32 = jnp.float32

    def nrm(k, shape, scale):
        return jax.random.normal(k, shape, f32) * scale

    def gain(k, shape):
        return 1.0 + 0.02 * jax.random.normal(k, shape, f32)

    u = jax.random.uniform(ks[11], (DEPTH, LRU_WIDTH), f32)
    a_c = 0.81 + 0.188 * u
    a = a_c ** (1.0 / LRU_C)
    lru_lambda = jnp.log(a) - jnp.log1p(-a)
    return {
        'x': nrm(ks[0], (BATCH, SEQ, D_MODEL), 1.0),
        'p': nrm(ks[1], (DEPTH, BATCH, SEQ, PLE_DIM), 1.0),
        'positions': jnp.broadcast_to(jnp.arange(SEQ, dtype=jnp.int32)[None, :], (BATCH, SEQ)),
        'ln_mix': gain(ks[2], (DEPTH, D_MODEL)),
        'w_in': nrm(ks[3], (DEPTH, D_MODEL, D_IN), D_MODEL ** -0.5),
        'conv_w': nrm(ks[4], (DEPTH, CONV_WIDTH, LRU_WIDTH), 0.5),
        'conv_b': nrm(ks[5], (DEPTH, LRU_WIDTH), 0.01),
        'lru_wa': nrm(ks[6], (DEPTH, LRU_BLOCKS, LRU_BLOCK_W, LRU_BLOCK_W), LRU_BLOCK_W ** -0.5),
        'lru_ba': nrm(ks[7], (DEPTH, LRU_WIDTH), 0.01),
        'lru_wx': nrm(ks[8], (DEPTH, LRU_BLOCKS, LRU_BLOCK_W, LRU_BLOCK_W), LRU_BLOCK_W ** -0.5),
        'lru_bx': nrm(ks[9], (DEPTH, LRU_WIDTH), 0.01),
        'lru_lambda': lru_lambda,
        'att_q_norm': gain(ks[12], (DEPTH, ATT_HEAD_DIM)),
        'att_k_norm': gain(ks[13], (DEPTH, ATT_HEAD_DIM)),
        'mla_qa_norm': gain(ks[14], (DEPTH, MLA_Q_LORA)),
        'mla_kva_norm': gain(ks[15], (DEPTH, MLA_KV_LORA)),
        'mla_w_uq': nrm(ks[16], (DEPTH, MLA_Q_LORA, MLA_HEADS * MLA_QK), MLA_Q_LORA ** -0.5),
        'mla_w_ukv': nrm(ks[17], (DEPTH, MLA_KV_LORA, MLA_HEADS * (MLA_NOPE + MLA_V)), MLA_KV_LORA ** -0.5),
        'mla_q_norm': gain(ks[18], (DEPTH, MLA_QK)),
        'mla_k_norm': gain(ks[19], (DEPTH, MLA_QK)),
        'w_branch': nrm(ks[20], (DEPTH, N_BRANCH, BRANCH_WIDTH, D_MODEL), BRANCH_WIDTH ** -0.5),
        'w_out': nrm(ks[21], (DEPTH, D_MODEL, D_MODEL), D_MODEL ** -0.5),
        'ln_ffn': gain(ks[22], (DEPTH, D_MODEL)),
        'dense_w_gate': nrm(ks[23], (N_DENSE, D_MODEL, D_FF), D_MODEL ** -0.5),
        'dense_w_up': nrm(ks[24], (N_DENSE, D_MODEL, D_FF), D_MODEL ** -0.5),
        'dense_w_down': nrm(ks[25], (N_DENSE, D_FF, D_MODEL), D_FF ** -0.5),
        'moe_router': nrm(ks[26], (N_MOE, D_MODEL, N_EXPERTS), D_MODEL ** -0.5),
        'moe_w_gate': nrm(ks[27], (N_MOE, N_EXPERTS, D_MODEL, D_FF_EXPERT), D_MODEL ** -0.5),
        'moe_w_up': nrm(ks[28], (N_MOE, N_EXPERTS, D_MODEL, D_FF_EXPERT), D_MODEL ** -0.5),
        'moe_w_down': nrm(ks[29], (N_MOE, N_EXPERTS, D_FF_EXPERT, D_MODEL), D_FF_EXPERT ** -0.5),
        'ple_norm': gain(ks[30], (DEPTH, D_MODEL)),
        'ple_w_gate': nrm(ks[31], (DEPTH, D_MODEL, D_MODEL), D_MODEL ** -0.5),
        'ple_w_proj': nrm(ks[32], (DEPTH, PLE_DIM, D_MODEL), PLE_DIM ** -0.5),
    }


def reference(x, p, positions, ln_mix, w_in, conv_w, conv_b, lru_wa, lru_ba, lru_wx, lru_bx,
              lru_lambda, att_q_norm, att_k_norm, mla_qa_norm, mla_kva_norm, mla_w_uq, mla_w_ukv,
              mla_q_norm, mla_k_norm, w_branch, w_out, ln_ffn, dense_w_gate, dense_w_up,
              dense_w_down, moe_router, moe_w_gate, moe_w_up, moe_w_down, ple_norm, ple_w_gate,
              ple_w_proj):
    B_, T_, D_ = x.shape
    n_sel = min(TOPK_MAX, T_ // 4)
    split_at = np.cumsum(IN_SPLITS)[:-1].tolist()
    h = x
    for i in range(DEPTH):
        hn = rms_norm(h, ln_mix[i])
        proj = hn @ w_in[i]
        (lru_x, lru_g, aq, ak, av, iq, ik, iw, cq, ckv, kr, gate_logits) = jnp.split(proj, split_at, axis=-1)

        out_a = rglru_branch(lru_x, lru_g, conv_w[i], conv_b[i], lru_wa[i], lru_ba[i],
                             lru_wx[i], lru_bx[i], lru_lambda[i])

        q = rope(rms_norm(aq.reshape(B_, T_, ATT_HEADS, ATT_HEAD_DIM), att_q_norm[i]), positions)
        k = rope(rms_norm(ak.reshape(B_, T_, ATT_HEADS, ATT_HEAD_DIM), att_k_norm[i]), positions)
        v = av.reshape(B_, T_, ATT_HEADS, ATT_HEAD_DIM)
        q_idx = rope(iq.reshape(B_, T_, IDX_HEADS, IDX_HEAD_DIM), positions)
        k_idx = rope(ik[:, :, None, :], positions)[:, :, 0, :]
        out_b = sparse_index_attention(q, k, v, q_idx, k_idx, iw, n_sel)

        out_c = mla_branch(cq, ckv, kr, positions, mla_qa_norm[i], mla_kva_norm[i],
                           mla_w_uq[i], mla_w_ukv[i], mla_q_norm[i], mla_k_norm[i])

        branches = jnp.stack([out_a, out_b, out_c], axis=2)
        br_proj = jnp.einsum('btnc,ncd->btnd', branches, w_branch[i])
        gates = jax.nn.sigmoid(gate_logits.reshape(B_, T_, N_BRANCH, D_))
        merged = jnp.einsum('btnd,btnd->btd', gates, br_proj)
        h = h + merged @ w_out[i]

        hn2 = rms_norm(h, ln_ffn[i])
        if i % 2 == 0:
            j = i // 2
            ff = dense_swiglu(hn2, dense_w_gate[j], dense_w_up[j], dense_w_down[j])
        else:
            j = i // 2
            ff = moe_swiglu(hn2.reshape(B_ * T_, D_), moe_router[j], moe_w_gate[j],
                            moe_w_up[j], moe_w_down[j]).reshape(B_, T_, D_)
        h = h + ff

        ple_gate = jax.nn.sigmoid(rms_norm(h, ple_norm[i]) @ ple_w_gate[i])
        h = h + (p[i] @ ple_w_proj[i]) * ple_gate
    return h
```

```python
import functools

import jax
import jax.numpy as jnp
from jax import lax
from jax.experimental import pallas as pl
from jax.experimental.pallas import tpu as pltpu

F32, BF16, I32 = jnp.float32, jnp.bfloat16, jnp.int32

EPS = 1e-6
ROPE_THETA = 10000.0
LRU_C = 8.0
LRU_WIDTH = 1024
LRU_BLOCKS = 8
ATT_HEADS = 8
ATT_HEAD_DIM = 128
IDX_HEADS = 16
IDX_HEAD_DIM = 64
TOPK_MAX = 256
MLA_HEADS = 8
MLA_Q_LORA = 768
MLA_KV_LORA = 512
MLA_NOPE = 128
MLA_ROPE = 64
MLA_QK = MLA_NOPE + MLA_ROPE
MLA_V = 128
N_BRANCH = 3
N_EXPERTS = 8
TOP_K = 2

LANES = 128
SUBLANES = 8
VMEM_LIMIT_BYTES = 56 << 20
MOE_ROWS = 512
INT_MIN = -(2 ** 31)

NT_DIMS = (((1,), (1,)), ((), ()))


def _params(sem):
    return pltpu.CompilerParams(dimension_semantics=sem, vmem_limit_bytes=VMEM_LIMIT_BYTES)


def _tile(n, cap, unit=LANES):
    if n <= cap:
        return n
    best = None
    for t in range(unit, cap + 1, unit):
        if n % t == 0:
            best = t
    assert best is not None, (n, cap)
    return best


def _rmsnorm_kernel(x_ref, g_ref, o_ref):
    x = x_ref[...]
    ms = jnp.mean(x * x, axis=-1, keepdims=True)
    o_ref[...] = (x * lax.rsqrt(ms + EPS) * g_ref[...]).astype(o_ref.dtype)


def _rmsnorm(x, g):
    m, d = x.shape
    tm = _tile(m, 512, SUBLANES)
    return pl.pallas_call(
        _rmsnorm_kernel,
        grid=(m // tm,),
        in_specs=[pl.BlockSpec((tm, d), lambda i: (i, 0)), pl.BlockSpec((1, d), lambda i: (0, 0))],
        out_specs=pl.BlockSpec((tm, d), lambda i: (i, 0)),
        out_shape=jax.ShapeDtypeStruct((m, d), BF16),
        compiler_params=_params(("parallel",)),
        name="rmsnorm",
    )(x, g.reshape(1, d))


def _mm_kernel(x_ref, w_ref, o_ref):
    o_ref[...] = jnp.dot(x_ref[...], w_ref[...], preferred_element_type=F32).astype(o_ref.dtype)


def _mm_res_kernel(x_ref, w_ref, r_ref, o_ref):
    o_ref[...] = r_ref[...] + jnp.dot(x_ref[...], w_ref[...], preferred_element_type=F32)


def _matmul(x, w, out_dtype, res=None, tm_cap=1024, tn_cap=512, name="matmul"):
    m, k = x.shape
    n = w.shape[1]
    tm = _tile(m, tm_cap, SUBLANES)
    tn = _tile(n, tn_cap)
    in_specs = [pl.BlockSpec((tm, k), lambda i, j: (i, 0)), pl.BlockSpec((k, tn), lambda i, j: (0, j))]
    args = [x, w]
    body = _mm_kernel
    if res is not None:
        in_specs.append(pl.BlockSpec((tm, tn), lambda i, j: (i, j)))
        args.append(res)
        body = _mm_res_kernel
    return pl.pallas_call(
        body,
        grid=(m // tm, n // tn),
        in_specs=in_specs,
        out_specs=pl.BlockSpec((tm, tn), lambda i, j: (i, j)),
        out_shape=jax.ShapeDtypeStruct((m, n), out_dtype),
        compiler_params=_params(("parallel", "arbitrary")),
        name=name,
    )(*args)


EXPM1_SERIES_BOUND = 0.25
EXPM1_SERIES_TERMS = 10


def _expm1(y):
    poly = jnp.full_like(y, 1.0 / 3628800.0)
    fact = 3628800.0
    for n in range(EXPM1_SERIES_TERMS, 1, -1):
        fact /= n
        poly = poly * y + 1.0 / fact
    return jnp.where(jnp.abs(y) < EXPM1_SERIES_BOUND, poly * y, jnp.exp(y) - 1.0)


def _rglru_kernel(x_ref, g_ref, cw_ref, cb_ref, wa_ref, ba_ref, wx_ref, bx_ref, lam_ref, o_ref,
                  xs_scr, a_scr, b_scr, h_scr, *, tt):
    c = x_ref.shape[1]
    t = pl.program_id(1)

    @pl.when(t == 0)
    def _():
        xs_scr[0:SUBLANES, :] = jnp.zeros((SUBLANES, c), F32)
        h_scr[...] = jnp.zeros_like(h_scr)

    x = x_ref[...]
    xs_scr[SUBLANES:SUBLANES + tt, :] = x
    cw = cw_ref[...]
    xc = (xs_scr[SUBLANES - 3:SUBLANES - 3 + tt, :] * cw[0:1, :]
          + xs_scr[SUBLANES - 2:SUBLANES - 2 + tt, :] * cw[1:2, :]
          + xs_scr[SUBLANES - 1:SUBLANES - 1 + tt, :] * cw[2:3, :]
          + x * cw[3:4, :]) + cb_ref[...]
    xs_scr[0:SUBLANES, :] = x[tt - SUBLANES:tt, :]

    xcb = xc.astype(BF16)
    bw = c // LRU_BLOCKS
    ra = jnp.concatenate(
        [jnp.dot(xcb[:, n * bw:(n + 1) * bw], wa_ref[n], preferred_element_type=F32)
         for n in range(LRU_BLOCKS)], axis=1) + ba_ref[...]
    rx = jnp.concatenate(
        [jnp.dot(xcb[:, n * bw:(n + 1) * bw], wx_ref[n], preferred_element_type=F32)
         for n in range(LRU_BLOCKS)], axis=1) + bx_ref[...]
    r = jax.nn.sigmoid(ra)
    gi = jax.nn.sigmoid(rx)
    nlam = -lam_ref[...]
    softplus = jnp.maximum(nlam, 0.0) + jnp.log1p(jnp.exp(-jnp.abs(nlam)))
    log_a = (-LRU_C) * r * softplus
    a_scr[...] = jnp.exp(log_a)
    b_scr[...] = jnp.sqrt(-_expm1(2.0 * log_a)) * (gi * xc)

    row = lax.broadcasted_iota(I32, (SUBLANES, c), 0)

    def group(gidx, h):
        off = pl.multiple_of(gidx * SUBLANES, SUBLANES)
        a8 = a_scr[pl.ds(off, SUBLANES), :]
        b8 = b_scr[pl.ds(off, SUBLANES), :]
        for s in (1, 2, 4):
            keep = row >= s
            a_sh = jnp.where(keep, pltpu.roll(a8, s, 0), 1.0)
            b_sh = jnp.where(keep, pltpu.roll(b8, s, 0), 0.0)
            b8 = a8 * b_sh + b8
            a8 = a8 * a_sh
        h8 = a8 * h + b8
        b_scr[pl.ds(off, SUBLANES), :] = h8
        return h8[SUBLANES - 1:SUBLANES, :]

    h_scr[...] = lax.fori_loop(0, tt // SUBLANES, group, h_scr[...])
    o_ref[...] = (b_scr[...] * jax.nn.gelu(g_ref[...], approximate=True)).astype(o_ref.dtype)


def _rglru(proj, b, t, conv_w, conv_b, wa, ba, wx, bx, lam):
    c = LRU_WIDTH
    tt = _tile(t, 256, SUBLANES)
    nt = t // tt
    row = lambda v: v.reshape(1, c)
    wspec = pl.BlockSpec(wa.shape, lambda bi, ti: (0, 0, 0))
    vspec = pl.BlockSpec((1, c), lambda bi, ti: (0, 0))
    return pl.pallas_call(
        functools.partial(_rglru_kernel, tt=tt),
        grid=(b, nt),
        in_specs=[pl.BlockSpec((tt, c), lambda bi, ti: (bi * nt + ti, 0)),
                  pl.BlockSpec((tt, c), lambda bi, ti: (bi * nt + ti, 1)),
                  pl.BlockSpec(conv_w.shape, lambda bi, ti: (0, 0)), vspec,
                  wspec, vspec, wspec, vspec, vspec],
        out_specs=pl.BlockSpec((tt, c), lambda bi, ti: (bi * nt + ti, 0)),
        out_shape=jax.ShapeDtypeStruct((b * t, c), BF16),
        scratch_shapes=[pltpu.VMEM((tt + SUBLANES, c), F32), pltpu.VMEM((tt, c), F32),
                        pltpu.VMEM((tt, c), F32), pltpu.VMEM((1, c), F32)],
        compiler_params=_params(("parallel", "arbitrary")),
        name="rglru",
    )(proj, proj, conv_w, row(conv_b), wa.astype(BF16), row(ba), wx.astype(BF16), row(bx), row(lam))


def _inv_freq_lanes(d):
    f = ROPE_THETA ** (-jnp.arange(0, d, 2, dtype=F32) / d)
    return jnp.tile(jnp.concatenate([f, f]), LANES // d).reshape(1, LANES)


def _rope_tables(pos_ref, invf_ref, half):
    ang = pos_ref[...].astype(F32) * invf_ref[...]
    lane = lax.broadcasted_iota(I32, ang.shape, 1)
    first = (lane & (2 * half - 1)) < half
    return jnp.cos(ang), jnp.where(first, -jnp.sin(ang), jnp.sin(ang)), first


def _swap_halves(x, half, first):
    if 2 * half == LANES:
        return pltpu.roll(x, half, 1)
    return jnp.where(first, pltpu.roll(x, LANES - half, 1), pltpu.roll(x, half, 1))


def _qk_prep_kernel(x_ref, pos_ref, invf_ref, gq_ref, gk_ref, q_ref, k_ref):
    cosf, sinf, first = _rope_tables(pos_ref, invf_ref, ATT_HEAD_DIM // 2)
    for which, g_ref, o_ref in ((0, gq_ref, q_ref), (1, gk_ref, k_ref)):
        for h in range(ATT_HEADS):
            lo = (which * ATT_HEADS + h) * ATT_HEAD_DIM
            s = x_ref[:, lo:lo + ATT_HEAD_DIM]
            y = s * lax.rsqrt(jnp.mean(s * s, axis=-1, keepdims=True) + EPS) * g_ref[...]
            y = y * cosf + _swap_halves(y, ATT_HEAD_DIM // 2, first) * sinf
            o_ref[0, h] = y.astype(o_ref.dtype)


def _qk_prep(proj, pos, b, t, gq, gk):
    tm = _tile(t, 256, SUBLANES)
    nt = t // tm
    hd = ATT_HEADS * ATT_HEAD_DIM
    ospec = pl.BlockSpec((1, ATT_HEADS, tm, ATT_HEAD_DIM), lambda bi, ti: (bi, 0, ti, 0))
    oshape = jax.ShapeDtypeStruct((b, ATT_HEADS, t, ATT_HEAD_DIM), BF16)
    vspec = pl.BlockSpec((1, LANES), lambda bi, ti: (0, 0))
    return pl.pallas_call(
        _qk_prep_kernel,
        grid=(b, nt),
        in_specs=[pl.BlockSpec((tm, 2 * hd), lambda bi, ti: (bi * nt + ti, 1)),
                  pl.BlockSpec((tm, 1), lambda bi, ti: (bi * nt + ti, 0)), vspec, vspec, vspec],
        out_specs=[ospec, ospec],
        out_shape=[oshape, oshape],
        compiler_params=_params(("parallel", "parallel")),
        name="qk_prep",
    )(proj, pos, _inv_freq_lanes(ATT_HEAD_DIM), gq.reshape(1, LANES), gk.reshape(1, LANES))


def _idx_prep_kernel(x_ref, pos_ref, invf_ref, qi_ref, ki_ref):
    half = IDX_HEAD_DIM // 2
    cosf, sinf, first = _rope_tables(pos_ref, invf_ref, half)
    lane = lax.broadcasted_iota(I32, cosf.shape, 1)
    left = lane < IDX_HEAD_DIM

    def rope(x):
        return x * cosf + _swap_halves(x, half, first) * sinf

    def split(y):
        hi = y.astype(BF16).astype(F32)
        return hi, y - hi

    for j in range(IDX_HEADS // 2):
        hi, lo = split(rope(x_ref[:, j * LANES:(j + 1) * LANES]))
        even = jnp.where(left, hi, pltpu.roll(lo, IDX_HEAD_DIM, 1)).astype(BF16)
        odd = jnp.where(left, pltpu.roll(hi, IDX_HEAD_DIM, 1), lo).astype(BF16)
        for h, v in ((2 * j, even), (2 * j + 1, odd)):
            qi_ref[0, h, :, 0:LANES] = v
            qi_ref[0, h, :, LANES:2 * LANES] = v
    kcol = IDX_HEADS * IDX_HEAD_DIM
    khi, klo = split(rope(x_ref[:, kcol:kcol + LANES]))
    ki_ref[0, :, 0:LANES] = jnp.where(left, khi, pltpu.roll(khi, IDX_HEAD_DIM, 1)).astype(BF16)
    ki_ref[0, :, LANES:2 * LANES] = jnp.where(left, klo, pltpu.roll(klo, IDX_HEAD_DIM, 1)).astype(BF16)


def _idx_prep(idx, pos, b, t):
    tm = _tile(t, 256, SUBLANES)
    nt = t // tm
    w = idx.shape[1]
    return pl.pallas_call(
        _idx_prep_kernel,
        grid=(b, nt),
        in_specs=[pl.BlockSpec((tm, w), lambda bi, ti: (bi * nt + ti, 0)),
                  pl.BlockSpec((tm, 1), lambda bi, ti: (bi * nt + ti, 0)),
                  pl.BlockSpec((1, LANES), lambda bi, ti: (0, 0))],
        out_specs=[pl.BlockSpec((1, IDX_HEADS, tm, 2 * LANES), lambda bi, ti: (bi, 0, ti, 0)),
                   pl.BlockSpec((1, tm, 2 * LANES), lambda bi, ti: (bi, ti, 0))],
        out_shape=[jax.ShapeDtypeStruct((b, IDX_HEADS, t, 2 * LANES), BF16),
                   jax.ShapeDtypeStruct((b, t, 2 * LANES), BF16)],
        compiler_params=_params(("parallel", "parallel")),
        name="idx_prep",
    )(idx, pos, _inv_freq_lanes(IDX_HEAD_DIM))


def _mla_prep_kernel(m_ref, pos_ref, invf_ref, qa_ref, kva_ref, wuq_ref, wukv_ref,
                     qnn_ref, qnr_ref, knn_ref, knr_ref, q_ref, k_ref, v_ref):
    half = MLA_ROPE // 2
    cosf, sinf, first = _rope_tables(pos_ref, invf_ref, half)
    lane = lax.broadcasted_iota(I32, cosf.shape, 1)
    left = lane < MLA_ROPE

    def rope(x):
        return x * cosf + _swap_halves(x, half, first) * sinf

    def norm(x, g_ref):
        return (x * lax.rsqrt(jnp.mean(x * x, axis=-1, keepdims=True) + EPS) * g_ref[...]).astype(BF16)

    cq = norm(m_ref[:, 0:MLA_Q_LORA], qa_ref)
    ckv = norm(m_ref[:, MLA_Q_LORA:MLA_Q_LORA + MLA_KV_LORA], kva_ref)
    kr = m_ref[:, MLA_Q_LORA + MLA_KV_LORA:MLA_Q_LORA + MLA_KV_LORA + LANES]
    qf = jnp.dot(cq, wuq_ref[...], preferred_element_type=F32)
    kvf = jnp.dot(ckv, wukv_ref[...], preferred_element_type=F32)
    nope_w = MLA_HEADS * MLA_NOPE
    v_ref[...] = kvf[:, nope_w:nope_w + MLA_HEADS * MLA_V].astype(v_ref.dtype)

    for j in range(MLA_HEADS // 2):
        rs = qf[:, nope_w + j * LANES:nope_w + (j + 1) * LANES]
        sq = rs * rs
        ss_pair = (jnp.sum(jnp.where(left, sq, 0.0), axis=-1, keepdims=True),
                   jnp.sum(jnp.where(left, 0.0, sq), axis=-1, keepdims=True))
        for par in range(2):
            h = 2 * j + par
            nope = qf[:, h * MLA_NOPE:(h + 1) * MLA_NOPE]
            ms = (jnp.sum(nope * nope, axis=-1, keepdims=True) + ss_pair[par]) * (1.0 / MLA_QK)
            rsq = lax.rsqrt(ms + EPS)
            q_ref[0, h, :, 0:LANES] = (nope * rsq * qnn_ref[...]).astype(q_ref.dtype)
            rr = rope(rs * rsq * qnr_ref[...])
            if par == 1:
                rr = pltpu.roll(rr, MLA_ROPE, 1)
            q_ref[0, h, :, LANES:2 * LANES] = jnp.where(left, rr, 0.0).astype(q_ref.dtype)

    ss_kr = jnp.sum(jnp.where(left, kr * kr, 0.0), axis=-1, keepdims=True)
    base = jnp.where(left, rope(kr * knr_ref[...]), 0.0)
    for h in range(MLA_HEADS):
        nope = kvf[:, h * MLA_NOPE:(h + 1) * MLA_NOPE]
        ms = (jnp.sum(nope * nope, axis=-1, keepdims=True) + ss_kr) * (1.0 / MLA_QK)
        rsq = lax.rsqrt(ms + EPS)
        k_ref[0, h, :, 0:LANES] = (nope * rsq * knn_ref[...]).astype(k_ref.dtype)
        k_ref[0, h, :, LANES:2 * LANES] = (base * rsq).astype(k_ref.dtype)


def _mla_prep(mla, pos, b, t, qa, kva, w_uq, w_ukv, qn, kn):
    tm = _tile(t, 256, SUBLANES)
    nt = t // tm
    wq = w_uq.reshape(MLA_Q_LORA, MLA_HEADS, MLA_QK)
    wq = jnp.concatenate([wq[:, :, :MLA_NOPE].reshape(MLA_Q_LORA, -1),
                          wq[:, :, MLA_NOPE:].reshape(MLA_Q_LORA, -1)], axis=1).astype(BF16)
    wkv = w_ukv.reshape(MLA_KV_LORA, MLA_HEADS, MLA_NOPE + MLA_V)
    wkv = jnp.concatenate([wkv[:, :, :MLA_NOPE].reshape(MLA_KV_LORA, -1),
                           wkv[:, :, MLA_NOPE:].reshape(MLA_KV_LORA, -1)], axis=1).astype(BF16)
    dup = lambda g: jnp.tile(g[MLA_NOPE:], 2).reshape(1, LANES)
    full = lambda a: pl.BlockSpec(a.shape, lambda bi, ti: (0,) * a.ndim)
    consts = [_inv_freq_lanes(MLA_ROPE), qa.reshape(1, -1), kva.reshape(1, -1), wq, wkv,
              qn[:MLA_NOPE].reshape(1, LANES), dup(qn), kn[:MLA_NOPE].reshape(1, LANES), dup(kn)]
    hspec = pl.BlockSpec((1, MLA_HEADS, tm, 2 * LANES), lambda bi, ti: (bi, 0, ti, 0))
    hshape = jax.ShapeDtypeStruct((b, MLA_HEADS, t, 2 * LANES), BF16)
    vw = MLA_HEADS * MLA_V
    return pl.pallas_call(
        _mla_prep_kernel,
        grid=(b, nt),
        in_specs=[pl.BlockSpec((tm, mla.shape[1]), lambda bi, ti: (bi * nt + ti, 0)),
                  pl.BlockSpec((tm, 1), lambda bi, ti: (bi * nt + ti, 0))] + [full(a) for a in consts],
        out_specs=[hspec, hspec, pl.BlockSpec((tm, vw), lambda bi, ti: (bi * nt + ti, 0))],
        out_shape=[hshape, hshape, jax.ShapeDtypeStruct((b * t, vw), BF16)],
        compiler_params=_params(("parallel", "parallel")),
        name="mla_prep",
    )(mla, pos, *consts)


def _attn_kernel(*refs, sparse, n_sel, tq, scale, idx_scale, t_total):
    if sparse:
        q_ref, k_ref, vt_ref, qi_ref, ki_ref, wt_ref, o_ref, l_scr, key_scr, bias_scr, sel_scr = refs
    else:
        q_ref, k_ref, vt_ref, o_ref, l_scr = refs
    n_heads, dv = vt_ref.shape[1], vt_ref.shape[3]
    qb = pl.program_id(1)
    nkc = qb + 1
    kio = lax.broadcasted_iota(I32, (tq, tq), 0)
    qio = lax.broadcasted_iota(I32, (tq, tq), 1)
    neg_inf = -jnp.inf

    def chunk_off(c):
        return pl.multiple_of(c * tq, tq)

    def causal(c):
        return (c * tq + kio) <= (qb * tq + qio)

    if sparse:
        n_idx_heads = qi_ref.shape[1]

        def score_chunk(c, carry):
            koff = chunk_off(c)
            ki = ki_ref[0, pl.ds(koff, tq), :]
            acc = jnp.zeros((tq, tq), F32)
            for h in range(n_idx_heads):
                s = lax.dot_general(ki, qi_ref[0, h], NT_DIMS, preferred_element_type=F32)
                acc = acc + jnp.maximum(s, 0.0) * wt_ref[0, h:h + 1, :]
            sc = jnp.where(causal(c), acc * idx_scale, neg_inf)
            sc = jnp.where(sc == 0.0, 0.0, sc)
            bits = pltpu.bitcast(sc, I32)
            key_scr[pl.ds(koff, tq), :] = jnp.where(bits < 0, bits ^ 0x7FFFFFFF, bits)
            return carry

        lax.fori_loop(0, nkc, score_chunk, 0)

        def count(pred):
            def body(c, acc):
                x = key_scr[pl.ds(chunk_off(c), tq), :]
                m = jnp.where(pred(x, c), 1.0, 0.0)
                return acc + jnp.sum(m.reshape(tq // SUBLANES, SUBLANES, tq), axis=0)
            acc = lax.fori_loop(0, nkc, body, jnp.zeros((SUBLANES, tq), F32))
            return jnp.sum(acc, axis=0, keepdims=True)

        def count_ge(cand):
            return count(lambda x, c: x >= cand)

        k_sel = jnp.float32(n_sel)
        zero = jnp.zeros((1, tq), I32)
        thr = jnp.where(count_ge(zero) >= k_sel, zero, jnp.full((1, tq), INT_MIN, I32))

        def bit_step(i, thr):
            cand = thr | lax.shift_left(jnp.int32(1), jnp.int32(30) - i)
            return jnp.where(count_ge(cand) >= k_sel, cand, thr)

        thr = lax.fori_loop(0, 31, bit_step, thr)
        n_ge = count_ge(thr)
        sel_scr[0:1, :] = thr
        sel_scr[1:2, :] = jnp.full((1, tq), t_total, I32)

        @pl.when(jnp.max(n_ge) > k_sel)
        def _():
            need = k_sel - count(lambda x, c: x > thr)

            def below(x, c, m):
                return jnp.logical_and(x == thr, (c * tq + kio) < m)

            def idx_step(i, lo):
                cand = lo | lax.shift_left(jnp.int32(1), jnp.int32(t_total.bit_length() - 1) - i)
                return jnp.where(count(lambda x, c: below(x, c, cand)) < need, cand, lo)

            lo = lax.fori_loop(0, t_total.bit_length(), idx_step, zero)
            sel_scr[1:2, :] = lo + 1

        thr = sel_scr[0:1, :]
        tie_end = sel_scr[1:2, :]

        def bias_chunk(c, carry):
            koff = chunk_off(c)
            x = key_scr[pl.ds(koff, tq), :]
            tie_ok = jnp.logical_and(x == thr, (c * tq + kio) < tie_end)
            keep = jnp.logical_and(jnp.logical_or(x > thr, tie_ok), causal(c))
            bias_scr[pl.ds(koff, tq), :] = jnp.where(keep, 0.0, neg_inf)
            return carry

        lax.fori_loop(0, nkc, bias_chunk, 0)

    def head(h, carry):
        q = q_ref[0, h]

        def logits_chunk(c, m):
            koff = chunk_off(c)
            s = lax.dot_general(k_ref[0, h, pl.ds(koff, tq), :], q, NT_DIMS,
                                preferred_element_type=F32) * scale
            if sparse:
                s = s + bias_scr[pl.ds(koff, tq), :]
            else:
                s = jnp.where(causal(c), s, neg_inf)
            l_scr[pl.ds(koff, tq), :] = s
            return jnp.maximum(m, jnp.max(s, axis=0, keepdims=True))

        m = lax.fori_loop(0, nkc, logits_chunk, jnp.full((1, tq), neg_inf, F32))

        def pv_chunk(c, acc_l):
            acc, l = acc_l
            p = jnp.exp(l_scr[pl.ds(chunk_off(c), tq), :] - m)
            l = l + jnp.sum(p, axis=0, keepdims=True)
            acc = acc + jnp.dot(vt_ref[0, h, c], p.astype(BF16), preferred_element_type=F32)
            return acc, l

        acc, l = lax.fori_loop(0, nkc, pv_chunk, (jnp.zeros((dv, tq), F32), jnp.zeros((1, tq), F32)))
        o_ref[0, h] = (acc / l).astype(o_ref.dtype)
        return carry

    lax.fori_loop(0, n_heads, head, 0)


def _attention(q, k, v, b, t, scale, sparse_inputs=None, n_sel=0):
    h, dq = q.shape[1], q.shape[3]
    dv = v.shape[1] // h
    tq = _tile(t, 256, LANES)
    nq = t // tq
    vt = v.reshape(b, nq, tq, h, dv).transpose(0, 3, 1, 4, 2)
    in_specs = [pl.BlockSpec((1, h, tq, dq), lambda bi, qi: (bi, 0, qi, 0)),
                pl.BlockSpec((1, h, t, dq), lambda bi, qi: (bi, 0, 0, 0)),
                pl.BlockSpec((1, h, nq, dv, tq), lambda bi, qi: (bi, 0, 0, 0, 0))]
    args = [q, k, vt]
    scratch = [pltpu.VMEM((t, tq), F32)]
    sparse = sparse_inputs is not None
    if sparse:
        qi_, ki_, wt_ = sparse_inputs
        hi, dk = qi_.shape[1], qi_.shape[3]
        in_specs += [pl.BlockSpec((1, hi, tq, dk), lambda bi, qi: (bi, 0, qi, 0)),
                     pl.BlockSpec((1, t, dk), lambda bi, qi: (bi, 0, 0)),
                     pl.BlockSpec((1, hi, tq), lambda bi, qi: (bi, 0, qi))]
        args += [qi_, ki_, wt_]
        scratch += [pltpu.VMEM((t, tq), I32), pltpu.VMEM((t, tq), F32), pltpu.VMEM((SUBLANES, tq), I32)]
    out = pl.pallas_call(
        functools.partial(_attn_kernel, sparse=sparse, n_sel=n_sel, tq=tq, scale=scale,
                          idx_scale=IDX_HEADS ** -0.5 * IDX_HEAD_DIM ** -0.5, t_total=t),
        grid=(b, nq),
        in_specs=in_specs,
        out_specs=pl.BlockSpec((1, h, dv, tq), lambda bi, qi: (bi, 0, 0, qi)),
        out_shape=jax.ShapeDtypeStruct((b, h, dv, t), BF16),
        scratch_shapes=scratch,
        compiler_params=_params(("parallel", "arbitrary")),
        name="sparse_attention" if sparse else "dense_attention",
    )(*args)
    return out.transpose(0, 3, 1, 2).reshape(b * t, h * dv)


def _merge_kernel(hn_ref, a_ref, b_ref, c_ref, wg_ref, wb_ref, o_ref):
    hn = hn_ref[...]
    acc = None
    for n, br_ref in enumerate((a_ref, b_ref, c_ref)):
        gate = jax.nn.sigmoid(jnp.dot(hn, wg_ref[n], preferred_element_type=F32))
        term = gate * jnp.dot(br_ref[...], wb_ref[n], preferred_element_type=F32)
        acc = term if acc is None else acc + term
    o_ref[...] = acc.astype(o_ref.dtype)


def _merge(hn, branches, w_gate, w_branch):
    m, d = hn.shape
    bw = w_branch.shape[1]
    tm = _tile(m, 512, SUBLANES)
    tn = _tile(d, 512)
    bspec = pl.BlockSpec((tm, bw), lambda i, j: (i, 0))
    return pl.pallas_call(
        _merge_kernel,
        grid=(m // tm, d // tn),
        in_specs=[pl.BlockSpec((tm, d), lambda i, j: (i, 0)), bspec, bspec, bspec,
                  pl.BlockSpec((N_BRANCH, d, tn), lambda i, j: (0, 0, j)),
                  pl.BlockSpec((N_BRANCH, bw, tn), lambda i, j: (0, 0, j))],
        out_specs=pl.BlockSpec((tm, tn), lambda i, j: (i, j)),
        out_shape=jax.ShapeDtypeStruct((m, d), BF16),
        compiler_params=_params(("parallel", "arbitrary")),
        name="gated_merge",
    )(hn, *branches, w_gate, w_branch)


def _ple_kernel(h_ref, hn_ref, p_ref, wg_ref, wp_ref, o_ref):
    gate = jax.nn.sigmoid(jnp.dot(hn_ref[...], wg_ref[...], preferred_element_type=F32))
    emb = jnp.dot(p_ref[...].astype(BF16), wp_ref[...], preferred_element_type=F32)
    o_ref[...] = h_ref[...] + emb * gate


def _ple(h, hn, p, w_gate, w_proj):
    m, d = h.shape
    pd = p.shape[1]
    tm = _tile(m, 1024, SUBLANES)
    tn = _tile(d, 512)
    return pl.pallas_call(
        _ple_kernel,
        grid=(m // tm, d // tn),
        in_specs=[pl.BlockSpec((tm, tn), lambda i, j: (i, j)), pl.BlockSpec((tm, d), lambda i, j: (i, 0)),
                  pl.BlockSpec((tm, pd), lambda i, j: (i, 0)), pl.BlockSpec((d, tn), lambda i, j: (0, j)),
                  pl.BlockSpec((pd, tn), lambda i, j: (0, j))],
        out_specs=pl.BlockSpec((tm, tn), lambda i, j: (i, j)),
        out_shape=jax.ShapeDtypeStruct((m, d), F32),
        compiler_params=_params(("parallel", "arbitrary")),
        name="ple",
    )(h, hn, p, w_gate, w_proj)


def _ffn_kernel(be_ref, nu_ref, x_ref, wg_ref, wu_ref, wd_ref, *rest, has_res):
    if has_res:
        res_ref, o_ref, acc_ref = rest
    else:
        o_ref, acc_ref = rest
    i, j = pl.program_id(0), pl.program_id(1)
    last = pl.num_programs(1) - 1
    used = i < nu_ref[0]

    @pl.when(used)
    def _():
        x = x_ref[...].astype(BF16)
        g = jnp.dot(x, wg_ref[0], preferred_element_type=F32)
        u = jnp.dot(x, wu_ref[0], preferred_element_type=F32)
        act = (g * jax.nn.sigmoid(g) * u).astype(BF16)
        d = jnp.dot(act, wd_ref[0], preferred_element_type=F32)

        @pl.when(j == 0)
        def _():
            acc_ref[...] = d

        @pl.when(jnp.logical_and(j > 0, j < last))
        def _():
            acc_ref[...] += d

        @pl.when(j == last)
        def _():
            total = acc_ref[...] + d
            o_ref[...] = (res_ref[...] + total) if has_res else total

    @pl.when(jnp.logical_and(jnp.logical_not(used), j == last))
    def _():
        o_ref[...] = jnp.zeros_like(o_ref)


def _ffn(x, w_gate, w_up, w_down, blk_expert, n_used, tm, res=None):
    r, d = x.shape
    f = w_gate.shape[2]
    tf = _tile(f, 512)
    nf = f // tf
    assert nf >= 2

    def jj(i, j, nu):
        return jnp.where(i < nu[0], j, nf - 1)

    in_specs = [pl.BlockSpec((tm, d), lambda i, j, be, nu: (i, 0)),
                pl.BlockSpec((1, d, tf), lambda i, j, be, nu: (be[i], 0, jj(i, j, nu))),
                pl.BlockSpec((1, d, tf), lambda i, j, be, nu: (be[i], 0, jj(i, j, nu))),
                pl.BlockSpec((1, tf, d), lambda i, j, be, nu: (be[i], jj(i, j, nu), 0))]
    args = [x, w_gate, w_up, w_down]
    if res is not None:
        in_specs.append(pl.BlockSpec((tm, d), lambda i, j, be, nu: (i, 0)))
        args.append(res)
    return pl.pallas_call(
        functools.partial(_ffn_kernel, has_res=res is not None),
        grid_spec=pltpu.PrefetchScalarGridSpec(
            num_scalar_prefetch=2,
            grid=(r // tm, nf),
            in_specs=in_specs,
            out_specs=pl.BlockSpec((tm, d), lambda i, j, be, nu: (i, 0)),
            scratch_shapes=[pltpu.VMEM((tm, d), F32)]),
        out_shape=jax.ShapeDtypeStruct((r, d), F32),
        compiler_params=_params(("arbitrary", "arbitrary")),
        name="swiglu",
    )(blk_expert, n_used, *args)


def _router_kernel(h_ref, g_ref, rhi_ref, rlo_ref, hn_ref, info_ref, cnt_ref, run_scr, *, n_experts):
    i = pl.program_id(0)
    tm = h_ref.shape[0]

    @pl.when(i == 0)
    def _():
        run_scr[...] = jnp.zeros_like(run_scr)

    x = h_ref[...]
    xn = x * lax.rsqrt(jnp.mean(x * x, axis=-1, keepdims=True) + EPS) * g_ref[...]
    hi = xn.astype(BF16)
    hn_ref[...] = xn
    lo = (xn - hi.astype(F32)).astype(BF16)
    logits = (jnp.dot(hi, rhi_ref[...], preferred_element_type=F32)
              + jnp.dot(hi, rlo_ref[...], preferred_element_type=F32)
              + jnp.dot(lo, rhi_ref[...], preferred_element_type=F32))
    lane = lax.broadcasted_iota(I32, logits.shape, 1)
    lane_f = lane.astype(F32)
    logits = jnp.where(lane < n_experts, logits, -jnp.inf)

    def top(vals):
        v = jnp.max(vals, axis=-1, keepdims=True)
        idx = jnp.min(jnp.where(vals == v, lane_f, float(LANES)), axis=-1, keepdims=True)
        return v, idx

    v1, i1 = top(logits)
    v2, i2 = top(jnp.where(lane_f == i1, -jnp.inf, logits))
    e2 = jnp.exp(v2 - v1)
    g1 = 1.0 / (1.0 + e2)
    g2 = e2 / (1.0 + e2)

    oh1 = lane_f == i1
    oh2 = lane_f == i2
    both = jnp.where(jnp.logical_or(oh1, oh2), 1.0, 0.0)
    r_io = lax.broadcasted_iota(I32, (tm, tm), 0)
    c_io = lax.broadcasted_iota(I32, (tm, tm), 1)
    strict_lower = jnp.where(c_io < r_io, 1.0, 0.0).astype(BF16)
    before = jnp.dot(strict_lower, both.astype(BF16), preferred_element_type=F32) + run_scr[0:1, :]
    rank1 = jnp.sum(jnp.where(oh1, before, 0.0), axis=-1, keepdims=True)
    rank2 = jnp.sum(jnp.where(oh2, before, 0.0), axis=-1, keepdims=True)
    run_scr[0:1, :] = run_scr[0:1, :] + jnp.sum(both, axis=0, keepdims=True)

    info = jnp.zeros(logits.shape, F32)
    for col, val in enumerate((i1, i2, g1, g2, rank1, rank2)):
        info = jnp.where(lane == col, val, info)
    info_ref[...] = info
    cnt_ref[...] = jnp.broadcast_to(run_scr[0:1, :], cnt_ref.shape)


def _router(h, g, router):
    m, d = h.shape
    e = router.shape[1]
    tm = _tile(m, 256, SUBLANES)
    rp = jnp.zeros((d, LANES), F32).at[:, :e].set(router)
    rhi = rp.astype(BF16)
    rlo = (rp - rhi.astype(F32)).astype(BF16)
    return pl.pallas_call(
        functools.partial(_router_kernel, n_experts=e),
        grid=(m // tm,),
        in_specs=[pl.BlockSpec((tm, d), lambda i: (i, 0)), pl.BlockSpec((1, d), lambda i: (0, 0)),
                  pl.BlockSpec((d, LANES), lambda i: (0, 0)), pl.BlockSpec((d, LANES), lambda i: (0, 0))],
        out_specs=[pl.BlockSpec((tm, d), lambda i: (i, 0)), pl.BlockSpec((tm, LANES), lambda i: (i, 0)),
                   pl.BlockSpec((SUBLANES, LANES), lambda i: (0, 0))],
        out_shape=[jax.ShapeDtypeStruct((m, d), F32), jax.ShapeDtypeStruct((m, LANES), F32),
                   jax.ShapeDtypeStruct((SUBLANES, LANES), F32)],
        scratch_shapes=[pltpu.VMEM((SUBLANES, LANES), F32)],
        compiler_params=_params(("arbitrary",)),
        name="router",
    )(h, g.reshape(1, d), rhi, rlo)


def _row_copy(src, dst, sem, s, d):
    return pltpu.make_async_copy(src.at[pl.ds(s, 1), :], dst.at[pl.ds(d, 1), :], sem)


def _dispatch_kernel(dest_ref, x_hbm, xs_in_hbm, xs_hbm, sem, *, rows):
    del xs_in_hbm
    i = pl.program_id(0)

    def issue(r, carry):
        for kk in range(TOP_K):
            _row_copy(x_hbm, xs_hbm, sem, i * rows + r, dest_ref[0, 0, TOP_K * r + kk]).start()
        return carry

    lax.fori_loop(0, rows, issue, 0)

    def drain(r, carry):
        for kk in range(TOP_K):
            _row_copy(x_hbm, xs_hbm, sem, 0, 0).wait()
        return carry

    lax.fori_loop(0, rows, drain, 0)


def _dispatch(x, dest, n_rows):
    m, d = x.shape
    rows = _tile(m, 512, SUBLANES)
    return pl.pallas_call(
        functools.partial(_dispatch_kernel, rows=rows),
        grid=(m // rows,),
        in_specs=[pl.BlockSpec((1, 1, TOP_K * rows), lambda i: (i, 0, 0), memory_space=pltpu.SMEM),
                  pl.BlockSpec(memory_space=pl.ANY), pl.BlockSpec(memory_space=pl.ANY)],
        out_specs=pl.BlockSpec(memory_space=pl.ANY),
        out_shape=jax.ShapeDtypeStruct((n_rows, d), x.dtype),
        scratch_shapes=[pltpu.SemaphoreType.DMA(())],
        input_output_aliases={2: 0},
        compiler_params=_params(("arbitrary",)),
        name="moe_dispatch",
    )(dest.reshape(m // rows, 1, TOP_K * rows), x, jnp.zeros((n_rows, d), x.dtype))


def _combine_kernel(dest_ref, h_ref, info_ref, g_ref, ys_hbm, o_ref, on_ref, buf, sem, *, rows):
    def issue(r, carry):
        for kk in range(TOP_K):
            pltpu.make_async_copy(ys_hbm.at[pl.ds(dest_ref[0, 0, TOP_K * r + kk], 1), :],
                                  buf.at[kk, pl.ds(r, 1), :], sem).start()
        return carry

    lax.fori_loop(0, rows, issue, 0)

    def drain(r, carry):
        for kk in range(TOP_K):
            pltpu.make_async_copy(ys_hbm.at[pl.ds(0, 1), :], buf.at[kk, pl.ds(0, 1), :], sem).wait()
        return carry

    lax.fori_loop(0, rows, drain, 0)
    info = info_ref[...]
    out = h_ref[...] + (info[:, 2:3] * buf[0] + info[:, 3:4] * buf[1])
    o_ref[...] = out
    on_ref[...] = (out * lax.rsqrt(jnp.mean(out * out, axis=-1, keepdims=True) + EPS)
                   * g_ref[...]).astype(on_ref.dtype)


def _combine(h, info, dest, ys, g):
    m, d = h.shape
    rows = _tile(m, 256, SUBLANES)
    return pl.pallas_call(
        functools.partial(_combine_kernel, rows=rows),
        grid=(m // rows,),
        in_specs=[pl.BlockSpec((1, 1, TOP_K * rows), lambda i: (i, 0, 0), memory_space=pltpu.SMEM),
                  pl.BlockSpec((rows, d), lambda i: (i, 0)), pl.BlockSpec((rows, LANES), lambda i: (i, 0)),
                  pl.BlockSpec((1, d), lambda i: (0, 0)), pl.BlockSpec(memory_space=pl.ANY)],
        out_specs=[pl.BlockSpec((rows, d), lambda i: (i, 0)), pl.BlockSpec((rows, d), lambda i: (i, 0))],
        out_shape=[jax.ShapeDtypeStruct((m, d), F32), jax.ShapeDtypeStruct((m, d), BF16)],
        scratch_shapes=[pltpu.VMEM((TOP_K, rows, d), F32), pltpu.SemaphoreType.DMA(())],
        compiler_params=_params(("arbitrary",)),
        name="moe_combine",
    )(dest.reshape(m // rows, 1, TOP_K * rows), h, info, g.reshape(1, d), ys)


def _moe(h, g_ffn, router, w_gate, w_up, w_down, g_next):
    m, d = h.shape
    e = router.shape[1]
    hn, info, cnt = _router(h, g_ffn, router)
    counts = cnt[0, :e].astype(I32)
    padded = (counts + MOE_ROWS - 1) // MOE_ROWS * MOE_ROWS
    pad_ends = jnp.cumsum(padded)
    pad_starts = pad_ends - padded
    n_blocks = -(-(m * TOP_K) // MOE_ROWS) + e
    ids = info[:, 0:TOP_K].astype(I32)
    dest = jnp.sum(jnp.where(ids[:, :, None] == jnp.arange(e)[None, None, :], pad_starts[None, None, :], 0),
                   axis=-1) + info[:, 4:4 + TOP_K].astype(I32)
    blk_start = jnp.arange(n_blocks, dtype=I32) * MOE_ROWS
    blk_expert = jnp.minimum(jnp.sum(blk_start[:, None] >= pad_ends[None, :], axis=1), e - 1).astype(I32)
    n_used = (pad_ends[e - 1] // MOE_ROWS).astype(I32).reshape(1)
    xs = _dispatch(hn, dest, n_blocks * MOE_ROWS)
    ys = _ffn(xs, w_gate, w_up, w_down, blk_expert, n_used, MOE_ROWS)
    return _combine(h, info, dest, ys, g_next)


def _split_w_in(w):
    d = w.shape[0]
    c = LRU_WIDTH
    a = ATT_HEADS * ATT_HEAD_DIM
    o_att = 2 * c
    o_idx = o_att + 3 * a
    n_idx = IDX_HEADS * IDX_HEAD_DIM + IDX_HEAD_DIM + IDX_HEADS
    o_mla = o_idx + n_idx
    n_mla = MLA_Q_LORA + MLA_KV_LORA + MLA_ROPE
    o_gate = o_mla + n_mla

    def padded(x):
        pad = -x.shape[1] % LANES
        return jnp.pad(x, ((0, 0), (0, pad)))

    main = w[:, :o_att + 2 * a]
    w_v = w[:, o_att + 2 * a:o_idx]
    w_idx = padded(w[:, o_idx:o_mla])
    w_mla = padded(w[:, o_mla:o_gate])
    w_gates = w[:, o_gate:].reshape(d, N_BRANCH, d).transpose(1, 0, 2)
    return tuple(x.astype(BF16) for x in (main, w_v, w_idx, w_mla, w_gates))


def kernel(x, p, positions, ln_mix, w_in, conv_w, conv_b, lru_wa, lru_ba, lru_wx, lru_bx, lru_lambda,
           att_q_norm, att_k_norm, mla_qa_norm, mla_kva_norm, mla_w_uq, mla_w_ukv, mla_q_norm,
           mla_k_norm, w_branch, w_out, ln_ffn, dense_w_gate, dense_w_up, dense_w_down, moe_router,
           moe_w_gate, moe_w_up, moe_w_down, ple_norm, ple_w_gate, ple_w_proj):
    b, t, d = x.shape
    m = b * t
    depth = w_in.shape[0]
    n_sel = min(TOPK_MAX, t // 4)
    pos = positions.reshape(m, 1).astype(I32)
    h = x.reshape(m, d)
    idx_w_off = IDX_HEADS * IDX_HEAD_DIM + IDX_HEAD_DIM
    for i in range(depth):
        w_main, w_v, w_idx, w_mla, w_gates = _split_w_in(w_in[i])
        hn = _rmsnorm(h, ln_mix[i])
        proj = _matmul(hn, w_main, F32, name="in_proj_main")
        v = _matmul(hn, w_v, BF16, name="in_proj_v")
        idx = _matmul(hn, w_idx, F32, name="in_proj_idx")
        mla = _matmul(hn, w_mla, F32, name="in_proj_mla")

        out_a = _rglru(proj, b, t, conv_w[i], conv_b[i], lru_wa[i], lru_ba[i], lru_wx[i], lru_bx[i],
                       lru_lambda[i])

        q, k = _qk_prep(proj, pos, b, t, att_q_norm[i], att_k_norm[i])
        qi, ki = _idx_prep(idx, pos, b, t)
        wt = idx[:, idx_w_off:idx_w_off + IDX_HEADS].reshape(b, t, IDX_HEADS).transpose(0, 2, 1)
        out_b = _attention(q, k, v, b, t, ATT_HEAD_DIM ** -0.5, sparse_inputs=(qi, ki, wt), n_sel=n_sel)

        mq, mk, mv = _mla_prep(mla, pos, b, t, mla_qa_norm[i], mla_kva_norm[i], mla_w_uq[i],
                               mla_w_ukv[i], mla_q_norm[i], mla_k_norm[i])
        out_c = _attention(mq, mk, mv, b, t, MLA_QK ** -0.5)

        merged = _merge(hn, (out_a, out_b, out_c), w_gates, w_branch[i].astype(BF16))
        h = _matmul(merged, w_out[i].astype(BF16), F32, res=h, name="out_proj")

        if i % 2 == 0:
            j = i // 2
            hn2 = _rmsnorm(h, ln_ffn[i])
            tm = _tile(m, 512, SUBLANES)
            h = _ffn(hn2, dense_w_gate[j:j + 1].astype(BF16), dense_w_up[j:j + 1].astype(BF16),
                     dense_w_down[j:j + 1].astype(BF16), jnp.zeros((m // tm,), I32),
                     jnp.full((1,), m // tm, I32), tm, res=h)
            hn3 = _rmsnorm(h, ple_norm[i])
        else:
            j = i // 2
            h, hn3 = _moe(h, ln_ffn[i], moe_router[j], moe_w_gate[j].astype(BF16),
                          moe_w_up[j].astype(BF16), moe_w_down[j].astype(BF16), ple_norm[i])
        h = _ple(h, hn3, p[i].reshape(m, -1), ple_w_gate[i].astype(BF16), ple_w_proj[i].astype(BF16))
    return h.reshape(b, t, d)
```

```python
import functools

import jax
import jax.numpy as jnp
from jax import lax
from jax.experimental import pallas as pl
from jax.experimental.pallas import tpu as pltpu

F32, BF16, I32 = jnp.float32, jnp.bfloat16, jnp.int32

EPS = 1e-6
ROPE_THETA = 10000.0
LRU_C = 8.0
LRU_WIDTH = 1024
LRU_BLOCKS = 8
ATT_HEADS = 8
ATT_HEAD_DIM = 128
IDX_HEADS = 16
IDX_HEAD_DIM = 64
TOPK_MAX = 256
MLA_HEADS = 8
MLA_Q_LORA = 768
MLA_KV_LORA = 512
MLA_NOPE = 128
MLA_ROPE = 64
MLA_QK = MLA_NOPE + MLA_ROPE
MLA_V = 128
N_BRANCH = 3
N_EXPERTS = 8
TOP_K = 2

LANES = 128
SUBLANES = 8
VMEM_LIMIT_BYTES = 56 << 20
MOE_ROWS = 512
INT_MIN = -(2 ** 31)
NEG_INF_KEY = 0x807FFFFF - 2 ** 32

NT_DIMS = (((1,), (1,)), ((), ()))


def _params(sem):
    return pltpu.CompilerParams(dimension_semantics=sem, vmem_limit_bytes=VMEM_LIMIT_BYTES)


def _tile(n, cap, unit=LANES):
    if n <= cap:
        return n
    best = None
    for t in range(unit, cap + 1, unit):
        if n % t == 0:
            best = t
    assert best is not None, (n, cap)
    return best


def _rmsnorm_kernel(x_ref, g_ref, o_ref):
    x = x_ref[...]
    ms = jnp.mean(x * x, axis=-1, keepdims=True)
    o_ref[...] = (x * lax.rsqrt(ms + EPS) * g_ref[...]).astype(o_ref.dtype)


def _rmsnorm(x, g):
    m, d = x.shape
    tm = _tile(m, 512, SUBLANES)
    return pl.pallas_call(
        _rmsnorm_kernel,
        grid=(m // tm,),
        in_specs=[pl.BlockSpec((tm, d), lambda i: (i, 0)), pl.BlockSpec((1, d), lambda i: (0, 0))],
        out_specs=pl.BlockSpec((tm, d), lambda i: (i, 0)),
        out_shape=jax.ShapeDtypeStruct((m, d), BF16),
        compiler_params=_params(("parallel",)),
        name="rmsnorm",
    )(x, g.reshape(1, d))


def _mm_kernel(x_ref, w_ref, o_ref):
    o_ref[...] = jnp.dot(x_ref[...], w_ref[...], preferred_element_type=F32).astype(o_ref.dtype)


def _mm_res_kernel(x_ref, w_ref, r_ref, o_ref):
    o_ref[...] = r_ref[...] + jnp.dot(x_ref[...], w_ref[...], preferred_element_type=F32)


def _matmul(x, w, out_dtype, res=None, tm_cap=1024, tn_cap=512, name="matmul"):
    m, k = x.shape
    n = w.shape[1]
    tm = _tile(m, tm_cap, SUBLANES)
    tn = _tile(n, tn_cap)
    in_specs = [pl.BlockSpec((tm, k), lambda i, j: (i, 0)), pl.BlockSpec((k, tn), lambda i, j: (0, j))]
    args = [x, w]
    body = _mm_kernel
    if res is not None:
        in_specs.append(pl.BlockSpec((tm, tn), lambda i, j: (i, j)))
        args.append(res)
        body = _mm_res_kernel
    return pl.pallas_call(
        body,
        grid=(m // tm, n // tn),
        in_specs=in_specs,
        out_specs=pl.BlockSpec((tm, tn), lambda i, j: (i, j)),
        out_shape=jax.ShapeDtypeStruct((m, n), out_dtype),
        compiler_params=_params(("parallel", "arbitrary")),
        name=name,
    )(*args)


EXPM1_SERIES_BOUND = 0.25
EXPM1_SERIES_TERMS = 10


def _expm1(y):
    poly = jnp.full_like(y, 1.0 / 3628800.0)
    fact = 3628800.0
    for n in range(EXPM1_SERIES_TERMS, 1, -1):
        fact /= n
        poly = poly * y + 1.0 / fact
    return jnp.where(jnp.abs(y) < EXPM1_SERIES_BOUND, poly * y, jnp.exp(y) - 1.0)


def _rglru_kernel(x_ref, g_ref, cw_ref, cb_ref, wa_ref, ba_ref, wx_ref, bx_ref, lam_ref, o_ref,
                  xs_scr, a_scr, b_scr, h_scr, *, tt):
    c = x_ref.shape[1]
    t = pl.program_id(1)

    @pl.when(t == 0)
    def _():
        xs_scr[0:SUBLANES, :] = jnp.zeros((SUBLANES, c), F32)
        h_scr[...] = jnp.zeros_like(h_scr)

    x = x_ref[...]
    xs_scr[SUBLANES:SUBLANES + tt, :] = x
    cw = cw_ref[...]
    xc = (xs_scr[SUBLANES - 3:SUBLANES - 3 + tt, :] * cw[0:1, :]
          + xs_scr[SUBLANES - 2:SUBLANES - 2 + tt, :] * cw[1:2, :]
          + xs_scr[SUBLANES - 1:SUBLANES - 1 + tt, :] * cw[2:3, :]
          + x * cw[3:4, :]) + cb_ref[...]
    xs_scr[0:SUBLANES, :] = x[tt - SUBLANES:tt, :]

    xcb = xc.astype(BF16)
    bw = c // LRU_BLOCKS
    ra = jnp.concatenate(
        [jnp.dot(xcb[:, n * bw:(n + 1) * bw], wa_ref[n], preferred_element_type=F32)
         for n in range(LRU_BLOCKS)], axis=1) + ba_ref[...]
    rx = jnp.concatenate(
        [jnp.dot(xcb[:, n * bw:(n + 1) * bw], wx_ref[n], preferred_element_type=F32)
         for n in range(LRU_BLOCKS)], axis=1) + bx_ref[...]
    r = jax.nn.sigmoid(ra)
    gi = jax.nn.sigmoid(rx)
    nlam = -lam_ref[...]
    softplus = jnp.maximum(nlam, 0.0) + jnp.log1p(jnp.exp(-jnp.abs(nlam)))
    log_a = (-LRU_C) * r * softplus
    a_scr[...] = jnp.exp(log_a)
    b_scr[...] = jnp.sqrt(-_expm1(2.0 * log_a)) * (gi * xc)

    row = lax.broadcasted_iota(I32, (SUBLANES, c), 0)

    def group(gidx, h):
        off = pl.multiple_of(gidx * SUBLANES, SUBLANES)
        a8 = a_scr[pl.ds(off, SUBLANES), :]
        b8 = b_scr[pl.ds(off, SUBLANES), :]
        for s in (1, 2, 4):
            keep = row >= s
            a_sh = jnp.where(keep, pltpu.roll(a8, s, 0), 1.0)
            b_sh = jnp.where(keep, pltpu.roll(b8, s, 0), 0.0)
            b8 = a8 * b_sh + b8
            a8 = a8 * a_sh
        h8 = a8 * h + b8
        b_scr[pl.ds(off, SUBLANES), :] = h8
        return h8[SUBLANES - 1:SUBLANES, :]

    h_scr[...] = lax.fori_loop(0, tt // SUBLANES, group, h_scr[...])
    o_ref[...] = (b_scr[...] * jax.nn.gelu(g_ref[...], approximate=True)).astype(o_ref.dtype)


def _rglru(proj, b, t, conv_w, conv_b, wa, ba, wx, bx, lam):
    c = LRU_WIDTH
    tt = _tile(t, 256, SUBLANES)
    nt = t // tt
    row = lambda v: v.reshape(1, c)
    wspec = pl.BlockSpec(wa.shape, lambda bi, ti: (0, 0, 0))
    vspec = pl.BlockSpec((1, c), lambda bi, ti: (0, 0))
    return pl.pallas_call(
        functools.partial(_rglru_kernel, tt=tt),
        grid=(b, nt),
        in_specs=[pl.BlockSpec((tt, c), lambda bi, ti: (bi * nt + ti, 0)),
                  pl.BlockSpec((tt, c), lambda bi, ti: (bi * nt + ti, 1)),
                  pl.BlockSpec(conv_w.shape, lambda bi, ti: (0, 0)), vspec,
                  wspec, vspec, wspec, vspec, vspec],
        out_specs=pl.BlockSpec((tt, c), lambda bi, ti: (bi * nt + ti, 0)),
        out_shape=jax.ShapeDtypeStruct((b * t, c), BF16),
        scratch_shapes=[pltpu.VMEM((tt + SUBLANES, c), F32), pltpu.VMEM((tt, c), F32),
                        pltpu.VMEM((tt, c), F32), pltpu.VMEM((1, c), F32)],
        compiler_params=_params(("parallel", "arbitrary")),
        name="rglru",
    )(proj, proj, conv_w, row(conv_b), wa.astype(BF16), row(ba), wx.astype(BF16), row(bx), row(lam))


def _inv_freq_lanes(d):
    f = ROPE_THETA ** (-jnp.arange(0, d, 2, dtype=F32) / d)
    return jnp.tile(jnp.concatenate([f, f]), LANES // d).reshape(1, LANES)


def _rope_tables(pos_ref, invf_ref, half):
    ang = pos_ref[...].astype(F32) * invf_ref[...]
    lane = lax.broadcasted_iota(I32, ang.shape, 1)
    first = (lane & (2 * half - 1)) < half
    return jnp.cos(ang), jnp.where(first, -jnp.sin(ang), jnp.sin(ang)), first


def _swap_halves(x, half, first):
    if 2 * half == LANES:
        return pltpu.roll(x, half, 1)
    return jnp.where(first, pltpu.roll(x, LANES - half, 1), pltpu.roll(x, half, 1))


def _qk_prep_kernel(x_ref, pos_ref, invf_ref, gq_ref, gk_ref, q_ref, k_ref):
    cosf, sinf, first = _rope_tables(pos_ref, invf_ref, ATT_HEAD_DIM // 2)
    for which, g_ref, o_ref in ((0, gq_ref, q_ref), (1, gk_ref, k_ref)):
        for h in range(ATT_HEADS):
            lo = (which * ATT_HEADS + h) * ATT_HEAD_DIM
            s = x_ref[:, lo:lo + ATT_HEAD_DIM]
            y = s * lax.rsqrt(jnp.mean(s * s, axis=-1, keepdims=True) + EPS) * g_ref[...]
            y = y * cosf + _swap_halves(y, ATT_HEAD_DIM // 2, first) * sinf
            o_ref[0, h] = y.astype(o_ref.dtype)


def _qk_prep(proj, pos, b, t, gq, gk):
    tm = _tile(t, 256, SUBLANES)
    nt = t // tm
    hd = ATT_HEADS * ATT_HEAD_DIM
    ospec = pl.BlockSpec((1, ATT_HEADS, tm, ATT_HEAD_DIM), lambda bi, ti: (bi, 0, ti, 0))
    oshape = jax.ShapeDtypeStruct((b, ATT_HEADS, t, ATT_HEAD_DIM), BF16)
    vspec = pl.BlockSpec((1, LANES), lambda bi, ti: (0, 0))
    return pl.pallas_call(
        _qk_prep_kernel,
        grid=(b, nt),
        in_specs=[pl.BlockSpec((tm, 2 * hd), lambda bi, ti: (bi * nt + ti, 1)),
                  pl.BlockSpec((tm, 1), lambda bi, ti: (bi * nt + ti, 0)), vspec, vspec, vspec],
        out_specs=[ospec, ospec],
        out_shape=[oshape, oshape],
        compiler_params=_params(("parallel", "parallel")),
        name="qk_prep",
    )(proj, pos, _inv_freq_lanes(ATT_HEAD_DIM), gq.reshape(1, LANES), gk.reshape(1, LANES))


def _idx_prep_kernel(x_ref, pos_ref, invf_ref, qi_ref, ki_ref):
    half = IDX_HEAD_DIM // 2
    cosf, sinf, first = _rope_tables(pos_ref, invf_ref, half)
    lane = lax.broadcasted_iota(I32, cosf.shape, 1)
    left = lane < IDX_HEAD_DIM

    def rope(x):
        return x * cosf + _swap_halves(x, half, first) * sinf

    def split(y):
        hi = y.astype(BF16).astype(F32)
        return hi, y - hi

    for j in range(IDX_HEADS // 2):
        hi, lo = split(rope(x_ref[:, j * LANES:(j + 1) * LANES]))
        even = jnp.where(left, hi, pltpu.roll(lo, IDX_HEAD_DIM, 1)).astype(BF16)
        odd = jnp.where(left, pltpu.roll(hi, IDX_HEAD_DIM, 1), lo).astype(BF16)
        for h, v in ((2 * j, even), (2 * j + 1, odd)):
            qi_ref[0, h, :, 0:LANES] = v
            qi_ref[0, h, :, LANES:2 * LANES] = v
    kcol = IDX_HEADS * IDX_HEAD_DIM
    khi, klo = split(rope(x_ref[:, kcol:kcol + LANES]))
    ki_ref[0, :, 0:LANES] = jnp.where(left, khi, pltpu.roll(khi, IDX_HEAD_DIM, 1)).astype(BF16)
    ki_ref[0, :, LANES:2 * LANES] = jnp.where(left, klo, pltpu.roll(klo, IDX_HEAD_DIM, 1)).astype(BF16)


def _idx_prep(idx, pos, b, t):
    tm = _tile(t, 256, SUBLANES)
    nt = t // tm
    w = idx.shape[1]
    return pl.pallas_call(
        _idx_prep_kernel,
        grid=(b, nt),
        in_specs=[pl.BlockSpec((tm, w), lambda bi, ti: (bi * nt + ti, 0)),
                  pl.BlockSpec((tm, 1), lambda bi, ti: (bi * nt + ti, 0)),
                  pl.BlockSpec((1, LANES), lambda bi, ti: (0, 0))],
        out_specs=[pl.BlockSpec((1, IDX_HEADS, tm, 2 * LANES), lambda bi, ti: (bi, 0, ti, 0)),
                   pl.BlockSpec((1, tm, 2 * LANES), lambda bi, ti: (bi, ti, 0))],
        out_shape=[jax.ShapeDtypeStruct((b, IDX_HEADS, t, 2 * LANES), BF16),
                   jax.ShapeDtypeStruct((b, t, 2 * LANES), BF16)],
        compiler_params=_params(("parallel", "parallel")),
        name="idx_prep",
    )(idx, pos, _inv_freq_lanes(IDX_HEAD_DIM))


def _mla_prep_kernel(m_ref, pos_ref, invf_ref, qa_ref, kva_ref, wuq_ref, wukv_ref,
                     qnn_ref, qnr_ref, knn_ref, knr_ref, q_ref, k_ref, v_ref):
    half = MLA_ROPE // 2
    cosf, sinf, first = _rope_tables(pos_ref, invf_ref, half)
    lane = lax.broadcasted_iota(I32, cosf.shape, 1)
    left = lane < MLA_ROPE

    def rope(x):
        return x * cosf + _swap_halves(x, half, first) * sinf

    def norm(x, g_ref):
        return (x * lax.rsqrt(jnp.mean(x * x, axis=-1, keepdims=True) + EPS) * g_ref[...]).astype(BF16)

    cq = norm(m_ref[:, 0:MLA_Q_LORA], qa_ref)
    ckv = norm(m_ref[:, MLA_Q_LORA:MLA_Q_LORA + MLA_KV_LORA], kva_ref)
    kr = m_ref[:, MLA_Q_LORA + MLA_KV_LORA:MLA_Q_LORA + MLA_KV_LORA + LANES]
    qf = jnp.dot(cq, wuq_ref[...], preferred_element_type=F32)
    kvf = jnp.dot(ckv, wukv_ref[...], preferred_element_type=F32)
    nope_w = MLA_HEADS * MLA_NOPE
    v_ref[...] = kvf[:, nope_w:nope_w + MLA_HEADS * MLA_V].astype(v_ref.dtype)

    for j in range(MLA_HEADS // 2):
        rs = qf[:, nope_w + j * LANES:nope_w + (j + 1) * LANES]
        sq = rs * rs
        ss_pair = (jnp.sum(jnp.where(left, sq, 0.0), axis=-1, keepdims=True),
                   jnp.sum(jnp.where(left, 0.0, sq), axis=-1, keepdims=True))
        for par in range(2):
            h = 2 * j + par
            nope = qf[:, h * MLA_NOPE:(h + 1) * MLA_NOPE]
            ms = (jnp.sum(nope * nope, axis=-1, keepdims=True) + ss_pair[par]) * (1.0 / MLA_QK)
            rsq = lax.rsqrt(ms + EPS)
            q_ref[0, h, :, 0:LANES] = (nope * rsq * qnn_ref[...]).astype(q_ref.dtype)
            rr = rope(rs * rsq * qnr_ref[...])
            if par == 1:
                rr = pltpu.roll(rr, MLA_ROPE, 1)
            q_ref[0, h, :, LANES:2 * LANES] = jnp.where(left, rr, 0.0).astype(q_ref.dtype)

    ss_kr = jnp.sum(jnp.where(left, kr * kr, 0.0), axis=-1, keepdims=True)
    base = jnp.where(left, rope(kr * knr_ref[...]), 0.0)
    for h in range(MLA_HEADS):
        nope = kvf[:, h * MLA_NOPE:(h + 1) * MLA_NOPE]
        ms = (jnp.sum(nope * nope, axis=-1, keepdims=True) + ss_kr) * (1.0 / MLA_QK)
        rsq = lax.rsqrt(ms + EPS)
        k_ref[0, h, :, 0:LANES] = (nope * rsq * knn_ref[...]).astype(k_ref.dtype)
        k_ref[0, h, :, LANES:2 * LANES] = (base * rsq).astype(k_ref.dtype)


def _mla_prep(mla, pos, b, t, qa, kva, w_uq, w_ukv, qn, kn):
    tm = _tile(t, 256, SUBLANES)
    nt = t // tm
    wq = w_uq.reshape(MLA_Q_LORA, MLA_HEADS, MLA_QK)
    wq = jnp.concatenate([wq[:, :, :MLA_NOPE].reshape(MLA_Q_LORA, -1),
                          wq[:, :, MLA_NOPE:].reshape(MLA_Q_LORA, -1)], axis=1).astype(BF16)
    wkv = w_ukv.reshape(MLA_KV_LORA, MLA_HEADS, MLA_NOPE + MLA_V)
    wkv = jnp.concatenate([wkv[:, :, :MLA_NOPE].reshape(MLA_KV_LORA, -1),
                           wkv[:, :, MLA_NOPE:].reshape(MLA_KV_LORA, -1)], axis=1).astype(BF16)
    dup = lambda g: jnp.tile(g[MLA_NOPE:], 2).reshape(1, LANES)
    full = lambda a: pl.BlockSpec(a.shape, lambda bi, ti: (0,) * a.ndim)
    consts = [_inv_freq_lanes(MLA_ROPE), qa.reshape(1, -1), kva.reshape(1, -1), wq, wkv,
              qn[:MLA_NOPE].reshape(1, LANES), dup(qn), kn[:MLA_NOPE].reshape(1, LANES), dup(kn)]
    hspec = pl.BlockSpec((1, MLA_HEADS, tm, 2 * LANES), lambda bi, ti: (bi, 0, ti, 0))
    hshape = jax.ShapeDtypeStruct((b, MLA_HEADS, t, 2 * LANES), BF16)
    vw = MLA_HEADS * MLA_V
    return pl.pallas_call(
        _mla_prep_kernel,
        grid=(b, nt),
        in_specs=[pl.BlockSpec((tm, mla.shape[1]), lambda bi, ti: (bi * nt + ti, 0)),
                  pl.BlockSpec((tm, 1), lambda bi, ti: (bi * nt + ti, 0))] + [full(a) for a in consts],
        out_specs=[hspec, hspec, pl.BlockSpec((tm, vw), lambda bi, ti: (bi * nt + ti, 0))],
        out_shape=[hshape, hshape, jax.ShapeDtypeStruct((b * t, vw), BF16)],
        compiler_params=_params(("parallel", "parallel")),
        name="mla_prep",
    )(mla, pos, *consts)


def _attn_kernel(*refs, sparse, n_sel, tq, scale, idx_scale, t_total):
    if sparse:
        (q_ref, k_ref, vt_ref, qi_ref, ki_ref, wt_ref, o_ref, m_scr, l_scr, acc_scr,
         score_scr, bias_scr, thr_scr, tie_scr) = refs
    else:
        q_ref, k_ref, vt_ref, o_ref, m_scr, l_scr, acc_scr = refs
    n_heads, dv = vt_ref.shape[1], vt_ref.shape[3]
    qb = pl.program_id(1)
    nkc = qb + 1
    kio = lax.broadcasted_iota(I32, (tq, tq), 0)
    qio = lax.broadcasted_iota(I32, (tq, tq), 1)
    neg_inf = -jnp.inf

    def chunk_off(c):
        return pl.multiple_of(c * tq, tq)

    def causal(c):
        return (c * tq + kio) <= (qb * tq + qio)

    if sparse:
        n_idx_heads = qi_ref.shape[1]

        def score_chunk(c, carry):
            koff = chunk_off(c)
            ki = ki_ref[0, pl.ds(koff, tq), :]
            acc = jnp.zeros((tq, tq), F32)
            for h in range(n_idx_heads):
                s = lax.dot_general(ki, qi_ref[0, h], NT_DIMS, preferred_element_type=F32)
                acc = acc + jnp.maximum(s, 0.0) * wt_ref[0, h:h + 1, :]
            score_scr[pl.ds(koff, tq), :] = jnp.where(causal(c), acc * idx_scale, neg_inf)
            return carry

        lax.fori_loop(0, nkc, score_chunk, 0)

        def key_to_float(key):
            val = pltpu.bitcast(jnp.where(key < 0, key ^ 0x7FFFFFFF, key), F32)
            return jnp.where(key < NEG_INF_KEY, neg_inf, val)

        def count(pred):
            def body(c, acc):
                x = score_scr[pl.ds(chunk_off(c), tq), :]
                m = jnp.where(pred(x, c), 1.0, 0.0)
                return acc + jnp.sum(m.reshape(tq // SUBLANES, SUBLANES, tq), axis=0)
            acc = lax.fori_loop(0, nkc, body, jnp.zeros((SUBLANES, tq), F32))
            return jnp.sum(acc, axis=0, keepdims=True)

        def count_ge(key):
            cand = key_to_float(key)
            return count(lambda x, c: x >= cand)

        k_sel = jnp.float32(n_sel)
        zero = jnp.zeros((1, tq), I32)
        thr_key = jnp.where(count_ge(zero) >= k_sel, zero, jnp.full((1, tq), INT_MIN, I32))

        def bit_step(i, key):
            cand = key | lax.shift_left(jnp.int32(1), jnp.int32(30) - i)
            return jnp.where(count_ge(cand) >= k_sel, cand, key)

        thr_key = lax.fori_loop(0, 31, bit_step, thr_key)
        thr = key_to_float(thr_key)
        n_ge = count(lambda x, c: x >= thr)
        thr_scr[0:1, :] = thr
        tie_scr[0:1, :] = jnp.full((1, tq), t_total, I32)

        @pl.when(jnp.max(n_ge) > k_sel)
        def _():
            need = k_sel - count(lambda x, c: x > thr)

            def below(x, c, m):
                return jnp.logical_and(x == thr, (c * tq + kio) < m)

            def idx_step(i, lo):
                cand = lo | lax.shift_left(jnp.int32(1), jnp.int32(t_total.bit_length() - 1) - i)
                return jnp.where(count(lambda x, c: below(x, c, cand)) < need, cand, lo)

            lo = lax.fori_loop(0, t_total.bit_length(), idx_step, zero)
            tie_scr[0:1, :] = lo + 1

        thr = thr_scr[0:1, :]
        tie_end = tie_scr[0:1, :]

        def bias_chunk(c, carry):
            koff = chunk_off(c)
            x = score_scr[pl.ds(koff, tq), :]
            tie_ok = jnp.logical_and(x == thr, (c * tq + kio) < tie_end)
            keep = jnp.logical_and(jnp.logical_or(x > thr, tie_ok), causal(c))
            bias_scr[pl.ds(koff, tq), :] = jnp.where(keep, 0.0, neg_inf)
            return carry

        lax.fori_loop(0, nkc, bias_chunk, 0)

    m_scr[...] = jnp.full(m_scr.shape, neg_inf, F32)
    l_scr[...] = jnp.zeros(l_scr.shape, F32)
    acc_scr[...] = jnp.zeros(acc_scr.shape, F32)

    def chunk_step(c, masked):
        koff = chunk_off(c)
        if sparse:
            bias = bias_scr[pl.ds(koff, tq), :]
        elif masked:
            allowed = causal(c)
        def qk(h):
            return lax.dot_general(k_ref[0, h, pl.ds(koff, tq), :], q_ref[0, h], NT_DIMS,
                                   preferred_element_type=F32)

        s_next = qk(0)
        for h in range(n_heads):
            s = s_next * scale
            if h + 1 < n_heads:
                s_next = qk(h + 1)
            if sparse:
                s = s + bias
            elif masked:
                s = jnp.where(allowed, s, neg_inf)
            m_old = m_scr[h]
            m_new = jnp.maximum(m_old, jnp.max(s, axis=0, keepdims=True))
            m_ref = jnp.where(m_new == neg_inf, 0.0, m_new)
            alpha = jnp.exp(m_old - m_ref)
            p = jnp.exp(s - m_ref)
            l_scr[h] = alpha * l_scr[h] + jnp.sum(p, axis=0, keepdims=True)
            acc_scr[h] = alpha * acc_scr[h] + jnp.dot(vt_ref[0, h, c], p.astype(BF16),
                                                      preferred_element_type=F32)
            m_scr[h] = m_new

    def off_diagonal(c, carry):
        chunk_step(c, False)
        return carry

    if sparse:
        lax.fori_loop(0, nkc, off_diagonal, 0)
    else:
        lax.fori_loop(0, qb, off_diagonal, 0)
        chunk_step(qb, True)
    for h in range(n_heads):
        o_ref[0, h] = (acc_scr[h] / l_scr[h]).astype(o_ref.dtype)


def _attention(q, k, v, b, t, scale, sparse_inputs=None, n_sel=0):
    h, dq = q.shape[1], q.shape[3]
    dv = v.shape[1] // h
    tq = _tile(t, 256, LANES)
    nq = t // tq
    vt = v.reshape(b, nq, tq, h, dv).transpose(0, 3, 1, 4, 2)
    in_specs = [pl.BlockSpec((1, h, tq, dq), lambda bi, qi: (bi, 0, qi, 0)),
                pl.BlockSpec((1, h, t, dq), lambda bi, qi: (bi, 0, 0, 0)),
                pl.BlockSpec((1, h, nq, dv, tq), lambda bi, qi: (bi, 0, 0, 0, 0))]
    args = [q, k, vt]
    scratch = [pltpu.VMEM((h, 1, tq), F32), pltpu.VMEM((h, 1, tq), F32), pltpu.VMEM((h, dv, tq), F32)]
    sparse = sparse_inputs is not None
    if sparse:
        qi_, ki_, wt_ = sparse_inputs
        hi, dk = qi_.shape[1], qi_.shape[3]
        in_specs += [pl.BlockSpec((1, hi, tq, dk), lambda bi, qi: (bi, 0, qi, 0)),
                     pl.BlockSpec((1, t, dk), lambda bi, qi: (bi, 0, 0)),
                     pl.BlockSpec((1, hi, tq), lambda bi, qi: (bi, 0, qi))]
        args += [qi_, ki_, wt_]
        scratch += [pltpu.VMEM((t, tq), F32), pltpu.VMEM((t, tq), F32),
                    pltpu.VMEM((SUBLANES, tq), F32), pltpu.VMEM((SUBLANES, tq), I32)]
    out = pl.pallas_call(
        functools.partial(_attn_kernel, sparse=sparse, n_sel=n_sel, tq=tq, scale=scale,
                          idx_scale=IDX_HEADS ** -0.5 * IDX_HEAD_DIM ** -0.5, t_total=t),
        grid=(b, nq),
        in_specs=in_specs,
        out_specs=pl.BlockSpec((1, h, dv, tq), lambda bi, qi: (bi, 0, 0, qi)),
        out_shape=jax.ShapeDtypeStruct((b, h, dv, t), BF16),
        scratch_shapes=scratch,
        compiler_params=_params(("parallel", "arbitrary")),
        name="sparse_attention" if sparse else "dense_attention",
    )(*args)
    return out.transpose(0, 3, 1, 2).reshape(b * t, h * dv)


def _merge_kernel(hn_ref, a_ref, b_ref, c_ref, wg_ref, wb_ref, o_ref):
    hn = hn_ref[...]
    acc = None
    for n, br_ref in enumerate((a_ref, b_ref, c_ref)):
        gate = jax.nn.sigmoid(jnp.dot(hn, wg_ref[n], preferred_element_type=F32))
        term = gate * jnp.dot(br_ref[...], wb_ref[n], preferred_element_type=F32)
        acc = term if acc is None else acc + term
    o_ref[...] = acc.astype(o_ref.dtype)


def _merge(hn, branches, w_gate, w_branch):
    m, d = hn.shape
    bw = w_branch.shape[1]
    tm = _tile(m, 512, SUBLANES)
    tn = _tile(d, 512)
    bspec = pl.BlockSpec((tm, bw), lambda i, j: (i, 0))
    return pl.pallas_call(
        _merge_kernel,
        grid=(m // tm, d // tn),
        in_specs=[pl.BlockSpec((tm, d), lambda i, j: (i, 0)), bspec, bspec, bspec,
                  pl.BlockSpec((N_BRANCH, d, tn), lambda i, j: (0, 0, j)),
                  pl.BlockSpec((N_BRANCH, bw, tn), lambda i, j: (0, 0, j))],
        out_specs=pl.BlockSpec((tm, tn), lambda i, j: (i, j)),
        out_shape=jax.ShapeDtypeStruct((m, d), BF16),
        compiler_params=_params(("parallel", "arbitrary")),
        name="gated_merge",
    )(hn, *branches, w_gate, w_branch)


def _ple_kernel(h_ref, hn_ref, p_ref, wg_ref, wp_ref, o_ref):
    gate = jax.nn.sigmoid(jnp.dot(hn_ref[...], wg_ref[...], preferred_element_type=F32))
    emb = jnp.dot(p_ref[...].astype(BF16), wp_ref[...], preferred_element_type=F32)
    o_ref[...] = h_ref[...] + emb * gate


def _ple(h, hn, p, w_gate, w_proj):
    m, d = h.shape
    pd = p.shape[1]
    tm = _tile(m, 1024, SUBLANES)
    tn = _tile(d, 512)
    return pl.pallas_call(
        _ple_kernel,
        grid=(m // tm, d // tn),
        in_specs=[pl.BlockSpec((tm, tn), lambda i, j: (i, j)), pl.BlockSpec((tm, d), lambda i, j: (i, 0)),
                  pl.BlockSpec((tm, pd), lambda i, j: (i, 0)), pl.BlockSpec((d, tn), lambda i, j: (0, j)),
                  pl.BlockSpec((pd, tn), lambda i, j: (0, j))],
        out_specs=pl.BlockSpec((tm, tn), lambda i, j: (i, j)),
        out_shape=jax.ShapeDtypeStruct((m, d), F32),
        compiler_params=_params(("parallel", "arbitrary")),
        name="ple",
    )(h, hn, p, w_gate, w_proj)


def _ffn_kernel(be_ref, nu_ref, x_ref, wg_ref, wu_ref, wd_ref, *rest, has_res):
    if has_res:
        res_ref, o_ref = rest
    else:
        (o_ref,) = rest
    i, j = pl.program_id(0), pl.program_id(1)
    used = i < nu_ref[0]

    @pl.when(used)
    def _():
        x = x_ref[...].astype(BF16)
        g = jnp.dot(x, wg_ref[0].astype(BF16), preferred_element_type=F32)
        u = jnp.dot(x, wu_ref[0].astype(BF16), preferred_element_type=F32)
        act = (g * jax.nn.sigmoid(g) * u).astype(BF16)
        d = jnp.dot(act, wd_ref[0].astype(BF16), preferred_element_type=F32)

        @pl.when(j == 0)
        def _():
            o_ref[...] = (res_ref[...] + d) if has_res else d

        @pl.when(j > 0)
        def _():
            o_ref[...] += d

    @pl.when(jnp.logical_and(jnp.logical_not(used), j == 0))
    def _():
        o_ref[...] = jnp.zeros_like(o_ref)


def _ffn(x, w_gate, w_up, w_down, blk_expert, n_used, tm, res=None):
    r, d = x.shape
    f = w_gate.shape[2]
    tf = _tile(f, 512)
    nf = f // tf
    assert nf >= 2

    def jj(i, j, nu):
        return jnp.where(i < nu[0], j, nf - 1)

    in_specs = [pl.BlockSpec((tm, d), lambda i, j, be, nu: (i, 0)),
                pl.BlockSpec((1, d, tf), lambda i, j, be, nu: (be[i], 0, jj(i, j, nu))),
                pl.BlockSpec((1, d, tf), lambda i, j, be, nu: (be[i], 0, jj(i, j, nu))),
                pl.BlockSpec((1, tf, d), lambda i, j, be, nu: (be[i], jj(i, j, nu), 0))]
    args = [x, w_gate, w_up, w_down]
    if res is not None:
        in_specs.append(pl.BlockSpec((tm, d), lambda i, j, be, nu: (i, 0)))
        args.append(res)
    return pl.pallas_call(
        functools.partial(_ffn_kernel, has_res=res is not None),
        grid_spec=pltpu.PrefetchScalarGridSpec(
            num_scalar_prefetch=2,
            grid=(r // tm, nf),
            in_specs=in_specs,
            out_specs=pl.BlockSpec((tm, d), lambda i, j, be, nu: (i, 0))),
        out_shape=jax.ShapeDtypeStruct((r, d), F32),
        compiler_params=_params(("arbitrary", "arbitrary")),
        name="swiglu",
    )(blk_expert, n_used, *args)


def _router_kernel(h_ref, g_ref, rhi_ref, rlo_ref, hn_ref, info_ref, cnt_ref, run_scr, *, n_experts):
    i = pl.program_id(0)
    tm = h_ref.shape[0]

    @pl.when(i == 0)
    def _():
        run_scr[...] = jnp.zeros_like(run_scr)

    x = h_ref[...]
    xn = x * lax.rsqrt(jnp.mean(x * x, axis=-1, keepdims=True) + EPS) * g_ref[...]
    hi = xn.astype(BF16)
    hn_ref[...] = xn
    lo = (xn - hi.astype(F32)).astype(BF16)
    logits = (jnp.dot(hi, rhi_ref[...], preferred_element_type=F32)
              + jnp.dot(hi, rlo_ref[...], preferred_element_type=F32)
              + jnp.dot(lo, rhi_ref[...], preferred_element_type=F32))
    lane = lax.broadcasted_iota(I32, logits.shape, 1)
    lane_f = lane.astype(F32)
    logits = jnp.where(lane < n_experts, logits, -jnp.inf)

    def top(vals):
        v = jnp.max(vals, axis=-1, keepdims=True)
        idx = jnp.min(jnp.where(vals == v, lane_f, float(LANES)), axis=-1, keepdims=True)
        return v, idx

    v1, i1 = top(logits)
    v2, i2 = top(jnp.where(lane_f == i1, -jnp.inf, logits))
    e2 = jnp.exp(v2 - v1)
    g1 = 1.0 / (1.0 + e2)
    g2 = e2 / (1.0 + e2)

    oh1 = lane_f == i1
    oh2 = lane_f == i2
    both = jnp.where(jnp.logical_or(oh1, oh2), 1.0, 0.0)
    r_io = lax.broadcasted_iota(I32, (tm, tm), 0)
    c_io = lax.broadcasted_iota(I32, (tm, tm), 1)
    strict_lower = jnp.where(c_io < r_io, 1.0, 0.0).astype(BF16)
    before = jnp.dot(strict_lower, both.astype(BF16), preferred_element_type=F32) + run_scr[0:1, :]
    rank1 = jnp.sum(jnp.where(oh1, before, 0.0), axis=-1, keepdims=True)
    rank2 = jnp.sum(jnp.where(oh2, before, 0.0), axis=-1, keepdims=True)
    run_scr[0:1, :] = run_scr[0:1, :] + jnp.sum(both, axis=0, keepdims=True)

    info = jnp.zeros(logits.shape, F32)
    for col, val in enumerate((i1, i2, g1, g2, rank1, rank2)):
        info = jnp.where(lane == col, val, info)
    info_ref[...] = info
    cnt_ref[...] = jnp.broadcast_to(run_scr[0:1, :], cnt_ref.shape)


def _router(h, g, router):
    m, d = h.shape
    e = router.shape[1]
    tm = _tile(m, 256, SUBLANES)
    rp = jnp.zeros((d, LANES), F32).at[:, :e].set(router)
    rhi = rp.astype(BF16)
    rlo = (rp - rhi.astype(F32)).astype(BF16)
    return pl.pallas_call(
        functools.partial(_router_kernel, n_experts=e),
        grid=(m // tm,),
        in_specs=[pl.BlockSpec((tm, d), lambda i: (i, 0)), pl.BlockSpec((1, d), lambda i: (0, 0)),
                  pl.BlockSpec((d, LANES), lambda i: (0, 0)), pl.BlockSpec((d, LANES), lambda i: (0, 0))],
        out_specs=[pl.BlockSpec((tm, d), lambda i: (i, 0)), pl.BlockSpec((tm, LANES), lambda i: (i, 0)),
                   pl.BlockSpec((SUBLANES, LANES), lambda i: (0, 0))],
        out_shape=[jax.ShapeDtypeStruct((m, d), F32), jax.ShapeDtypeStruct((m, LANES), F32),
                   jax.ShapeDtypeStruct((SUBLANES, LANES), F32)],
        scratch_shapes=[pltpu.VMEM((SUBLANES, LANES), F32)],
        compiler_params=_params(("arbitrary",)),
        name="router",
    )(h, g.reshape(1, d), rhi, rlo)


def _row_copy(src, dst, sem, s, d):
    return pltpu.make_async_copy(src.at[pl.ds(s, 1), :], dst.at[pl.ds(d, 1), :], sem)


def _dispatch_kernel(dest_ref, x_ref, xs_in_hbm, xs_hbm, sem, *, rows):
    del xs_in_hbm

    def issue(r, carry):
        for kk in range(TOP_K):
            _row_copy(x_ref, xs_hbm, sem, r, dest_ref[0, 0, TOP_K * r + kk]).start()
        return carry

    lax.fori_loop(0, rows, issue, 0)

    def drain(r, carry):
        for kk in range(TOP_K):
            _row_copy(x_ref, xs_hbm, sem, 0, 0).wait()
        return carry

    lax.fori_loop(0, rows, drain, 0)


def _dispatch(x, dest, n_rows):
    m, d = x.shape
    rows = _tile(m, 512, SUBLANES)
    return pl.pallas_call(
        functools.partial(_dispatch_kernel, rows=rows),
        grid=(m // rows,),
        in_specs=[pl.BlockSpec((1, 1, TOP_K * rows), lambda i: (i, 0, 0), memory_space=pltpu.SMEM),
                  pl.BlockSpec((rows, d), lambda i: (i, 0)), pl.BlockSpec(memory_space=pl.ANY)],
        out_specs=pl.BlockSpec(memory_space=pl.ANY),
        out_shape=jax.ShapeDtypeStruct((n_rows, d), x.dtype),
        scratch_shapes=[pltpu.SemaphoreType.DMA(())],
        input_output_aliases={2: 0},
        compiler_params=_params(("arbitrary",)),
        name="moe_dispatch",
    )(dest.reshape(m // rows, 1, TOP_K * rows), x, jnp.zeros((n_rows, d), x.dtype))


def _combine_kernel(dest_ref, h_ref, info_ref, g_ref, ys_hbm, o_ref, on_ref, buf, sem, *, rows):
    def issue(r, carry):
        for kk in range(TOP_K):
            pltpu.make_async_copy(ys_hbm.at[pl.ds(dest_ref[0, 0, TOP_K * r + kk], 1), :],
                                  buf.at[kk, pl.ds(r, 1), :], sem).start()
        return carry

    lax.fori_loop(0, rows, issue, 0)

    def drain(r, carry):
        for kk in range(TOP_K):
            pltpu.make_async_copy(ys_hbm.at[pl.ds(0, 1), :], buf.at[kk, pl.ds(0, 1), :], sem).wait()
        return carry

    lax.fori_loop(0, rows, drain, 0)
    info = info_ref[...]
    out = h_ref[...] + (info[:, 2:3] * buf[0] + info[:, 3:4] * buf[1])
    o_ref[...] = out
    on_ref[...] = (out * lax.rsqrt(jnp.mean(out * out, axis=-1, keepdims=True) + EPS)
                   * g_ref[...]).astype(on_ref.dtype)


def _combine(h, info, dest, ys, g):
    m, d = h.shape
    rows = _tile(m, 256, SUBLANES)
    return pl.pallas_call(
        functools.partial(_combine_kernel, rows=rows),
        grid=(m // rows,),
        in_specs=[pl.BlockSpec((1, 1, TOP_K * rows), lambda i: (i, 0, 0), memory_space=pltpu.SMEM),
                  pl.BlockSpec((rows, d), lambda i: (i, 0)), pl.BlockSpec((rows, LANES), lambda i: (i, 0)),
                  pl.BlockSpec((1, d), lambda i: (0, 0)), pl.BlockSpec(memory_space=pl.ANY)],
        out_specs=[pl.BlockSpec((rows, d), lambda i: (i, 0)), pl.BlockSpec((rows, d), lambda i: (i, 0))],
        out_shape=[jax.ShapeDtypeStruct((m, d), F32), jax.ShapeDtypeStruct((m, d), BF16)],
        scratch_shapes=[pltpu.VMEM((TOP_K, rows, d), F32), pltpu.SemaphoreType.DMA(())],
        compiler_params=_params(("arbitrary",)),
        name="moe_combine",
    )(dest.reshape(m // rows, 1, TOP_K * rows), h, info, g.reshape(1, d), ys)


def _moe(h, g_ffn, router, w_gate, w_up, w_down, g_next):
    m, d = h.shape
    e = router.shape[1]
    hn, info, cnt = _router(h, g_ffn, router)
    counts = cnt[0, :e].astype(I32)
    padded = (counts + MOE_ROWS - 1) // MOE_ROWS * MOE_ROWS
    pad_ends = jnp.cumsum(padded)
    pad_starts = pad_ends - padded
    n_blocks = -(-(m * TOP_K) // MOE_ROWS) + e
    ids = info[:, 0:TOP_K].astype(I32)
    dest = jnp.sum(jnp.where(ids[:, :, None] == jnp.arange(e)[None, None, :], pad_starts[None, None, :], 0),
                   axis=-1) + info[:, 4:4 + TOP_K].astype(I32)
    blk_start = jnp.arange(n_blocks, dtype=I32) * MOE_ROWS
    blk_expert = jnp.minimum(jnp.sum(blk_start[:, None] >= pad_ends[None, :], axis=1), e - 1).astype(I32)
    n_used = (pad_ends[e - 1] // MOE_ROWS).astype(I32).reshape(1)
    xs = _dispatch(hn, dest, n_blocks * MOE_ROWS)
    ys = _ffn(xs, w_gate, w_up, w_down, blk_expert, n_used, MOE_ROWS)
    return _combine(h, info, dest, ys, g_next)


def _split_w_in(w):
    d = w.shape[0]
    c = LRU_WIDTH
    a = ATT_HEADS * ATT_HEAD_DIM
    o_att = 2 * c
    o_idx = o_att + 3 * a
    n_idx = IDX_HEADS * IDX_HEAD_DIM + IDX_HEAD_DIM + IDX_HEADS
    o_mla = o_idx + n_idx
    n_mla = MLA_Q_LORA + MLA_KV_LORA + MLA_ROPE
    o_gate = o_mla + n_mla

    def padded(x):
        pad = -x.shape[1] % LANES
        return jnp.pad(x, ((0, 0), (0, pad)))

    main = w[:, :o_att + 2 * a]
    w_v = w[:, o_att + 2 * a:o_idx]
    w_idx = padded(w[:, o_idx:o_mla])
    w_mla = padded(w[:, o_mla:o_gate])
    w_gates = w[:, o_gate:].reshape(d, N_BRANCH, d).transpose(1, 0, 2)
    return tuple(x.astype(BF16) for x in (main, w_v, w_idx, w_mla, w_gates))


def kernel(x, p, positions, ln_mix, w_in, conv_w, conv_b, lru_wa, lru_ba, lru_wx, lru_bx, lru_lambda,
           att_q_norm, att_k_norm, mla_qa_norm, mla_kva_norm, mla_w_uq, mla_w_ukv, mla_q_norm,
           mla_k_norm, w_branch, w_out, ln_ffn, dense_w_gate, dense_w_up, dense_w_down, moe_router,
           moe_w_gate, moe_w_up, moe_w_down, ple_norm, ple_w_gate, ple_w_proj):
    b, t, d = x.shape
    m = b * t
    depth = w_in.shape[0]
    n_sel = min(TOPK_MAX, t // 4)
    pos = positions.reshape(m, 1).astype(I32)
    h = x.reshape(m, d)
    idx_w_off = IDX_HEADS * IDX_HEAD_DIM + IDX_HEAD_DIM
    for i in range(depth):
        w_main, w_v, w_idx, w_mla, w_gates = _split_w_in(w_in[i])
        hn = _rmsnorm(h, ln_mix[i])
        proj = _matmul(hn, w_main, F32, name="in_proj_main")
        v = _matmul(hn, w_v, BF16, name="in_proj_v")
        idx = _matmul(hn, w_idx, F32, name="in_proj_idx")
        mla = _matmul(hn, w_mla, F32, name="in_proj_mla")

        out_a = _rglru(proj, b, t, conv_w[i], conv_b[i], lru_wa[i], lru_ba[i], lru_wx[i], lru_bx[i],
                       lru_lambda[i])

        q, k = _qk_prep(proj, pos, b, t, att_q_norm[i], att_k_norm[i])
        qi, ki = _idx_prep(idx, pos, b, t)
        wt = idx[:, idx_w_off:idx_w_off + IDX_HEADS].reshape(b, t, IDX_HEADS).transpose(0, 2, 1)
        out_b = _attention(q, k, v, b, t, ATT_HEAD_DIM ** -0.5, sparse_inputs=(qi, ki, wt), n_sel=n_sel)

        mq, mk, mv = _mla_prep(mla, pos, b, t, mla_qa_norm[i], mla_kva_norm[i], mla_w_uq[i],
                               mla_w_ukv[i], mla_q_norm[i], mla_k_norm[i])
        out_c = _attention(mq, mk, mv, b, t, MLA_QK ** -0.5)

        merged = _merge(hn, (out_a, out_b, out_c), w_gates, w_branch[i].astype(BF16))
        h = _matmul(merged, w_out[i].astype(BF16), F32, res=h, name="out_proj")

        if i % 2 == 0:
            j = i // 2
            hn2 = _rmsnorm(h, ln_ffn[i])
            tm = _tile(m, 512, SUBLANES)
            h = _ffn(hn2, dense_w_gate[j:j + 1].astype(BF16), dense_w_up[j:j + 1].astype(BF16),
                     dense_w_down[j:j + 1].astype(BF16), jnp.zeros((m // tm,), I32),
                     jnp.full((1,), m // tm, I32), tm, res=h)
            hn3 = _rmsnorm(h, ple_norm[i])
        else:
            j = i // 2
            h, hn3 = _moe(h, ln_ffn[i], moe_router[j], moe_w_gate[j], moe_w_up[j], moe_w_down[j],
                          ple_norm[i])
        h = _ple(h, hn3, p[i].reshape(m, -1), ple_w_gate[i].astype(BF16), ple_w_proj[i].astype(BF16))
    return h.reshape(b, t, d)
```

```python
import functools

import jax
import jax.numpy as jnp
from jax import lax
from jax.experimental import pallas as pl
from jax.experimental.pallas import tpu as pltpu

F32, BF16, I32 = jnp.float32, jnp.bfloat16, jnp.int32

EPS = 1e-6
ROPE_THETA = 10000.0
LRU_C = 8.0
LRU_WIDTH = 1024
LRU_BLOCKS = 8
ATT_HEADS = 8
ATT_HEAD_DIM = 128
IDX_HEADS = 16
IDX_HEAD_DIM = 64
TOPK_MAX = 256
MLA_HEADS = 8
MLA_Q_LORA = 768
MLA_KV_LORA = 512
MLA_NOPE = 128
MLA_ROPE = 64
MLA_QK = MLA_NOPE + MLA_ROPE
MLA_V = 128
N_BRANCH = 3
N_EXPERTS = 8
TOP_K = 2

LANES = 128
SUBLANES = 8
VMEM_LIMIT_BYTES = 56 << 20
MOE_ROWS = 768
INT_MIN = -(2 ** 31)
NEG_INF_KEY = 0x807FFFFF - 2 ** 32

NT_DIMS = (((1,), (1,)), ((), ()))
LOG2_E = 1.4426950408889634
QK_LOOKAHEAD = 4


def _params(sem):
    return pltpu.CompilerParams(dimension_semantics=sem, vmem_limit_bytes=VMEM_LIMIT_BYTES)


def _tile(n, cap, unit=LANES):
    if n <= cap:
        return n
    best = None
    for t in range(unit, cap + 1, unit):
        if n % t == 0:
            best = t
    assert best is not None, (n, cap)
    return best


def _rmsnorm_kernel(x_ref, g_ref, o_ref):
    x = x_ref[...]
    ms = jnp.mean(x * x, axis=-1, keepdims=True)
    o_ref[...] = (x * lax.rsqrt(ms + EPS) * g_ref[...]).astype(o_ref.dtype)


def _rmsnorm(x, g):
    m, d = x.shape
    tm = _tile(m, 512, SUBLANES)
    return pl.pallas_call(
        _rmsnorm_kernel,
        grid=(m // tm,),
        in_specs=[pl.BlockSpec((tm, d), lambda i: (i, 0)), pl.BlockSpec((1, d), lambda i: (0, 0))],
        out_specs=pl.BlockSpec((tm, d), lambda i: (i, 0)),
        out_shape=jax.ShapeDtypeStruct((m, d), BF16),
        compiler_params=_params(("parallel",)),
        name="rmsnorm",
    )(x, g.reshape(1, d))


def _mm_kernel(x_ref, w_ref, o_ref):
    o_ref[...] = jnp.dot(x_ref[...], w_ref[...], preferred_element_type=F32).astype(o_ref.dtype)


def _mm_res_kernel(x_ref, w_ref, r_ref, o_ref):
    o_ref[...] = r_ref[...] + jnp.dot(x_ref[...], w_ref[...], preferred_element_type=F32)


def _matmul(x, w, out_dtype, res=None, tm_cap=1024, tn_cap=1024, name="matmul"):
    m, k = x.shape
    n = w.shape[1]
    tm = _tile(m, tm_cap, SUBLANES)
    tn = _tile(n, tn_cap)
    in_specs = [pl.BlockSpec((tm, k), lambda i, j: (i, 0)), pl.BlockSpec((k, tn), lambda i, j: (0, j))]
    args = [x, w]
    body = _mm_kernel
    if res is not None:
        in_specs.append(pl.BlockSpec((tm, tn), lambda i, j: (i, j)))
        args.append(res)
        body = _mm_res_kernel
    return pl.pallas_call(
        body,
        grid=(m // tm, n // tn),
        in_specs=in_specs,
        out_specs=pl.BlockSpec((tm, tn), lambda i, j: (i, j)),
        out_shape=jax.ShapeDtypeStruct((m, n), out_dtype),
        compiler_params=_params(("parallel", "arbitrary")),
        name=name,
    )(*args)


EXPM1_SERIES_BOUND = 0.25
EXPM1_SERIES_TERMS = 10


def _expm1(y):
    poly = jnp.full_like(y, 1.0 / 3628800.0)
    fact = 3628800.0
    for n in range(EXPM1_SERIES_TERMS, 1, -1):
        fact /= n
        poly = poly * y + 1.0 / fact
    return jnp.where(jnp.abs(y) < EXPM1_SERIES_BOUND, poly * y, jnp.exp(y) - 1.0)


def _rglru_kernel(x_ref, g_ref, cw_ref, cb_ref, wa_ref, ba_ref, wx_ref, bx_ref, lam_ref, o_ref,
                  xs_scr, a_scr, b_scr, h_scr, *, tt):
    c = x_ref.shape[1]
    t = pl.program_id(1)

    @pl.when(t == 0)
    def _():
        xs_scr[0:SUBLANES, :] = jnp.zeros((SUBLANES, c), F32)
        h_scr[...] = jnp.zeros_like(h_scr)

    x = x_ref[...]
    xs_scr[SUBLANES:SUBLANES + tt, :] = x
    cw = cw_ref[...]
    xc = (xs_scr[SUBLANES - 3:SUBLANES - 3 + tt, :] * cw[0:1, :]
          + xs_scr[SUBLANES - 2:SUBLANES - 2 + tt, :] * cw[1:2, :]
          + xs_scr[SUBLANES - 1:SUBLANES - 1 + tt, :] * cw[2:3, :]
          + x * cw[3:4, :]) + cb_ref[...]
    xs_scr[0:SUBLANES, :] = x[tt - SUBLANES:tt, :]

    xcb = xc.astype(BF16)
    bw = c // LRU_BLOCKS
    ra = jnp.concatenate(
        [jnp.dot(xcb[:, n * bw:(n + 1) * bw], wa_ref[n], preferred_element_type=F32)
         for n in range(LRU_BLOCKS)], axis=1) + ba_ref[...]
    rx = jnp.concatenate(
        [jnp.dot(xcb[:, n * bw:(n + 1) * bw], wx_ref[n], preferred_element_type=F32)
         for n in range(LRU_BLOCKS)], axis=1) + bx_ref[...]
    r = jax.nn.sigmoid(ra)
    gi = jax.nn.sigmoid(rx)
    nlam = -lam_ref[...]
    softplus = jnp.maximum(nlam, 0.0) + jnp.log1p(jnp.exp(-jnp.abs(nlam)))
    log_a = (-LRU_C) * r * softplus
    a_scr[...] = jnp.exp(log_a)
    b_scr[...] = jnp.sqrt(-_expm1(2.0 * log_a)) * (gi * xc)

    row = lax.broadcasted_iota(I32, (SUBLANES, c), 0)

    def group(gidx, h):
        off = pl.multiple_of(gidx * SUBLANES, SUBLANES)
        a8 = a_scr[pl.ds(off, SUBLANES), :]
        b8 = b_scr[pl.ds(off, SUBLANES), :]
        for s in (1, 2, 4):
            keep = row >= s
            a_sh = jnp.where(keep, pltpu.roll(a8, s, 0), 1.0)
            b_sh = jnp.where(keep, pltpu.roll(b8, s, 0), 0.0)
            b8 = a8 * b_sh + b8
            a8 = a8 * a_sh
        h8 = a8 * h + b8
        b_scr[pl.ds(off, SUBLANES), :] = h8
        return h8[SUBLANES - 1:SUBLANES, :]

    h_scr[...] = lax.fori_loop(0, tt // SUBLANES, group, h_scr[...])
    o_ref[...] = (b_scr[...] * jax.nn.gelu(g_ref[...], approximate=True)).astype(o_ref.dtype)


def _rglru(proj, b, t, conv_w, conv_b, wa, ba, wx, bx, lam):
    c = LRU_WIDTH
    tt = _tile(t, 256, SUBLANES)
    nt = t // tt
    row = lambda v: v.reshape(1, c)
    wspec = pl.BlockSpec(wa.shape, lambda bi, ti: (0, 0, 0))
    vspec = pl.BlockSpec((1, c), lambda bi, ti: (0, 0))
    return pl.pallas_call(
        functools.partial(_rglru_kernel, tt=tt),
        grid=(b, nt),
        in_specs=[pl.BlockSpec((tt, c), lambda bi, ti: (bi * nt + ti, 0)),
                  pl.BlockSpec((tt, c), lambda bi, ti: (bi * nt + ti, 1)),
                  pl.BlockSpec(conv_w.shape, lambda bi, ti: (0, 0)), vspec,
                  wspec, vspec, wspec, vspec, vspec],
        out_specs=pl.BlockSpec((tt, c), lambda bi, ti: (bi * nt + ti, 0)),
        out_shape=jax.ShapeDtypeStruct((b * t, c), BF16),
        scratch_shapes=[pltpu.VMEM((tt + SUBLANES, c), F32), pltpu.VMEM((tt, c), F32),
                        pltpu.VMEM((tt, c), F32), pltpu.VMEM((1, c), F32)],
        compiler_params=_params(("parallel", "arbitrary")),
        name="rglru",
    )(proj, proj, conv_w, row(conv_b), wa.astype(BF16), row(ba), wx.astype(BF16), row(bx), row(lam))


def _inv_freq_lanes(d):
    f = ROPE_THETA ** (-jnp.arange(0, d, 2, dtype=F32) / d)
    return jnp.tile(jnp.concatenate([f, f]), LANES // d).reshape(1, LANES)


def _rope_tables(pos_ref, invf_ref, half):
    ang = pos_ref[...].astype(F32) * invf_ref[...]
    lane = lax.broadcasted_iota(I32, ang.shape, 1)
    first = (lane & (2 * half - 1)) < half
    return jnp.cos(ang), jnp.where(first, -jnp.sin(ang), jnp.sin(ang)), first


def _swap_halves(x, half, first):
    if 2 * half == LANES:
        return pltpu.roll(x, half, 1)
    return jnp.where(first, pltpu.roll(x, LANES - half, 1), pltpu.roll(x, half, 1))


def _qk_prep_kernel(x_ref, pos_ref, invf_ref, gq_ref, gk_ref, q_ref, k_ref):
    cosf, sinf, first = _rope_tables(pos_ref, invf_ref, ATT_HEAD_DIM // 2)
    for which, g_ref, o_ref in ((0, gq_ref, q_ref), (1, gk_ref, k_ref)):
        for h in range(ATT_HEADS):
            lo = (which * ATT_HEADS + h) * ATT_HEAD_DIM
            s = x_ref[:, lo:lo + ATT_HEAD_DIM]
            y = s * lax.rsqrt(jnp.mean(s * s, axis=-1, keepdims=True) + EPS) * g_ref[...]
            y = y * cosf + _swap_halves(y, ATT_HEAD_DIM // 2, first) * sinf
            o_ref[0, h] = y.astype(o_ref.dtype)


def _qk_prep(proj, pos, b, t, gq, gk):
    tm = _tile(t, 256, SUBLANES)
    nt = t // tm
    hd = ATT_HEADS * ATT_HEAD_DIM
    ospec = pl.BlockSpec((1, ATT_HEADS, tm, ATT_HEAD_DIM), lambda bi, ti: (bi, 0, ti, 0))
    oshape = jax.ShapeDtypeStruct((b, ATT_HEADS, t, ATT_HEAD_DIM), BF16)
    vspec = pl.BlockSpec((1, LANES), lambda bi, ti: (0, 0))
    return pl.pallas_call(
        _qk_prep_kernel,
        grid=(b, nt),
        in_specs=[pl.BlockSpec((tm, 2 * hd), lambda bi, ti: (bi * nt + ti, 1)),
                  pl.BlockSpec((tm, 1), lambda bi, ti: (bi * nt + ti, 0)), vspec, vspec, vspec],
        out_specs=[ospec, ospec],
        out_shape=[oshape, oshape],
        compiler_params=_params(("parallel", "parallel")),
        name="qk_prep",
    )(proj, pos, _inv_freq_lanes(ATT_HEAD_DIM), gq.reshape(1, LANES), gk.reshape(1, LANES))


def _idx_prep_kernel(x_ref, pos_ref, invf_ref, qi_ref, ki_ref):
    half = IDX_HEAD_DIM // 2
    cosf, sinf, first = _rope_tables(pos_ref, invf_ref, half)
    lane = lax.broadcasted_iota(I32, cosf.shape, 1)
    left = lane < IDX_HEAD_DIM

    def rope(x):
        return x * cosf + _swap_halves(x, half, first) * sinf

    def split(y):
        hi = y.astype(BF16).astype(F32)
        return hi, y - hi

    for j in range(IDX_HEADS // 2):
        hi, lo = split(rope(x_ref[:, j * LANES:(j + 1) * LANES]))
        even = jnp.where(left, hi, pltpu.roll(lo, IDX_HEAD_DIM, 1)).astype(BF16)
        odd = jnp.where(left, pltpu.roll(hi, IDX_HEAD_DIM, 1), lo).astype(BF16)
        for h, v in ((2 * j, even), (2 * j + 1, odd)):
            qi_ref[0, h, :, 0:LANES] = v
            qi_ref[0, h, :, LANES:2 * LANES] = v
    kcol = IDX_HEADS * IDX_HEAD_DIM
    khi, klo = split(rope(x_ref[:, kcol:kcol + LANES]))
    ki_ref[0, :, 0:LANES] = jnp.where(left, khi, pltpu.roll(khi, IDX_HEAD_DIM, 1)).astype(BF16)
    ki_ref[0, :, LANES:2 * LANES] = jnp.where(left, klo, pltpu.roll(klo, IDX_HEAD_DIM, 1)).astype(BF16)


def _idx_prep(idx, pos, b, t):
    tm = _tile(t, 256, SUBLANES)
    nt = t // tm
    w = idx.shape[1]
    return pl.pallas_call(
        _idx_prep_kernel,
        grid=(b, nt),
        in_specs=[pl.BlockSpec((tm, w), lambda bi, ti: (bi * nt + ti, 0)),
                  pl.BlockSpec((tm, 1), lambda bi, ti: (bi * nt + ti, 0)),
                  pl.BlockSpec((1, LANES), lambda bi, ti: (0, 0))],
        out_specs=[pl.BlockSpec((1, IDX_HEADS, tm, 2 * LANES), lambda bi, ti: (bi, 0, ti, 0)),
                   pl.BlockSpec((1, tm, 2 * LANES), lambda bi, ti: (bi, ti, 0))],
        out_shape=[jax.ShapeDtypeStruct((b, IDX_HEADS, t, 2 * LANES), BF16),
                   jax.ShapeDtypeStruct((b, t, 2 * LANES), BF16)],
        compiler_params=_params(("parallel", "parallel")),
        name="idx_prep",
    )(idx, pos, _inv_freq_lanes(IDX_HEAD_DIM))


def _mla_prep_kernel(m_ref, pos_ref, invf_ref, qa_ref, kva_ref, wuq_ref, wukv_ref,
                     qnn_ref, qnr_ref, knn_ref, knr_ref, q_ref, k_ref, v_ref):
    half = MLA_ROPE // 2
    cosf, sinf, first = _rope_tables(pos_ref, invf_ref, half)
    lane = lax.broadcasted_iota(I32, cosf.shape, 1)
    left = lane < MLA_ROPE

    def rope(x):
        return x * cosf + _swap_halves(x, half, first) * sinf

    def norm(x, g_ref):
        return (x * lax.rsqrt(jnp.mean(x * x, axis=-1, keepdims=True) + EPS) * g_ref[...]).astype(BF16)

    cq = norm(m_ref[:, 0:MLA_Q_LORA], qa_ref)
    ckv = norm(m_ref[:, MLA_Q_LORA:MLA_Q_LORA + MLA_KV_LORA], kva_ref)
    kr = m_ref[:, MLA_Q_LORA + MLA_KV_LORA:MLA_Q_LORA + MLA_KV_LORA + LANES]
    qf = jnp.dot(cq, wuq_ref[...], preferred_element_type=F32)
    kvf = jnp.dot(ckv, wukv_ref[...], preferred_element_type=F32)
    nope_w = MLA_HEADS * MLA_NOPE
    v_ref[...] = kvf[:, nope_w:nope_w + MLA_HEADS * MLA_V].astype(v_ref.dtype)

    for j in range(MLA_HEADS // 2):
        rs = qf[:, nope_w + j * LANES:nope_w + (j + 1) * LANES]
        sq = rs * rs
        ss_pair = (jnp.sum(jnp.where(left, sq, 0.0), axis=-1, keepdims=True),
                   jnp.sum(jnp.where(left, 0.0, sq), axis=-1, keepdims=True))
        for par in range(2):
            h = 2 * j + par
            nope = qf[:, h * MLA_NOPE:(h + 1) * MLA_NOPE]
            ms = (jnp.sum(nope * nope, axis=-1, keepdims=True) + ss_pair[par]) * (1.0 / MLA_QK)
            rsq = lax.rsqrt(ms + EPS)
            q_ref[0, h, :, 0:LANES] = (nope * rsq * qnn_ref[...]).astype(q_ref.dtype)
            rr = rope(rs * rsq * qnr_ref[...])
            if par == 1:
                rr = pltpu.roll(rr, MLA_ROPE, 1)
            q_ref[0, h, :, LANES:2 * LANES] = jnp.where(left, rr, 0.0).astype(q_ref.dtype)

    ss_kr = jnp.sum(jnp.where(left, kr * kr, 0.0), axis=-1, keepdims=True)
    base = jnp.where(left, rope(kr * knr_ref[...]), 0.0)
    for h in range(MLA_HEADS):
        nope = kvf[:, h * MLA_NOPE:(h + 1) * MLA_NOPE]
        ms = (jnp.sum(nope * nope, axis=-1, keepdims=True) + ss_kr) * (1.0 / MLA_QK)
        rsq = lax.rsqrt(ms + EPS)
        k_ref[0, h, :, 0:LANES] = (nope * rsq * knn_ref[...]).astype(k_ref.dtype)
        k_ref[0, h, :, LANES:2 * LANES] = (base * rsq).astype(k_ref.dtype)


def _mla_prep(mla, pos, b, t, qa, kva, w_uq, w_ukv, qn, kn):
    tm = _tile(t, 256, SUBLANES)
    nt = t // tm
    wq = w_uq.reshape(MLA_Q_LORA, MLA_HEADS, MLA_QK)
    wq = jnp.concatenate([wq[:, :, :MLA_NOPE].reshape(MLA_Q_LORA, -1),
                          wq[:, :, MLA_NOPE:].reshape(MLA_Q_LORA, -1)], axis=1).astype(BF16)
    wkv = w_ukv.reshape(MLA_KV_LORA, MLA_HEADS, MLA_NOPE + MLA_V)
    wkv = jnp.concatenate([wkv[:, :, :MLA_NOPE].reshape(MLA_KV_LORA, -1),
                           wkv[:, :, MLA_NOPE:].reshape(MLA_KV_LORA, -1)], axis=1).astype(BF16)
    dup = lambda g: jnp.tile(g[MLA_NOPE:], 2).reshape(1, LANES)
    full = lambda a: pl.BlockSpec(a.shape, lambda bi, ti: (0,) * a.ndim)
    consts = [_inv_freq_lanes(MLA_ROPE), qa.reshape(1, -1), kva.reshape(1, -1), wq, wkv,
              qn[:MLA_NOPE].reshape(1, LANES), dup(qn), kn[:MLA_NOPE].reshape(1, LANES), dup(kn)]
    hspec = pl.BlockSpec((1, MLA_HEADS, tm, 2 * LANES), lambda bi, ti: (bi, 0, ti, 0))
    hshape = jax.ShapeDtypeStruct((b, MLA_HEADS, t, 2 * LANES), BF16)
    vw = MLA_HEADS * MLA_V
    return pl.pallas_call(
        _mla_prep_kernel,
        grid=(b, nt),
        in_specs=[pl.BlockSpec((tm, mla.shape[1]), lambda bi, ti: (bi * nt + ti, 0)),
                  pl.BlockSpec((tm, 1), lambda bi, ti: (bi * nt + ti, 0))] + [full(a) for a in consts],
        out_specs=[hspec, hspec, pl.BlockSpec((tm, vw), lambda bi, ti: (bi * nt + ti, 0))],
        out_shape=[hshape, hshape, jax.ShapeDtypeStruct((b * t, vw), BF16)],
        compiler_params=_params(("parallel", "parallel")),
        name="mla_prep",
    )(mla, pos, *consts)


def _attn_kernel(*refs, sparse, n_sel, tq, scale, idx_scale, t_total):
    if sparse:
        (q_ref, k_ref, vt_ref, qi_ref, ki_ref, wt_ref, o_ref, m_scr, l_scr, acc_scr,
         score_scr, bias_scr, thr_scr, tie_scr) = refs
    else:
        q_ref, k_ref, vt_ref, o_ref, m_scr, l_scr, acc_scr = refs
    n_heads, dv = vt_ref.shape[1], vt_ref.shape[3]
    qb = pl.program_id(1)
    nkc = qb + 1
    kio = lax.broadcasted_iota(I32, (tq, tq), 0)
    qio = lax.broadcasted_iota(I32, (tq, tq), 1)
    neg_inf = -jnp.inf

    def chunk_off(c):
        return pl.multiple_of(c * tq, tq)

    def causal(c):
        return (c * tq + kio) <= (qb * tq + qio)

    if sparse:
        n_idx_heads = qi_ref.shape[1]

        def score_chunk(c, carry):
            koff = chunk_off(c)
            ki = ki_ref[0, pl.ds(koff, tq), :]
            acc = jnp.zeros((tq, tq), F32)
            for h in range(n_idx_heads):
                s = lax.dot_general(ki, qi_ref[0, h], NT_DIMS, preferred_element_type=F32)
                acc = acc + jnp.maximum(s, 0.0) * wt_ref[0, h:h + 1, :]
            score_scr[pl.ds(koff, tq), :] = jnp.where(causal(c), acc * idx_scale, neg_inf)
            return carry

        lax.fori_loop(0, nkc, score_chunk, 0)

        def key_to_float(key):
            val = pltpu.bitcast(jnp.where(key < 0, key ^ 0x7FFFFFFF, key), F32)
            return jnp.where(key < NEG_INF_KEY, neg_inf, val)

        def count(pred):
            def body(c, acc):
                x = score_scr[pl.ds(chunk_off(c), tq), :]
                m = jnp.where(pred(x, c), 1.0, 0.0)
                return acc + jnp.sum(m.reshape(tq // SUBLANES, SUBLANES, tq), axis=0)
            acc = lax.fori_loop(0, nkc, body, jnp.zeros((SUBLANES, tq), F32))
            return jnp.sum(acc, axis=0, keepdims=True)

        def count_ge(key):
            cand = key_to_float(key)
            return count(lambda x, c: x >= cand)

        k_sel = jnp.float32(n_sel)
        zero = jnp.zeros((1, tq), I32)
        thr_key = jnp.where(count_ge(zero) >= k_sel, zero, jnp.full((1, tq), INT_MIN, I32))

        def bit_step(i, key):
            cand = key | lax.shift_left(jnp.int32(1), jnp.int32(30) - i)
            return jnp.where(count_ge(cand) >= k_sel, cand, key)

        thr_key = lax.fori_loop(0, 31, bit_step, thr_key)
        thr = key_to_float(thr_key)
        n_ge = count(lambda x, c: x >= thr)
        thr_scr[0:1, :] = thr
        tie_scr[0:1, :] = jnp.full((1, tq), t_total, I32)

        @pl.when(jnp.max(n_ge) > k_sel)
        def _():
            need = k_sel - count(lambda x, c: x > thr)

            def below(x, c, m):
                return jnp.logical_and(x == thr, (c * tq + kio) < m)

            def idx_step(i, lo):
                cand = lo | lax.shift_left(jnp.int32(1), jnp.int32(t_total.bit_length() - 1) - i)
                return jnp.where(count(lambda x, c: below(x, c, cand)) < need, cand, lo)

            lo = lax.fori_loop(0, t_total.bit_length(), idx_step, zero)
            tie_scr[0:1, :] = lo + 1

        thr = thr_scr[0:1, :]
        tie_end = tie_scr[0:1, :]

        def bias_chunk(c, carry):
            koff = chunk_off(c)
            x = score_scr[pl.ds(koff, tq), :]
            tie_ok = jnp.logical_and(x == thr, (c * tq + kio) < tie_end)
            keep = jnp.logical_and(jnp.logical_or(x > thr, tie_ok), causal(c))
            bias_scr[pl.ds(koff, tq), :] = jnp.where(keep, 0.0, neg_inf)
            return carry

        lax.fori_loop(0, nkc, bias_chunk, 0)

    m_scr[...] = jnp.full(m_scr.shape, neg_inf, F32)
    l_scr[...] = jnp.zeros(l_scr.shape, F32)
    acc_scr[...] = jnp.zeros(acc_scr.shape, F32)

    def chunk_step(c, masked):
        koff = chunk_off(c)
        if sparse:
            bias = bias_scr[pl.ds(koff, tq), :]
        elif masked:
            allowed = causal(c)
        def qk(h):
            return lax.dot_general(k_ref[0, h, pl.ds(koff, tq), :], q_ref[0, h], NT_DIMS,
                                   preferred_element_type=F32)

        queued = [qk(h) for h in range(QK_LOOKAHEAD)]
        for h in range(n_heads):
            s = queued.pop(0) * (scale * LOG2_E)
            if h + QK_LOOKAHEAD < n_heads:
                queued.append(qk(h + QK_LOOKAHEAD))
            if sparse:
                s = s + bias
            elif masked:
                s = jnp.where(allowed, s, neg_inf)
            m_old = m_scr[h]
            m_new = jnp.maximum(m_old, jnp.max(s, axis=0, keepdims=True))
            m_ref = jnp.where(m_new == neg_inf, 0.0, m_new)
            alpha = jnp.exp2(m_old - m_ref)
            p = jnp.exp2(s - m_ref)
            l_scr[h] = alpha * l_scr[h] + jnp.sum(p, axis=0, keepdims=True)
            acc_scr[h] = alpha * acc_scr[h] + jnp.dot(vt_ref[0, h, c], p.astype(BF16),
                                                      preferred_element_type=F32)
            m_scr[h] = m_new

    def off_diagonal(c, carry):
        chunk_step(c, False)
        return carry

    if sparse:
        lax.fori_loop(0, nkc, off_diagonal, 0)
    else:
        lax.fori_loop(0, qb, off_diagonal, 0)
        chunk_step(qb, True)
    for h in range(n_heads):
        o_ref[0, h] = (acc_scr[h] / l_scr[h]).astype(o_ref.dtype)


def _attention(q, k, v, b, t, scale, sparse_inputs=None, n_sel=0):
    h, dq = q.shape[1], q.shape[3]
    dv = v.shape[1] // h
    tq = _tile(t, 256, LANES)
    nq = t // tq
    vt = v.reshape(b, nq, tq, h, dv).transpose(0, 3, 1, 4, 2)
    in_specs = [pl.BlockSpec((1, h, tq, dq), lambda bi, qi: (bi, 0, qi, 0)),
                pl.BlockSpec((1, h, t, dq), lambda bi, qi: (bi, 0, 0, 0)),
                pl.BlockSpec((1, h, nq, dv, tq), lambda bi, qi: (bi, 0, 0, 0, 0))]
    args = [q, k, vt]
    scratch = [pltpu.VMEM((h, 1, tq), F32), pltpu.VMEM((h, 1, tq), F32), pltpu.VMEM((h, dv, tq), F32)]
    sparse = sparse_inputs is not None
    if sparse:
        qi_, ki_, wt_ = sparse_inputs
        hi, dk = qi_.shape[1], qi_.shape[3]
        in_specs += [pl.BlockSpec((1, hi, tq, dk), lambda bi, qi: (bi, 0, qi, 0)),
                     pl.BlockSpec((1, t, dk), lambda bi, qi: (bi, 0, 0)),
                     pl.BlockSpec((1, hi, tq), lambda bi, qi: (bi, 0, qi))]
        args += [qi_, ki_, wt_]
        scratch += [pltpu.VMEM((t, tq), F32), pltpu.VMEM((t, tq), F32),
                    pltpu.VMEM((SUBLANES, tq), F32), pltpu.VMEM((SUBLANES, tq), I32)]
    out = pl.pallas_call(
        functools.partial(_attn_kernel, sparse=sparse, n_sel=n_sel, tq=tq, scale=scale,
                          idx_scale=IDX_HEADS ** -0.5 * IDX_HEAD_DIM ** -0.5, t_total=t),
        grid=(b, nq),
        in_specs=in_specs,
        out_specs=pl.BlockSpec((1, h, dv, tq), lambda bi, qi: (bi, 0, 0, qi)),
        out_shape=jax.ShapeDtypeStruct((b, h, dv, t), BF16),
        scratch_shapes=scratch,
        compiler_params=_params(("parallel", "arbitrary")),
        name="sparse_attention" if sparse else "dense_attention",
    )(*args)
    return out.transpose(0, 3, 1, 2).reshape(b * t, h * dv)


def _merge_kernel(hn_ref, a_ref, b_ref, c_ref, wg_ref, wb_ref, o_ref):
    hn = hn_ref[...]
    acc = None
    for n, br_ref in enumerate((a_ref, b_ref, c_ref)):
        gate = jax.nn.sigmoid(jnp.dot(hn, wg_ref[n], preferred_element_type=F32))
        term = gate * jnp.dot(br_ref[...], wb_ref[n], preferred_element_type=F32)
        acc = term if acc is None else acc + term
    o_ref[...] = acc.astype(o_ref.dtype)


def _merge(hn, branches, w_gate, w_branch):
    m, d = hn.shape
    bw = w_branch.shape[1]
    tm = _tile(m, 512, SUBLANES)
    tn = _tile(d, 512)
    bspec = pl.BlockSpec((tm, bw), lambda i, j: (i, 0))
    return pl.pallas_call(
        _merge_kernel,
        grid=(m // tm, d // tn),
        in_specs=[pl.BlockSpec((tm, d), lambda i, j: (i, 0)), bspec, bspec, bspec,
                  pl.BlockSpec((N_BRANCH, d, tn), lambda i, j: (0, 0, j)),
                  pl.BlockSpec((N_BRANCH, bw, tn), lambda i, j: (0, 0, j))],
        out_specs=pl.BlockSpec((tm, tn), lambda i, j: (i, j)),
        out_shape=jax.ShapeDtypeStruct((m, d), BF16),
        compiler_params=_params(("parallel", "arbitrary")),
        name="gated_merge",
    )(hn, *branches, w_gate, w_branch)


def _ple_kernel(h_ref, hn_ref, p_ref, wg_ref, wp_ref, o_ref):
    gate = jax.nn.sigmoid(jnp.dot(hn_ref[...], wg_ref[...], preferred_element_type=F32))
    emb = jnp.dot(p_ref[...].astype(BF16), wp_ref[...], preferred_element_type=F32)
    o_ref[...] = h_ref[...] + emb * gate


def _ple(h, hn, p, w_gate, w_proj):
    m, d = h.shape
    pd = p.shape[1]
    tm = _tile(m, 1024, SUBLANES)
    tn = _tile(d, 512)
    return pl.pallas_call(
        _ple_kernel,
        grid=(m // tm, d // tn),
        in_specs=[pl.BlockSpec((tm, tn), lambda i, j: (i, j)), pl.BlockSpec((tm, d), lambda i, j: (i, 0)),
                  pl.BlockSpec((tm, pd), lambda i, j: (i, 0)), pl.BlockSpec((d, tn), lambda i, j: (0, j)),
                  pl.BlockSpec((pd, tn), lambda i, j: (0, j))],
        out_specs=pl.BlockSpec((tm, tn), lambda i, j: (i, j)),
        out_shape=jax.ShapeDtypeStruct((m, d), F32),
        compiler_params=_params(("parallel", "arbitrary")),
        name="ple",
    )(h, hn, p, w_gate, w_proj)


def _ffn_kernel(be_ref, nu_ref, x_ref, wg_ref, wu_ref, wd_ref, *rest, has_res):
    if has_res:
        res_ref, o_ref = rest
    else:
        (o_ref,) = rest
    i, j = pl.program_id(0), pl.program_id(1)
    used = i < nu_ref[0]

    @pl.when(used)
    def _():
        x = x_ref[...].astype(BF16)
        g = jnp.dot(x, wg_ref[0].astype(BF16), preferred_element_type=F32)
        u = jnp.dot(x, wu_ref[0].astype(BF16), preferred_element_type=F32)
        act = (g * jax.nn.sigmoid(g) * u).astype(BF16)
        d = jnp.dot(act, wd_ref[0].astype(BF16), preferred_element_type=F32)

        @pl.when(j == 0)
        def _():
            o_ref[...] = (res_ref[...] + d) if has_res else d

        @pl.when(j > 0)
        def _():
            o_ref[...] += d

    @pl.when(jnp.logical_and(jnp.logical_not(used), j == 0))
    def _():
        o_ref[...] = jnp.zeros_like(o_ref)


def _ffn(x, w_gate, w_up, w_down, blk_expert, n_used, tm, res=None):
    r, d = x.shape
    f = w_gate.shape[2]
    tf = _tile(f, 512)
    nf = f // tf
    assert nf >= 2

    def jj(i, j, nu):
        return jnp.where(i < nu[0], j, nf - 1)

    rows_mode = pl.Buffered(1) if w_gate.dtype == F32 else None
    in_specs = [pl.BlockSpec((tm, d), lambda i, j, be, nu: (i, 0), pipeline_mode=rows_mode),
                pl.BlockSpec((1, d, tf), lambda i, j, be, nu: (be[i], 0, jj(i, j, nu))),
                pl.BlockSpec((1, d, tf), lambda i, j, be, nu: (be[i], 0, jj(i, j, nu))),
                pl.BlockSpec((1, tf, d), lambda i, j, be, nu: (be[i], jj(i, j, nu), 0))]
    args = [x, w_gate, w_up, w_down]
    if res is not None:
        in_specs.append(pl.BlockSpec((tm, d), lambda i, j, be, nu: (i, 0)))
        args.append(res)
    return pl.pallas_call(
        functools.partial(_ffn_kernel, has_res=res is not None),
        grid_spec=pltpu.PrefetchScalarGridSpec(
            num_scalar_prefetch=2,
            grid=(r // tm, nf),
            in_specs=in_specs,
            out_specs=pl.BlockSpec((tm, d), lambda i, j, be, nu: (i, 0), pipeline_mode=rows_mode)),
        out_shape=jax.ShapeDtypeStruct((r, d), F32),
        compiler_params=_params(("arbitrary", "arbitrary")),
        name="swiglu",
    )(blk_expert, n_used, *args)


def _router_kernel(h_ref, g_ref, rhi_ref, rlo_ref, hn_ref, info_ref, cnt_ref, run_scr, *, n_experts):
    i = pl.program_id(0)
    tm = h_ref.shape[0]

    @pl.when(i == 0)
    def _():
        run_scr[...] = jnp.zeros_like(run_scr)

    x = h_ref[...]
    xn = x * lax.rsqrt(jnp.mean(x * x, axis=-1, keepdims=True) + EPS) * g_ref[...]
    hi = xn.astype(BF16)
    hn_ref[...] = xn
    lo = (xn - hi.astype(F32)).astype(BF16)
    logits = (jnp.dot(hi, rhi_ref[...], preferred_element_type=F32)
              + jnp.dot(hi, rlo_ref[...], preferred_element_type=F32)
              + jnp.dot(lo, rhi_ref[...], preferred_element_type=F32))
    lane = lax.broadcasted_iota(I32, logits.shape, 1)
    lane_f = lane.astype(F32)
    logits = jnp.where(lane < n_experts, logits, -jnp.inf)

    def top(vals):
        v = jnp.max(vals, axis=-1, keepdims=True)
        idx = jnp.min(jnp.where(vals == v, lane_f, float(LANES)), axis=-1, keepdims=True)
        return v, idx

    v1, i1 = top(logits)
    v2, i2 = top(jnp.where(lane_f == i1, -jnp.inf, logits))
    e2 = jnp.exp(v2 - v1)
    g1 = 1.0 / (1.0 + e2)
    g2 = e2 / (1.0 + e2)

    oh1 = lane_f == i1
    oh2 = lane_f == i2
    both = jnp.where(jnp.logical_or(oh1, oh2), 1.0, 0.0)
    r_io = lax.broadcasted_iota(I32, (tm, tm), 0)
    c_io = lax.broadcasted_iota(I32, (tm, tm), 1)
    strict_lower = jnp.where(c_io < r_io, 1.0, 0.0).astype(BF16)
    before = jnp.dot(strict_lower, both.astype(BF16), preferred_element_type=F32) + run_scr[0:1, :]
    rank1 = jnp.sum(jnp.where(oh1, before, 0.0), axis=-1, keepdims=True)
    rank2 = jnp.sum(jnp.where(oh2, before, 0.0), axis=-1, keepdims=True)
    run_scr[0:1, :] = run_scr[0:1, :] + jnp.sum(both, axis=0, keepdims=True)

    info = jnp.zeros(logits.shape, F32)
    for col, val in enumerate((i1, i2, g1, g2, rank1, rank2)):
        info = jnp.where(lane == col, val, info)
    info_ref[...] = info
    cnt_ref[...] = jnp.broadcast_to(run_scr[0:1, :], cnt_ref.shape)


def _router(h, g, router):
    m, d = h.shape
    e = router.shape[1]
    tm = _tile(m, 256, SUBLANES)
    rp = jnp.zeros((d, LANES), F32).at[:, :e].set(router)
    rhi = rp.astype(BF16)
    rlo = (rp - rhi.astype(F32)).astype(BF16)
    return pl.pallas_call(
        functools.partial(_router_kernel, n_experts=e),
        grid=(m // tm,),
        in_specs=[pl.BlockSpec((tm, d), lambda i: (i, 0)), pl.BlockSpec((1, d), lambda i: (0, 0)),
                  pl.BlockSpec((d, LANES), lambda i: (0, 0)), pl.BlockSpec((d, LANES), lambda i: (0, 0))],
        out_specs=[pl.BlockSpec((tm, d), lambda i: (i, 0)), pl.BlockSpec((tm, LANES), lambda i: (i, 0)),
                   pl.BlockSpec((SUBLANES, LANES), lambda i: (0, 0))],
        out_shape=[jax.ShapeDtypeStruct((m, d), F32), jax.ShapeDtypeStruct((m, LANES), F32),
                   jax.ShapeDtypeStruct((SUBLANES, LANES), F32)],
        scratch_shapes=[pltpu.VMEM((SUBLANES, LANES), F32)],
        compiler_params=_params(("arbitrary",)),
        name="router",
    )(h, g.reshape(1, d), rhi, rlo)


def _row_copy(src, dst, sem, s, d):
    return pltpu.make_async_copy(src.at[pl.ds(s, 1), :], dst.at[pl.ds(d, 1), :], sem)


def _dispatch_kernel(dest_ref, x_ref, xs_in_hbm, xs_hbm, sem, *, rows):
    del xs_in_hbm

    def issue(r, carry):
        for kk in range(TOP_K):
            _row_copy(x_ref, xs_hbm, sem, r, dest_ref[0, 0, TOP_K * r + kk]).start()
        return carry

    lax.fori_loop(0, rows, issue, 0, unroll=8)
    for kk in range(TOP_K):
        pltpu.make_async_copy(x_ref, xs_hbm.at[pl.ds(0, rows), :], sem).wait()


def _dispatch(x, dest, n_rows):
    m, d = x.shape
    rows = _tile(m, 512, SUBLANES)
    return pl.pallas_call(
        functools.partial(_dispatch_kernel, rows=rows),
        grid=(m // rows,),
        in_specs=[pl.BlockSpec((1, 1, TOP_K * rows), lambda i: (i, 0, 0), memory_space=pltpu.SMEM),
                  pl.BlockSpec((rows, d), lambda i: (i, 0)), pl.BlockSpec(memory_space=pl.ANY)],
        out_specs=pl.BlockSpec(memory_space=pl.ANY),
        out_shape=jax.ShapeDtypeStruct((n_rows, d), x.dtype),
        scratch_shapes=[pltpu.SemaphoreType.DMA(())],
        input_output_aliases={2: 0},
        compiler_params=_params(("arbitrary",)),
        name="moe_dispatch",
    )(dest.reshape(m // rows, 1, TOP_K * rows), x, jnp.zeros((n_rows, d), x.dtype))


def _combine_kernel(dest_ref, h_ref, info_ref, g_ref, ys_hbm, o_ref, on_ref, buf, sem, *, rows):
    def issue(r, carry):
        for kk in range(TOP_K):
            pltpu.make_async_copy(ys_hbm.at[pl.ds(dest_ref[0, 0, TOP_K * r + kk], 1), :],
                                  buf.at[kk, pl.ds(r, 1), :], sem).start()
        return carry

    lax.fori_loop(0, rows, issue, 0, unroll=8)
    for kk in range(TOP_K):
        pltpu.make_async_copy(ys_hbm.at[pl.ds(0, rows), :], buf.at[kk], sem).wait()
    info = info_ref[...]
    out = h_ref[...] + (info[:, 2:3] * buf[0] + info[:, 3:4] * buf[1])
    o_ref[...] = out
    on_ref[...] = (out * lax.rsqrt(jnp.mean(out * out, axis=-1, keepdims=True) + EPS)
                   * g_ref[...]).astype(on_ref.dtype)


def _combine(h, info, dest, ys, g):
    m, d = h.shape
    rows = _tile(m, 256, SUBLANES)
    return pl.pallas_call(
        functools.partial(_combine_kernel, rows=rows),
        grid=(m // rows,),
        in_specs=[pl.BlockSpec((1, 1, TOP_K * rows), lambda i: (i, 0, 0), memory_space=pltpu.SMEM),
                  pl.BlockSpec((rows, d), lambda i: (i, 0)), pl.BlockSpec((rows, LANES), lambda i: (i, 0)),
                  pl.BlockSpec((1, d), lambda i: (0, 0)), pl.BlockSpec(memory_space=pl.ANY)],
        out_specs=[pl.BlockSpec((rows, d), lambda i: (i, 0)), pl.BlockSpec((rows, d), lambda i: (i, 0))],
        out_shape=[jax.ShapeDtypeStruct((m, d), F32), jax.ShapeDtypeStruct((m, d), BF16)],
        scratch_shapes=[pltpu.VMEM((TOP_K, rows, d), F32), pltpu.SemaphoreType.DMA(())],
        compiler_params=_params(("arbitrary",)),
        name="moe_combine",
    )(dest.reshape(m // rows, 1, TOP_K * rows), h, info, g.reshape(1, d), ys)


def _moe(h, g_ffn, router, w_gate, w_up, w_down, g_next):
    m, d = h.shape
    e = router.shape[1]
    hn, info, cnt = _router(h, g_ffn, router)
    counts = cnt[0, :e].astype(I32)
    padded = (counts + MOE_ROWS - 1) // MOE_ROWS * MOE_ROWS
    pad_ends = jnp.cumsum(padded)
    pad_starts = pad_ends - padded
    n_blocks = -(-(m * TOP_K) // MOE_ROWS) + e
    ids = info[:, 0:TOP_K].astype(I32)
    dest = jnp.sum(jnp.where(ids[:, :, None] == jnp.arange(e)[None, None, :], pad_starts[None, None, :], 0),
                   axis=-1) + info[:, 4:4 + TOP_K].astype(I32)
    blk_start = jnp.arange(n_blocks, dtype=I32) * MOE_ROWS
    blk_expert = jnp.minimum(jnp.sum(blk_start[:, None] >= pad_ends[None, :], axis=1), e - 1).astype(I32)
    n_used = (pad_ends[e - 1] // MOE_ROWS).astype(I32).reshape(1)
    xs = _dispatch(hn, dest, n_blocks * MOE_ROWS)
    ys = _ffn(xs, w_gate, w_up, w_down, blk_expert, n_used, MOE_ROWS)
    return _combine(h, info, dest, ys, g_next)


def _split_w_in(w):
    d = w.shape[0]
    c = LRU_WIDTH
    a = ATT_HEADS * ATT_HEAD_DIM
    o_att = 2 * c
    o_idx = o_att + 3 * a
    n_idx = IDX_HEADS * IDX_HEAD_DIM + IDX_HEAD_DIM + IDX_HEADS
    o_mla = o_idx + n_idx
    n_mla = MLA_Q_LORA + MLA_KV_LORA + MLA_ROPE
    o_gate = o_mla + n_mla

    def padded(x):
        pad = -x.shape[1] % LANES
        return jnp.pad(x, ((0, 0), (0, pad)))

    main = w[:, :o_att + 2 * a]
    w_v = w[:, o_att + 2 * a:o_idx]
    w_idx = padded(w[:, o_idx:o_mla])
    w_mla = padded(w[:, o_mla:o_gate])
    w_gates = w[:, o_gate:].reshape(d, N_BRANCH, d).transpose(1, 0, 2)
    return tuple(x.astype(BF16) for x in (main, w_v, w_idx, w_mla, w_gates))


def kernel(x, p, positions, ln_mix, w_in, conv_w, conv_b, lru_wa, lru_ba, lru_wx, lru_bx, lru_lambda,
           att_q_norm, att_k_norm, mla_qa_norm, mla_kva_norm, mla_w_uq, mla_w_ukv, mla_q_norm,
           mla_k_norm, w_branch, w_out, ln_ffn, dense_w_gate, dense_w_up, dense_w_down, moe_router,
           moe_w_gate, moe_w_up, moe_w_down, ple_norm, ple_w_gate, ple_w_proj):
    b, t, d = x.shape
    m = b * t
    depth = w_in.shape[0]
    n_sel = min(TOPK_MAX, t // 4)
    pos = positions.reshape(m, 1).astype(I32)
    h = x.reshape(m, d)
    idx_w_off = IDX_HEADS * IDX_HEAD_DIM + IDX_HEAD_DIM
    for i in range(depth):
        w_main, w_v, w_idx, w_mla, w_gates = _split_w_in(w_in[i])
        hn = _rmsnorm(h, ln_mix[i])
        proj = _matmul(hn, w_main, F32, name="in_proj_main")
        v = _matmul(hn, w_v, BF16, name="in_proj_v")
        idx = _matmul(hn, w_idx, F32, tn_cap=w_idx.shape[1], name="in_proj_idx")
        mla = _matmul(hn, w_mla, F32, tn_cap=w_mla.shape[1], name="in_proj_mla")

        out_a = _rglru(proj, b, t, conv_w[i], conv_b[i], lru_wa[i], lru_ba[i], lru_wx[i], lru_bx[i],
                       lru_lambda[i])

        q, k = _qk_prep(proj, pos, b, t, att_q_norm[i], att_k_norm[i])
        qi, ki = _idx_prep(idx, pos, b, t)
        wt = idx[:, idx_w_off:idx_w_off + IDX_HEADS].reshape(b, t, IDX_HEADS).transpose(0, 2, 1)
        out_b = _attention(q, k, v, b, t, ATT_HEAD_DIM ** -0.5, sparse_inputs=(qi, ki, wt), n_sel=n_sel)

        mq, mk, mv = _mla_prep(mla, pos, b, t, mla_qa_norm[i], mla_kva_norm[i], mla_w_uq[i],
                               mla_w_ukv[i], mla_q_norm[i], mla_k_norm[i])
        out_c = _attention(mq, mk, mv, b, t, MLA_QK ** -0.5)

        merged = _merge(hn, (out_a, out_b, out_c), w_gates, w_branch[i].astype(BF16))
        h = _matmul(merged, w_out[i].astype(BF16), F32, res=h, name="out_proj")

        if i % 2 == 0:
            j = i // 2
            hn2 = _rmsnorm(h, ln_ffn[i])
            tm = _tile(m, 512, SUBLANES)
            h = _ffn(hn2, dense_w_gate[j:j + 1].astype(BF16), dense_w_up[j:j + 1].astype(BF16),
                     dense_w_down[j:j + 1].astype(BF16), jnp.zeros((m // tm,), I32),
                     jnp.full((1,), m // tm, I32), tm, res=h)
            hn3 = _rmsnorm(h, ple_norm[i])
        else:
            j = i // 2
            h, hn3 = _moe(h, ln_ffn[i], moe_router[j], moe_w_gate[j], moe_w_up[j], moe_w_down[j],
                          ple_norm[i])
        h = _ple(h, hn3, p[i].reshape(m, -1), ple_w_gate[i].astype(BF16), ple_w_proj[i].astype(BF16))
    return h.reshape(b, t, d)
```

```python
import functools

import jax
import jax.numpy as jnp
from jax import lax
from jax.experimental import pallas as pl
from jax.experimental.pallas import tpu as pltpu

F32, BF16, I32 = jnp.float32, jnp.bfloat16, jnp.int32

EPS = 1e-6
ROPE_THETA = 10000.0
LRU_C = 8.0
LRU_WIDTH = 1024
LRU_BLOCKS = 8
ATT_HEADS = 8
ATT_HEAD_DIM = 128
IDX_HEADS = 16
IDX_HEAD_DIM = 64
TOPK_MAX = 256
MLA_HEADS = 8
MLA_Q_LORA = 768
MLA_KV_LORA = 512
MLA_NOPE = 128
MLA_ROPE = 64
MLA_QK = MLA_NOPE + MLA_ROPE
MLA_V = 128
N_BRANCH = 3
N_EXPERTS = 8
TOP_K = 2

LANES = 128
SUBLANES = 8
VMEM_LIMIT_BYTES = 56 << 20
MOE_ROWS = 768
INT_MIN = -(2 ** 31)
NEG_INF_KEY = 0x807FFFFF - 2 ** 32

NT_DIMS = (((1,), (1,)), ((), ()))
LOG2_E = 1.4426950408889634
QK_LOOKAHEAD = 4


def _params(sem):
    return pltpu.CompilerParams(dimension_semantics=sem, vmem_limit_bytes=VMEM_LIMIT_BYTES)


def _tile(n, cap, unit=LANES):
    if n <= cap:
        return n
    best = None
    for t in range(unit, cap + 1, unit):
        if n % t == 0:
            best = t
    assert best is not None, (n, cap)
    return best


def _rmsnorm_kernel(x_ref, g_ref, o_ref):
    x = x_ref[...]
    ms = jnp.mean(x * x, axis=-1, keepdims=True)
    o_ref[...] = (x * lax.rsqrt(ms + EPS) * g_ref[...]).astype(o_ref.dtype)


def _rmsnorm(x, g):
    m, d = x.shape
    tm = _tile(m, 512, SUBLANES)
    return pl.pallas_call(
        _rmsnorm_kernel,
        grid=(m // tm,),
        in_specs=[pl.BlockSpec((tm, d), lambda i: (i, 0)), pl.BlockSpec((1, d), lambda i: (0, 0))],
        out_specs=pl.BlockSpec((tm, d), lambda i: (i, 0)),
        out_shape=jax.ShapeDtypeStruct((m, d), BF16),
        compiler_params=_params(("parallel",)),
        name="rmsnorm",
    )(x, g.reshape(1, d))


def _mm_kernel(x_ref, w_ref, o_ref):
    w = w_ref[0] if len(w_ref.shape) == 3 else w_ref[...]
    o_ref[...] = jnp.dot(x_ref[...], w.astype(BF16), preferred_element_type=F32).astype(o_ref.dtype)


def _mm_res_kernel(x_ref, w_ref, r_ref, o_ref):
    o_ref[...] = r_ref[...] + jnp.dot(x_ref[...], w_ref[...], preferred_element_type=F32)


def _matmul(x, w, out_dtype, res=None, tm_cap=1024, tn_cap=1024, cols=None, layer=None, name="matmul"):
    m, k = x.shape
    c0, n = cols if cols is not None else (0, w.shape[-1])
    tm = _tile(m, tm_cap, SUBLANES)
    tn = _tile(n, tn_cap)
    assert c0 % tn == 0, (c0, tn)
    jb = c0 // tn
    if layer is None:
        w_spec = pl.BlockSpec((k, tn), lambda i, j: (0, jb + j))
    else:
        w_spec = pl.BlockSpec((1, k, tn), lambda i, j: (layer, 0, jb + j))
    in_specs = [pl.BlockSpec((tm, k), lambda i, j: (i, 0)), w_spec]
    args = [x, w]
    body = _mm_kernel
    if res is not None:
        in_specs.append(pl.BlockSpec((tm, tn), lambda i, j: (i, j)))
        args.append(res)
        body = _mm_res_kernel
    return pl.pallas_call(
        body,
        grid=(m // tm, n // tn),
        in_specs=in_specs,
        out_specs=pl.BlockSpec((tm, tn), lambda i, j: (i, j)),
        out_shape=jax.ShapeDtypeStruct((m, n), out_dtype),
        compiler_params=_params(("parallel", "arbitrary")),
        name=name,
    )(*args)


EXPM1_SERIES_BOUND = 0.25
EXPM1_SERIES_TERMS = 10


def _expm1(y):
    poly = jnp.full_like(y, 1.0 / 3628800.0)
    fact = 3628800.0
    for n in range(EXPM1_SERIES_TERMS, 1, -1):
        fact /= n
        poly = poly * y + 1.0 / fact
    return jnp.where(jnp.abs(y) < EXPM1_SERIES_BOUND, poly * y, jnp.exp(y) - 1.0)


def _rglru_kernel(x_ref, g_ref, cw_ref, cb_ref, wa_ref, ba_ref, wx_ref, bx_ref, lam_ref, o_ref,
                  xs_scr, a_scr, b_scr, h_scr, *, tt):
    c = x_ref.shape[1]
    t = pl.program_id(1)

    @pl.when(t == 0)
    def _():
        xs_scr[0:SUBLANES, :] = jnp.zeros((SUBLANES, c), F32)
        h_scr[...] = jnp.zeros_like(h_scr)

    x = x_ref[...]
    xs_scr[SUBLANES:SUBLANES + tt, :] = x
    cw = cw_ref[...]
    xc = (xs_scr[SUBLANES - 3:SUBLANES - 3 + tt, :] * cw[0:1, :]
          + xs_scr[SUBLANES - 2:SUBLANES - 2 + tt, :] * cw[1:2, :]
          + xs_scr[SUBLANES - 1:SUBLANES - 1 + tt, :] * cw[2:3, :]
          + x * cw[3:4, :]) + cb_ref[...]
    xs_scr[0:SUBLANES, :] = x[tt - SUBLANES:tt, :]

    xcb = xc.astype(BF16)
    bw = c // LRU_BLOCKS
    ra = jnp.concatenate(
        [jnp.dot(xcb[:, n * bw:(n + 1) * bw], wa_ref[n], preferred_element_type=F32)
         for n in range(LRU_BLOCKS)], axis=1) + ba_ref[...]
    rx = jnp.concatenate(
        [jnp.dot(xcb[:, n * bw:(n + 1) * bw], wx_ref[n], preferred_element_type=F32)
         for n in range(LRU_BLOCKS)], axis=1) + bx_ref[...]
    r = jax.nn.sigmoid(ra)
    gi = jax.nn.sigmoid(rx)
    nlam = -lam_ref[...]
    softplus = jnp.maximum(nlam, 0.0) + jnp.log1p(jnp.exp(-jnp.abs(nlam)))
    log_a = (-LRU_C) * r * softplus
    a_scr[...] = jnp.exp(log_a)
    b_scr[...] = jnp.sqrt(-_expm1(2.0 * log_a)) * (gi * xc)

    row = lax.broadcasted_iota(I32, (SUBLANES, c), 0)

    def group(gidx, h):
        off = pl.multiple_of(gidx * SUBLANES, SUBLANES)
        a8 = a_scr[pl.ds(off, SUBLANES), :]
        b8 = b_scr[pl.ds(off, SUBLANES), :]
        for s in (1, 2, 4):
            keep = row >= s
            a_sh = jnp.where(keep, pltpu.roll(a8, s, 0), 1.0)
            b_sh = jnp.where(keep, pltpu.roll(b8, s, 0), 0.0)
            b8 = a8 * b_sh + b8
            a8 = a8 * a_sh
        h8 = a8 * h + b8
        b_scr[pl.ds(off, SUBLANES), :] = h8
        return h8[SUBLANES - 1:SUBLANES, :]

    h_scr[...] = lax.fori_loop(0, tt // SUBLANES, group, h_scr[...])
    o_ref[...] = (b_scr[...] * jax.nn.gelu(g_ref[...], approximate=True)).astype(o_ref.dtype)


def _rglru(proj, b, t, conv_w, conv_b, wa, ba, wx, bx, lam):
    c = LRU_WIDTH
    tt = _tile(t, 256, SUBLANES)
    nt = t // tt
    row = lambda v: v.reshape(1, c)
    wspec = pl.BlockSpec(wa.shape, lambda bi, ti: (0, 0, 0))
    vspec = pl.BlockSpec((1, c), lambda bi, ti: (0, 0))
    return pl.pallas_call(
        functools.partial(_rglru_kernel, tt=tt),
        grid=(b, nt),
        in_specs=[pl.BlockSpec((tt, c), lambda bi, ti: (bi * nt + ti, 0)),
                  pl.BlockSpec((tt, c), lambda bi, ti: (bi * nt + ti, 1)),
                  pl.BlockSpec(conv_w.shape, lambda bi, ti: (0, 0)), vspec,
                  wspec, vspec, wspec, vspec, vspec],
        out_specs=pl.BlockSpec((tt, c), lambda bi, ti: (bi * nt + ti, 0)),
        out_shape=jax.ShapeDtypeStruct((b * t, c), BF16),
        scratch_shapes=[pltpu.VMEM((tt + SUBLANES, c), F32), pltpu.VMEM((tt, c), F32),
                        pltpu.VMEM((tt, c), F32), pltpu.VMEM((1, c), F32)],
        compiler_params=_params(("parallel", "arbitrary")),
        name="rglru",
    )(proj, proj, conv_w, row(conv_b), wa.astype(BF16), row(ba), wx.astype(BF16), row(bx), row(lam))


def _inv_freq_lanes(d):
    f = ROPE_THETA ** (-jnp.arange(0, d, 2, dtype=F32) / d)
    return jnp.tile(jnp.concatenate([f, f]), LANES // d).reshape(1, LANES)


def _rope_tables(pos_ref, invf_ref, half):
    ang = pos_ref[...].astype(F32) * invf_ref[...]
    lane = lax.broadcasted_iota(I32, ang.shape, 1)
    first = (lane & (2 * half - 1)) < half
    return jnp.cos(ang), jnp.where(first, -jnp.sin(ang), jnp.sin(ang)), first


def _swap_halves(x, half, first):
    if 2 * half == LANES:
        return pltpu.roll(x, half, 1)
    return jnp.where(first, pltpu.roll(x, LANES - half, 1), pltpu.roll(x, half, 1))


def _qk_prep_kernel(x_ref, pos_ref, invf_ref, gq_ref, gk_ref, q_ref, k_ref):
    cosf, sinf, first = _rope_tables(pos_ref, invf_ref, ATT_HEAD_DIM // 2)
    for which, g_ref, o_ref in ((0, gq_ref, q_ref), (1, gk_ref, k_ref)):
        for h in range(ATT_HEADS):
            lo = (which * ATT_HEADS + h) * ATT_HEAD_DIM
            s = x_ref[:, lo:lo + ATT_HEAD_DIM]
            y = s * lax.rsqrt(jnp.mean(s * s, axis=-1, keepdims=True) + EPS) * g_ref[...]
            y = y * cosf + _swap_halves(y, ATT_HEAD_DIM // 2, first) * sinf
            o_ref[0, h] = y.astype(o_ref.dtype)


def _qk_prep(proj, pos, b, t, gq, gk):
    tm = _tile(t, 256, SUBLANES)
    nt = t // tm
    hd = ATT_HEADS * ATT_HEAD_DIM
    ospec = pl.BlockSpec((1, ATT_HEADS, tm, ATT_HEAD_DIM), lambda bi, ti: (bi, 0, ti, 0))
    oshape = jax.ShapeDtypeStruct((b, ATT_HEADS, t, ATT_HEAD_DIM), BF16)
    vspec = pl.BlockSpec((1, LANES), lambda bi, ti: (0, 0))
    return pl.pallas_call(
        _qk_prep_kernel,
        grid=(b, nt),
        in_specs=[pl.BlockSpec((tm, 2 * hd), lambda bi, ti: (bi * nt + ti, 1)),
                  pl.BlockSpec((tm, 1), lambda bi, ti: (bi * nt + ti, 0)), vspec, vspec, vspec],
        out_specs=[ospec, ospec],
        out_shape=[oshape, oshape],
        compiler_params=_params(("parallel", "parallel")),
        name="qk_prep",
    )(proj, pos, _inv_freq_lanes(ATT_HEAD_DIM), gq.reshape(1, LANES), gk.reshape(1, LANES))


def _idx_prep_kernel(x_ref, pos_ref, invf_ref, qi_ref, ki_ref):
    half = IDX_HEAD_DIM // 2
    cosf, sinf, first = _rope_tables(pos_ref, invf_ref, half)
    lane = lax.broadcasted_iota(I32, cosf.shape, 1)
    left = lane < IDX_HEAD_DIM

    def rope(x):
        return x * cosf + _swap_halves(x, half, first) * sinf

    def split(y):
        hi = y.astype(BF16).astype(F32)
        return hi, y - hi

    for j in range(IDX_HEADS // 2):
        hi, lo = split(rope(x_ref[:, j * LANES:(j + 1) * LANES]))
        even = jnp.where(left, hi, pltpu.roll(lo, IDX_HEAD_DIM, 1)).astype(BF16)
        odd = jnp.where(left, pltpu.roll(hi, IDX_HEAD_DIM, 1), lo).astype(BF16)
        for h, v in ((2 * j, even), (2 * j + 1, odd)):
            qi_ref[0, h, :, 0:LANES] = v
            qi_ref[0, h, :, LANES:2 * LANES] = v
    kcol = IDX_HEADS * IDX_HEAD_DIM
    khi, klo = split(rope(x_ref[:, kcol:kcol + LANES]))
    ki_ref[0, :, 0:LANES] = jnp.where(left, khi, pltpu.roll(khi, IDX_HEAD_DIM, 1)).astype(BF16)
    ki_ref[0, :, LANES:2 * LANES] = jnp.where(left, klo, pltpu.roll(klo, IDX_HEAD_DIM, 1)).astype(BF16)


def _idx_prep(idx, pos, b, t):
    tm = _tile(t, 256, SUBLANES)
    nt = t // tm
    w = idx.shape[1]
    return pl.pallas_call(
        _idx_prep_kernel,
        grid=(b, nt),
        in_specs=[pl.BlockSpec((tm, w), lambda bi, ti: (bi * nt + ti, 0)),
                  pl.BlockSpec((tm, 1), lambda bi, ti: (bi * nt + ti, 0)),
                  pl.BlockSpec((1, LANES), lambda bi, ti: (0, 0))],
        out_specs=[pl.BlockSpec((1, IDX_HEADS, tm, 2 * LANES), lambda bi, ti: (bi, 0, ti, 0)),
                   pl.BlockSpec((1, tm, 2 * LANES), lambda bi, ti: (bi, ti, 0))],
        out_shape=[jax.ShapeDtypeStruct((b, IDX_HEADS, t, 2 * LANES), BF16),
                   jax.ShapeDtypeStruct((b, t, 2 * LANES), BF16)],
        compiler_params=_params(("parallel", "parallel")),
        name="idx_prep",
    )(idx, pos, _inv_freq_lanes(IDX_HEAD_DIM))


def _mla_prep_kernel(m_ref, pos_ref, invf_ref, qa_ref, kva_ref, wuq_ref, wukv_ref,
                     qnn_ref, qnr_ref, knn_ref, knr_ref, q_ref, k_ref, v_ref):
    half = MLA_ROPE // 2
    cosf, sinf, first = _rope_tables(pos_ref, invf_ref, half)
    lane = lax.broadcasted_iota(I32, cosf.shape, 1)
    left = lane < MLA_ROPE

    def rope(x):
        return x * cosf + _swap_halves(x, half, first) * sinf

    def norm(x, g_ref):
        return (x * lax.rsqrt(jnp.mean(x * x, axis=-1, keepdims=True) + EPS) * g_ref[...]).astype(BF16)

    cq = norm(m_ref[:, 0:MLA_Q_LORA], qa_ref)
    ckv = norm(m_ref[:, MLA_Q_LORA:MLA_Q_LORA + MLA_KV_LORA], kva_ref)
    kr = m_ref[:, MLA_Q_LORA + MLA_KV_LORA:MLA_Q_LORA + MLA_KV_LORA + LANES]
    qf = jnp.dot(cq, wuq_ref[...], preferred_element_type=F32)
    kvf = jnp.dot(ckv, wukv_ref[...], preferred_element_type=F32)
    nope_w = MLA_HEADS * MLA_NOPE
    v_ref[...] = kvf[:, nope_w:nope_w + MLA_HEADS * MLA_V].astype(v_ref.dtype)

    for j in range(MLA_HEADS // 2):
        rs = qf[:, nope_w + j * LANES:nope_w + (j + 1) * LANES]
        sq = rs * rs
        ss_pair = (jnp.sum(jnp.where(left, sq, 0.0), axis=-1, keepdims=True),
                   jnp.sum(jnp.where(left, 0.0, sq), axis=-1, keepdims=True))
        for par in range(2):
            h = 2 * j + par
            nope = qf[:, h * MLA_NOPE:(h + 1) * MLA_NOPE]
            ms = (jnp.sum(nope * nope, axis=-1, keepdims=True) + ss_pair[par]) * (1.0 / MLA_QK)
            rsq = lax.rsqrt(ms + EPS)
            q_ref[0, h, :, 0:LANES] = (nope * rsq * qnn_ref[...]).astype(q_ref.dtype)
            rr = rope(rs * rsq * qnr_ref[...])
            if par == 1:
                rr = pltpu.roll(rr, MLA_ROPE, 1)
            q_ref[0, h, :, LANES:2 * LANES] = jnp.where(left, rr, 0.0).astype(q_ref.dtype)

    ss_kr = jnp.sum(jnp.where(left, kr * kr, 0.0), axis=-1, keepdims=True)
    base = jnp.where(left, rope(kr * knr_ref[...]), 0.0)
    for h in range(MLA_HEADS):
        nope = kvf[:, h * MLA_NOPE:(h + 1) * MLA_NOPE]
        ms = (jnp.sum(nope * nope, axis=-1, keepdims=True) + ss_kr) * (1.0 / MLA_QK)
        rsq = lax.rsqrt(ms + EPS)
        k_ref[0, h, :, 0:LANES] = (nope * rsq * knn_ref[...]).astype(k_ref.dtype)
        k_ref[0, h, :, LANES:2 * LANES] = (base * rsq).astype(k_ref.dtype)


def _mla_prep(mla, pos, b, t, qa, kva, w_uq, w_ukv, qn, kn):
    tm = _tile(t, 256, SUBLANES)
    nt = t // tm
    wq = w_uq.reshape(MLA_Q_LORA, MLA_HEADS, MLA_QK)
    wq = jnp.concatenate([wq[:, :, :MLA_NOPE].reshape(MLA_Q_LORA, -1),
                          wq[:, :, MLA_NOPE:].reshape(MLA_Q_LORA, -1)], axis=1).astype(BF16)
    wkv = w_ukv.reshape(MLA_KV_LORA, MLA_HEADS, MLA_NOPE + MLA_V)
    wkv = jnp.concatenate([wkv[:, :, :MLA_NOPE].reshape(MLA_KV_LORA, -1),
                           wkv[:, :, MLA_NOPE:].reshape(MLA_KV_LORA, -1)], axis=1).astype(BF16)
    dup = lambda g: jnp.tile(g[MLA_NOPE:], 2).reshape(1, LANES)
    full = lambda a: pl.BlockSpec(a.shape, lambda bi, ti: (0,) * a.ndim)
    consts = [_inv_freq_lanes(MLA_ROPE), qa.reshape(1, -1), kva.reshape(1, -1), wq, wkv,
              qn[:MLA_NOPE].reshape(1, LANES), dup(qn), kn[:MLA_NOPE].reshape(1, LANES), dup(kn)]
    hspec = pl.BlockSpec((1, MLA_HEADS, tm, 2 * LANES), lambda bi, ti: (bi, 0, ti, 0))
    hshape = jax.ShapeDtypeStruct((b, MLA_HEADS, t, 2 * LANES), BF16)
    vw = MLA_HEADS * MLA_V
    return pl.pallas_call(
        _mla_prep_kernel,
        grid=(b, nt),
        in_specs=[pl.BlockSpec((tm, mla.shape[1]), lambda bi, ti: (bi * nt + ti, 0)),
                  pl.BlockSpec((tm, 1), lambda bi, ti: (bi * nt + ti, 0))] + [full(a) for a in consts],
        out_specs=[hspec, hspec, pl.BlockSpec((tm, vw), lambda bi, ti: (bi * nt + ti, 0))],
        out_shape=[hshape, hshape, jax.ShapeDtypeStruct((b * t, vw), BF16)],
        compiler_params=_params(("parallel", "parallel")),
        name="mla_prep",
    )(mla, pos, *consts)


def _attn_kernel(*refs, sparse, n_sel, tq, scale, idx_scale, t_total):
    if sparse:
        (q_ref, k_ref, vt_ref, qi_ref, ki_ref, wt_ref, o_ref, m_scr, l_scr, acc_scr,
         score_scr, bias_scr, thr_scr, tie_scr) = refs
    else:
        q_ref, k_ref, vt_ref, o_ref, m_scr, l_scr, acc_scr = refs
    n_heads, dv = vt_ref.shape[1], vt_ref.shape[3]
    qb = pl.program_id(1)
    nkc = qb + 1
    kio = lax.broadcasted_iota(I32, (tq, tq), 0)
    qio = lax.broadcasted_iota(I32, (tq, tq), 1)
    neg_inf = -jnp.inf

    def chunk_off(c):
        return pl.multiple_of(c * tq, tq)

    def causal(c):
        return (c * tq + kio) <= (qb * tq + qio)

    if sparse:
        n_idx_heads = qi_ref.shape[1]

        def score_chunk(c, carry):
            koff = chunk_off(c)
            ki = ki_ref[0, pl.ds(koff, tq), :]
            acc = jnp.zeros((tq, tq), F32)
            for h in range(n_idx_heads):
                s = lax.dot_general(ki, qi_ref[0, h], NT_DIMS, preferred_element_type=F32)
                acc = acc + jnp.maximum(s, 0.0) * wt_ref[0, h:h + 1, :]
            score_scr[pl.ds(koff, tq), :] = jnp.where(causal(c), acc * idx_scale, neg_inf)
            return carry

        lax.fori_loop(0, nkc, score_chunk, 0)

        def key_to_float(key):
            val = pltpu.bitcast(jnp.where(key < 0, key ^ 0x7FFFFFFF, key), F32)
            return jnp.where(key < NEG_INF_KEY, neg_inf, val)

        def count(pred):
            def body(c, acc):
                x = score_scr[pl.ds(chunk_off(c), tq), :]
                m = jnp.where(pred(x, c), 1.0, 0.0)
                return acc + jnp.sum(m.reshape(tq // SUBLANES, SUBLANES, tq), axis=0)
            acc = lax.fori_loop(0, nkc, body, jnp.zeros((SUBLANES, tq), F32))
            return jnp.sum(acc, axis=0, keepdims=True)

        def count_ge(key):
            cand = key_to_float(key)
            return count(lambda x, c: x >= cand)

        k_sel = jnp.float32(n_sel)
        zero = jnp.zeros((1, tq), I32)
        thr_key = jnp.where(count_ge(zero) >= k_sel, zero, jnp.full((1, tq), INT_MIN, I32))

        def bit_step(i, key):
            cand = key | lax.shift_left(jnp.int32(1), jnp.int32(30) - i)
            return jnp.where(count_ge(cand) >= k_sel, cand, key)

        thr_key = lax.fori_loop(0, 31, bit_step, thr_key)
        thr = key_to_float(thr_key)
        n_ge = count(lambda x, c: x >= thr)
        thr_scr[0:1, :] = thr
        tie_scr[0:1, :] = jnp.full((1, tq), t_total, I32)

        @pl.when(jnp.max(n_ge) > k_sel)
        def _():
            need = k_sel - count(lambda x, c: x > thr)

            def below(x, c, m):
                return jnp.logical_and(x == thr, (c * tq + kio) < m)

            def idx_step(i, lo):
                cand = lo | lax.shift_left(jnp.int32(1), jnp.int32(t_total.bit_length() - 1) - i)
                return jnp.where(count(lambda x, c: below(x, c, cand)) < need, cand, lo)

            lo = lax.fori_loop(0, t_total.bit_length(), idx_step, zero)
            tie_scr[0:1, :] = lo + 1

        thr = thr_scr[0:1, :]
        tie_end = tie_scr[0:1, :]

        def bias_chunk(c, carry):
            koff = chunk_off(c)
            x = score_scr[pl.ds(koff, tq), :]
            tie_ok = jnp.logical_and(x == thr, (c * tq + kio) < tie_end)
            keep = jnp.logical_and(jnp.logical_or(x > thr, tie_ok), causal(c))
            bias_scr[pl.ds(koff, tq), :] = jnp.where(keep, 0.0, neg_inf)
            return carry

        lax.fori_loop(0, nkc, bias_chunk, 0)

    m_scr[...] = jnp.full(m_scr.shape, neg_inf, F32)
    l_scr[...] = jnp.zeros(l_scr.shape, F32)
    acc_scr[...] = jnp.zeros(acc_scr.shape, F32)

    def chunk_step(c, masked):
        koff = chunk_off(c)
        if sparse:
            bias = bias_scr[pl.ds(koff, tq), :]
        elif masked:
            allowed = causal(c)
        def qk(h):
            return lax.dot_general(k_ref[0, h, pl.ds(koff, tq), :], q_ref[0, h], NT_DIMS,
                                   preferred_element_type=F32)

        queued = [qk(h) for h in range(QK_LOOKAHEAD)]
        for h in range(n_heads):
            s = queued.pop(0) * (scale * LOG2_E)
            if h + QK_LOOKAHEAD < n_heads:
                queued.append(qk(h + QK_LOOKAHEAD))
            if sparse:
                s = s + bias
            elif masked:
                s = jnp.where(allowed, s, neg_inf)
            m_old = m_scr[h]
            m_new = jnp.maximum(m_old, jnp.max(s, axis=0, keepdims=True))
            m_ref = jnp.where(m_new == neg_inf, 0.0, m_new)
            alpha = jnp.exp2(m_old - m_ref)
            p = jnp.exp2(s - m_ref)
            l_scr[h] = alpha * l_scr[h] + jnp.sum(p, axis=0, keepdims=True)
            acc_scr[h] = alpha * acc_scr[h] + jnp.dot(vt_ref[0, h, c], p.astype(BF16),
                                                      preferred_element_type=F32)
            m_scr[h] = m_new

    def off_diagonal(c, carry):
        chunk_step(c, False)
        return carry

    if sparse:
        lax.fori_loop(0, nkc, off_diagonal, 0)
    else:
        lax.fori_loop(0, qb, off_diagonal, 0)
        chunk_step(qb, True)
    for h in range(n_heads):
        o_ref[0, h] = (acc_scr[h] / l_scr[h]).astype(o_ref.dtype)


def _attention(q, k, v, b, t, scale, sparse_inputs=None, n_sel=0):
    h, dq = q.shape[1], q.shape[3]
    dv = v.shape[1] // h
    tq = _tile(t, 256, LANES)
    nq = t // tq
    vt = v.reshape(b, nq, tq, h, dv).transpose(0, 3, 1, 4, 2)
    in_specs = [pl.BlockSpec((1, h, tq, dq), lambda bi, qi: (bi, 0, qi, 0)),
                pl.BlockSpec((1, h, t, dq), lambda bi, qi: (bi, 0, 0, 0)),
                pl.BlockSpec((1, h, nq, dv, tq), lambda bi, qi: (bi, 0, 0, 0, 0))]
    args = [q, k, vt]
    scratch = [pltpu.VMEM((h, 1, tq), F32), pltpu.VMEM((h, 1, tq), F32), pltpu.VMEM((h, dv, tq), F32)]
    sparse = sparse_inputs is not None
    if sparse:
        qi_, ki_, wt_ = sparse_inputs
        hi, dk = qi_.shape[1], qi_.shape[3]
        in_specs += [pl.BlockSpec((1, hi, tq, dk), lambda bi, qi: (bi, 0, qi, 0)),
                     pl.BlockSpec((1, t, dk), lambda bi, qi: (bi, 0, 0)),
                     pl.BlockSpec((1, hi, tq), lambda bi, qi: (bi, 0, qi))]
        args += [qi_, ki_, wt_]
        scratch += [pltpu.VMEM((t, tq), F32), pltpu.VMEM((t, tq), F32),
                    pltpu.VMEM((SUBLANES, tq), F32), pltpu.VMEM((SUBLANES, tq), I32)]
    out = pl.pallas_call(
        functools.partial(_attn_kernel, sparse=sparse, n_sel=n_sel, tq=tq, scale=scale,
                          idx_scale=IDX_HEADS ** -0.5 * IDX_HEAD_DIM ** -0.5, t_total=t),
        grid=(b, nq),
        in_specs=in_specs,
        out_specs=pl.BlockSpec((1, h, dv, tq), lambda bi, qi: (bi, 0, 0, qi)),
        out_shape=jax.ShapeDtypeStruct((b, h, dv, t), BF16),
        scratch_shapes=scratch,
        compiler_params=_params(("parallel", "arbitrary")),
        name="sparse_attention" if sparse else "dense_attention",
    )(*args)
    return out.transpose(0, 3, 1, 2).reshape(b * t, h * dv)


def _merge_kernel(hn_ref, a_ref, b_ref, c_ref, wg_ref, wb_ref, o_ref):
    hn = hn_ref[...]
    acc = None
    for n, br_ref in enumerate((a_ref, b_ref, c_ref)):
        gate = jax.nn.sigmoid(jnp.dot(hn, wg_ref[n], preferred_element_type=F32))
        term = gate * jnp.dot(br_ref[...], wb_ref[n], preferred_element_type=F32)
        acc = term if acc is None else acc + term
    o_ref[...] = acc.astype(o_ref.dtype)


def _merge(hn, branches, w_gate, w_branch):
    m, d = hn.shape
    bw = w_branch.shape[1]
    tm = _tile(m, 512, SUBLANES)
    tn = _tile(d, 512)
    bspec = pl.BlockSpec((tm, bw), lambda i, j: (i, 0))
    return pl.pallas_call(
        _merge_kernel,
        grid=(m // tm, d // tn),
        in_specs=[pl.BlockSpec((tm, d), lambda i, j: (i, 0)), bspec, bspec, bspec,
                  pl.BlockSpec((N_BRANCH, d, tn), lambda i, j: (0, 0, j)),
                  pl.BlockSpec((N_BRANCH, bw, tn), lambda i, j: (0, 0, j))],
        out_specs=pl.BlockSpec((tm, tn), lambda i, j: (i, j)),
        out_shape=jax.ShapeDtypeStruct((m, d), BF16),
        compiler_params=_params(("parallel", "arbitrary")),
        name="gated_merge",
    )(hn, *branches, w_gate, w_branch)


def _ple_kernel(h_ref, hn_ref, p_ref, wg_ref, wp_ref, o_ref):
    gate = jax.nn.sigmoid(jnp.dot(hn_ref[...], wg_ref[...], preferred_element_type=F32))
    emb = jnp.dot(p_ref[...].astype(BF16), wp_ref[...], preferred_element_type=F32)
    o_ref[...] = h_ref[...] + emb * gate


def _ple(h, hn, p, w_gate, w_proj):
    m, d = h.shape
    pd = p.shape[1]
    tm = _tile(m, 1024, SUBLANES)
    tn = _tile(d, 512)
    return pl.pallas_call(
        _ple_kernel,
        grid=(m // tm, d // tn),
        in_specs=[pl.BlockSpec((tm, tn), lambda i, j: (i, j)), pl.BlockSpec((tm, d), lambda i, j: (i, 0)),
                  pl.BlockSpec((tm, pd), lambda i, j: (i, 0)), pl.BlockSpec((d, tn), lambda i, j: (0, j)),
                  pl.BlockSpec((pd, tn), lambda i, j: (0, j))],
        out_specs=pl.BlockSpec((tm, tn), lambda i, j: (i, j)),
        out_shape=jax.ShapeDtypeStruct((m, d), F32),
        compiler_params=_params(("parallel", "arbitrary")),
        name="ple",
    )(h, hn, p, w_gate, w_proj)


def _ffn_kernel(be_ref, nu_ref, x_ref, wg_ref, wu_ref, wd_ref, *rest, has_res):
    if has_res:
        res_ref, o_ref = rest
    else:
        (o_ref,) = rest
    i, j = pl.program_id(0), pl.program_id(1)

    @pl.when(j == 0)
    def _():
        o_ref[...] = res_ref[...] if has_res else jnp.zeros_like(o_ref)

    @pl.when(i < nu_ref[0])
    def _():
        x = x_ref[...].astype(BF16)
        g = jnp.dot(x, wg_ref[0].astype(BF16), preferred_element_type=F32)
        u = jnp.dot(x, wu_ref[0].astype(BF16), preferred_element_type=F32)
        act = (g * jax.nn.sigmoid(g) * u).astype(BF16)
        o_ref[...] += jnp.dot(act, wd_ref[0].astype(BF16), preferred_element_type=F32)


def _ffn(x, w_gate, w_up, w_down, blk_expert, n_used, tm, res=None):
    r, d = x.shape
    f = w_gate.shape[2]
    tf = _tile(f, 512)
    nf = f // tf
    assert nf >= 2

    def jj(i, j, nu):
        return jnp.where(i < nu[0], j, nf - 1)

    rows_mode = pl.Buffered(1) if w_gate.dtype == F32 else None
    in_specs = [pl.BlockSpec((tm, d), lambda i, j, be, nu: (i, 0), pipeline_mode=rows_mode),
                pl.BlockSpec((1, d, tf), lambda i, j, be, nu: (be[i], 0, jj(i, j, nu))),
                pl.BlockSpec((1, d, tf), lambda i, j, be, nu: (be[i], 0, jj(i, j, nu))),
                pl.BlockSpec((1, tf, d), lambda i, j, be, nu: (be[i], jj(i, j, nu), 0))]
    args = [x, w_gate, w_up, w_down]
    if res is not None:
        in_specs.append(pl.BlockSpec((tm, d), lambda i, j, be, nu: (i, 0)))
        args.append(res)
    return pl.pallas_call(
        functools.partial(_ffn_kernel, has_res=res is not None),
        grid_spec=pltpu.PrefetchScalarGridSpec(
            num_scalar_prefetch=2,
            grid=(r // tm, nf),
            in_specs=in_specs,
            out_specs=pl.BlockSpec((tm, d), lambda i, j, be, nu: (i, 0), pipeline_mode=rows_mode)),
        out_shape=jax.ShapeDtypeStruct((r, d), F32),
        compiler_params=_params(("arbitrary", "arbitrary")),
        name="swiglu",
    )(blk_expert, n_used, *args)


def _router_kernel(h_ref, g_ref, rhi_ref, rlo_ref, hn_ref, info_ref, cnt_ref, run_scr, *, n_experts):
    i = pl.program_id(0)
    tm = h_ref.shape[0]

    @pl.when(i == 0)
    def _():
        run_scr[...] = jnp.zeros_like(run_scr)

    x = h_ref[...]
    xn = x * lax.rsqrt(jnp.mean(x * x, axis=-1, keepdims=True) + EPS) * g_ref[...]
    hi = xn.astype(BF16)
    hn_ref[...] = xn
    lo = (xn - hi.astype(F32)).astype(BF16)
    logits = (jnp.dot(hi, rhi_ref[...], preferred_element_type=F32)
              + jnp.dot(hi, rlo_ref[...], preferred_element_type=F32)
              + jnp.dot(lo, rhi_ref[...], preferred_element_type=F32))
    lane = lax.broadcasted_iota(I32, logits.shape, 1)
    lane_f = lane.astype(F32)
    logits = jnp.where(lane < n_experts, logits, -jnp.inf)

    def top(vals):
        v = jnp.max(vals, axis=-1, keepdims=True)
        idx = jnp.min(jnp.where(vals == v, lane_f, float(LANES)), axis=-1, keepdims=True)
        return v, idx

    v1, i1 = top(logits)
    v2, i2 = top(jnp.where(lane_f == i1, -jnp.inf, logits))
    e2 = jnp.exp(v2 - v1)
    g1 = 1.0 / (1.0 + e2)
    g2 = e2 / (1.0 + e2)

    oh1 = lane_f == i1
    oh2 = lane_f == i2
    both = jnp.where(jnp.logical_or(oh1, oh2), 1.0, 0.0)
    r_io = lax.broadcasted_iota(I32, (tm, tm), 0)
    c_io = lax.broadcasted_iota(I32, (tm, tm), 1)
    strict_lower = jnp.where(c_io < r_io, 1.0, 0.0).astype(BF16)
    before = jnp.dot(strict_lower, both.astype(BF16), preferred_element_type=F32) + run_scr[0:1, :]
    rank1 = jnp.sum(jnp.where(oh1, before, 0.0), axis=-1, keepdims=True)
    rank2 = jnp.sum(jnp.where(oh2, before, 0.0), axis=-1, keepdims=True)
    run_scr[0:1, :] = run_scr[0:1, :] + jnp.sum(both, axis=0, keepdims=True)

    info = jnp.zeros(logits.shape, F32)
    for col, val in enumerate((i1, i2, g1, g2, rank1, rank2)):
        info = jnp.where(lane == col, val, info)
    info_ref[...] = info
    cnt_ref[...] = jnp.broadcast_to(run_scr[0:1, :], cnt_ref.shape)


def _router(h, g, router):
    m, d = h.shape
    e = router.shape[1]
    tm = _tile(m, 256, SUBLANES)
    rp = jnp.zeros((d, LANES), F32).at[:, :e].set(router)
    rhi = rp.astype(BF16)
    rlo = (rp - rhi.astype(F32)).astype(BF16)
    return pl.pallas_call(
        functools.partial(_router_kernel, n_experts=e),
        grid=(m // tm,),
        in_specs=[pl.BlockSpec((tm, d), lambda i: (i, 0)), pl.BlockSpec((1, d), lambda i: (0, 0)),
                  pl.BlockSpec((d, LANES), lambda i: (0, 0)), pl.BlockSpec((d, LANES), lambda i: (0, 0))],
        out_specs=[pl.BlockSpec((tm, d), lambda i: (i, 0)), pl.BlockSpec((tm, LANES), lambda i: (i, 0)),
                   pl.BlockSpec((SUBLANES, LANES), lambda i: (0, 0))],
        out_shape=[jax.ShapeDtypeStruct((m, d), F32), jax.ShapeDtypeStruct((m, LANES), F32),
                   jax.ShapeDtypeStruct((SUBLANES, LANES), F32)],
        scratch_shapes=[pltpu.VMEM((SUBLANES, LANES), F32)],
        compiler_params=_params(("arbitrary",)),
        name="router",
    )(h, g.reshape(1, d), rhi, rlo)


def _row_copy(src, dst, sem, s, d):
    return pltpu.make_async_copy(src.at[pl.ds(s, 1), :], dst.at[pl.ds(d, 1), :], sem)


def _dispatch_kernel(dest_ref, x_ref, xs_in_hbm, xs_hbm, sem, *, rows):
    del xs_in_hbm

    def issue(r, carry):
        for kk in range(TOP_K):
            _row_copy(x_ref, xs_hbm, sem, r, dest_ref[0, 0, TOP_K * r + kk]).start()
        return carry

    lax.fori_loop(0, rows, issue, 0, unroll=8)
    for kk in range(TOP_K):
        pltpu.make_async_copy(x_ref, xs_hbm.at[pl.ds(0, rows), :], sem).wait()


def _dispatch(x, dest, n_rows):
    m, d = x.shape
    rows = _tile(m, 512, SUBLANES)
    return pl.pallas_call(
        functools.partial(_dispatch_kernel, rows=rows),
        grid=(m // rows,),
        in_specs=[pl.BlockSpec((1, 1, TOP_K * rows), lambda i: (i, 0, 0), memory_space=pltpu.SMEM),
                  pl.BlockSpec((rows, d), lambda i: (i, 0)), pl.BlockSpec(memory_space=pl.ANY)],
        out_specs=pl.BlockSpec(memory_space=pl.ANY),
        out_shape=jax.ShapeDtypeStruct((n_rows, d), x.dtype),
        scratch_shapes=[pltpu.SemaphoreType.DMA(())],
        input_output_aliases={2: 0},
        compiler_params=_params(("arbitrary",)),
        name="moe_dispatch",
    )(dest.reshape(m // rows, 1, TOP_K * rows), x, jnp.zeros((n_rows, d), x.dtype))


def _combine_kernel(dest_ref, h_ref, info_ref, g_ref, ys_hbm, o_ref, on_ref, buf, sem, *, rows):
    def issue(r, carry):
        for kk in range(TOP_K):
            pltpu.make_async_copy(ys_hbm.at[pl.ds(dest_ref[0, 0, TOP_K * r + kk], 1), :],
                                  buf.at[kk, pl.ds(r, 1), :], sem).start()
        return carry

    lax.fori_loop(0, rows, issue, 0, unroll=8)
    for kk in range(TOP_K):
        pltpu.make_async_copy(ys_hbm.at[pl.ds(0, rows), :], buf.at[kk], sem).wait()
    info = info_ref[...]
    out = h_ref[...] + (info[:, 2:3] * buf[0] + info[:, 3:4] * buf[1])
    o_ref[...] = out
    on_ref[...] = (out * lax.rsqrt(jnp.mean(out * out, axis=-1, keepdims=True) + EPS)
                   * g_ref[...]).astype(on_ref.dtype)


def _combine(h, info, dest, ys, g):
    m, d = h.shape
    rows = _tile(m, 256, SUBLANES)
    return pl.pallas_call(
        functools.partial(_combine_kernel, rows=rows),
        grid=(m // rows,),
        in_specs=[pl.BlockSpec((1, 1, TOP_K * rows), lambda i: (i, 0, 0), memory_space=pltpu.SMEM),
                  pl.BlockSpec((rows, d), lambda i: (i, 0)), pl.BlockSpec((rows, LANES), lambda i: (i, 0)),
                  pl.BlockSpec((1, d), lambda i: (0, 0)), pl.BlockSpec(memory_space=pl.ANY)],
        out_specs=[pl.BlockSpec((rows, d), lambda i: (i, 0)), pl.BlockSpec((rows, d), lambda i: (i, 0))],
        out_shape=[jax.ShapeDtypeStruct((m, d), F32), jax.ShapeDtypeStruct((m, d), BF16)],
        scratch_shapes=[pltpu.VMEM((TOP_K, rows, d), F32), pltpu.SemaphoreType.DMA(())],
        compiler_params=_params(("arbitrary",)),
        name="moe_combine",
    )(dest.reshape(m // rows, 1, TOP_K * rows), h, info, g.reshape(1, d), ys)


def _moe(h, g_ffn, router, w_gate, w_up, w_down, g_next):
    m, d = h.shape
    e = router.shape[1]
    hn, info, cnt = _router(h, g_ffn, router)
    counts = cnt[0, :e].astype(I32)
    padded = (counts + MOE_ROWS - 1) // MOE_ROWS * MOE_ROWS
    pad_ends = jnp.cumsum(padded)
    pad_starts = pad_ends - padded
    n_blocks = -(-(m * TOP_K) // MOE_ROWS) + e
    ids = info[:, 0:TOP_K].astype(I32)
    dest = jnp.sum(jnp.where(ids[:, :, None] == jnp.arange(e)[None, None, :], pad_starts[None, None, :], 0),
                   axis=-1) + info[:, 4:4 + TOP_K].astype(I32)
    blk_start = jnp.arange(n_blocks, dtype=I32) * MOE_ROWS
    blk_expert = jnp.minimum(jnp.sum(blk_start[:, None] >= pad_ends[None, :], axis=1), e - 1).astype(I32)
    n_used = (pad_ends[e - 1] // MOE_ROWS).astype(I32).reshape(1)
    xs = _dispatch(hn, dest, n_blocks * MOE_ROWS)
    ys = _ffn(xs, w_gate, w_up, w_down, blk_expert, n_used, MOE_ROWS)
    return _combine(h, info, dest, ys, g_next)


N_MAIN = 2 * LRU_WIDTH + 2 * ATT_HEADS * ATT_HEAD_DIM
O_V = N_MAIN
O_IDX = O_V + ATT_HEADS * ATT_HEAD_DIM
N_IDX = IDX_HEADS * IDX_HEAD_DIM + IDX_HEAD_DIM + IDX_HEADS
IDX_WINDOW = 1280
O_MLA = O_IDX + N_IDX
N_MLA = MLA_Q_LORA + MLA_KV_LORA + MLA_ROPE
O_GATE = O_MLA + N_MLA
assert O_IDX % IDX_WINDOW == 0 and N_IDX <= IDX_WINDOW


def _split_w_in(w):
    d = w.shape[0]
    w_mla = jnp.pad(w[:, O_MLA:O_GATE], ((0, 0), (0, -N_MLA % LANES)))
    w_gates = w[:, O_GATE:].reshape(d, N_BRANCH, d).transpose(1, 0, 2)
    return w_mla.astype(BF16), w_gates.astype(BF16)


def kernel(x, p, positions, ln_mix, w_in, conv_w, conv_b, lru_wa, lru_ba, lru_wx, lru_bx, lru_lambda,
           att_q_norm, att_k_norm, mla_qa_norm, mla_kva_norm, mla_w_uq, mla_w_ukv, mla_q_norm,
           mla_k_norm, w_branch, w_out, ln_ffn, dense_w_gate, dense_w_up, dense_w_down, moe_router,
           moe_w_gate, moe_w_up, moe_w_down, ple_norm, ple_w_gate, ple_w_proj):
    b, t, d = x.shape
    m = b * t
    depth = w_in.shape[0]
    n_sel = min(TOPK_MAX, t // 4)
    pos = positions.reshape(m, 1).astype(I32)
    h = x.reshape(m, d)
    idx_w_off = IDX_HEADS * IDX_HEAD_DIM + IDX_HEAD_DIM
    for i in range(depth):
        w_mla, w_gates = _split_w_in(w_in[i])
        hn = _rmsnorm(h, ln_mix[i])
        proj = _matmul(hn, w_in, F32, cols=(0, N_MAIN), layer=i, name="in_proj_main")
        v = _matmul(hn, w_in, BF16, cols=(O_V, O_IDX - O_V), layer=i, name="in_proj_v")
        idx = _matmul(hn, w_in, F32, cols=(O_IDX, IDX_WINDOW), tn_cap=IDX_WINDOW, layer=i,
                      name="in_proj_idx")
        mla = _matmul(hn, w_mla, F32, tn_cap=w_mla.shape[1], name="in_proj_mla")

        out_a = _rglru(proj, b, t, conv_w[i], conv_b[i], lru_wa[i], lru_ba[i], lru_wx[i], lru_bx[i],
                       lru_lambda[i])

        q, k = _qk_prep(proj, pos, b, t, att_q_norm[i], att_k_norm[i])
        qi, ki = _idx_prep(idx, pos, b, t)
        wt = idx[:, idx_w_off:idx_w_off + IDX_HEADS].reshape(b, t, IDX_HEADS).transpose(0, 2, 1)
        out_b = _attention(q, k, v, b, t, ATT_HEAD_DIM ** -0.5, sparse_inputs=(qi, ki, wt), n_sel=n_sel)

        mq, mk, mv = _mla_prep(mla, pos, b, t, mla_qa_norm[i], mla_kva_norm[i], mla_w_uq[i],
                               mla_w_ukv[i], mla_q_norm[i], mla_k_norm[i])
        out_c = _attention(mq, mk, mv, b, t, MLA_QK ** -0.5)

        merged = _merge(hn, (out_a, out_b, out_c), w_gates, w_branch[i].astype(BF16))
        h = _matmul(merged, w_out[i].astype(BF16), F32, res=h, name="out_proj")

        if i % 2 == 0:
            j = i // 2
            hn2 = _rmsnorm(h, ln_ffn[i])
            tm = _tile(m, 512, SUBLANES)
            h = _ffn(hn2, dense_w_gate[j:j + 1].astype(BF16), dense_w_up[j:j + 1].astype(BF16),
                     dense_w_down[j:j + 1].astype(BF16), jnp.zeros((m // tm,), I32),
                     jnp.full((1,), m // tm, I32), tm, res=h)
            hn3 = _rmsnorm(h, ple_norm[i])
        else:
            j = i // 2
            h, hn3 = _moe(h, ln_ffn[i], moe_router[j], moe_w_gate[j], moe_w_up[j], moe_w_down[j],
                          ple_norm[i])
        h = _ple(h, hn3, p[i].reshape(m, -1), ple_w_gate[i].astype(BF16), ple_w_proj[i].astype(BF16))
    return h.reshape(b, t, d)
```

```python
import functools

import jax
import jax.numpy as jnp
from jax import lax
from jax.experimental import pallas as pl
from jax.experimental.pallas import tpu as pltpu

F32, BF16, I32 = jnp.float32, jnp.bfloat16, jnp.int32

EPS = 1e-6
ROPE_THETA = 10000.0
LRU_C = 8.0
LRU_WIDTH = 1024
LRU_BLOCKS = 8
ATT_HEADS = 8
ATT_HEAD_DIM = 128
IDX_HEADS = 16
IDX_HEAD_DIM = 64
TOPK_MAX = 256
MLA_HEADS = 8
MLA_Q_LORA = 768
MLA_KV_LORA = 512
MLA_NOPE = 128
MLA_ROPE = 64
MLA_QK = MLA_NOPE + MLA_ROPE
MLA_V = 128
N_BRANCH = 3
N_EXPERTS = 8
TOP_K = 2

LANES = 128
SUBLANES = 8
VMEM_LIMIT_BYTES = 56 << 20
MOE_ROWS = 768
INT_MIN = -(2 ** 31)
NEG_INF_KEY = 0x807FFFFF - 2 ** 32

NT_DIMS = (((1,), (1,)), ((), ()))
LOG2_E = 1.4426950408889634
QK_LOOKAHEAD = 4


def _params(sem):
    return pltpu.CompilerParams(dimension_semantics=sem, vmem_limit_bytes=VMEM_LIMIT_BYTES)


def _tile(n, cap, unit=LANES):
    if n <= cap:
        return n
    best = None
    for t in range(unit, cap + 1, unit):
        if n % t == 0:
            best = t
    assert best is not None, (n, cap)
    return best


def _rmsnorm_kernel(x_ref, g_ref, o_ref):
    x = x_ref[...]
    ms = jnp.mean(x * x, axis=-1, keepdims=True)
    o_ref[...] = (x * lax.rsqrt(ms + EPS) * g_ref[...]).astype(o_ref.dtype)


def _rmsnorm(x, g):
    m, d = x.shape
    tm = _tile(m, 512, SUBLANES)
    return pl.pallas_call(
        _rmsnorm_kernel,
        grid=(m // tm,),
        in_specs=[pl.BlockSpec((tm, d), lambda i: (i, 0)), pl.BlockSpec((1, d), lambda i: (0, 0))],
        out_specs=pl.BlockSpec((tm, d), lambda i: (i, 0)),
        out_shape=jax.ShapeDtypeStruct((m, d), BF16),
        compiler_params=_params(("parallel",)),
        name="rmsnorm",
    )(x, g.reshape(1, d))


def _mm_kernel(x_ref, w_ref, o_ref):
    o_ref[...] = jnp.dot(x_ref[...], w_ref[...], preferred_element_type=F32).astype(o_ref.dtype)


def _mm_nt_kernel(x_ref, wt_ref, o_ref):
    o_ref[...] = lax.dot_general(x_ref[...], wt_ref[...], NT_DIMS,
                                 preferred_element_type=F32).astype(o_ref.dtype)


def _mm_res_kernel(x_ref, w_ref, r_ref, o_ref):
    o_ref[...] = r_ref[...] + jnp.dot(x_ref[...], w_ref[...], preferred_element_type=F32)


def _matmul(x, w, out_dtype, res=None, tm_cap=1024, tn_cap=1024, transposed=False, name="matmul"):
    m, k = x.shape
    n = w.shape[0] if transposed else w.shape[1]
    tm = _tile(m, tm_cap, SUBLANES)
    tn = _tile(n, tn_cap)
    if transposed:
        w_spec = pl.BlockSpec((tn, k), lambda i, j: (j, 0))
    else:
        w_spec = pl.BlockSpec((k, tn), lambda i, j: (0, j))
    in_specs = [pl.BlockSpec((tm, k), lambda i, j: (i, 0)), w_spec]
    args = [x, w]
    body = _mm_nt_kernel if transposed else _mm_kernel
    if res is not None:
        assert not transposed
        in_specs.append(pl.BlockSpec((tm, tn), lambda i, j: (i, j)))
        args.append(res)
        body = _mm_res_kernel
    return pl.pallas_call(
        body,
        grid=(m // tm, n // tn),
        in_specs=in_specs,
        out_specs=pl.BlockSpec((tm, tn), lambda i, j: (i, j)),
        out_shape=jax.ShapeDtypeStruct((m, n), out_dtype),
        compiler_params=_params(("parallel", "arbitrary")),
        name=name,
    )(*args)


EXPM1_SERIES_BOUND = 0.25
EXPM1_SERIES_TERMS = 10


def _expm1(y):
    poly = jnp.full_like(y, 1.0 / 3628800.0)
    fact = 3628800.0
    for n in range(EXPM1_SERIES_TERMS, 1, -1):
        fact /= n
        poly = poly * y + 1.0 / fact
    return jnp.where(jnp.abs(y) < EXPM1_SERIES_BOUND, poly * y, jnp.exp(y) - 1.0)


def _rglru_kernel(x_ref, g_ref, cw_ref, cb_ref, wa_ref, ba_ref, wx_ref, bx_ref, lam_ref, o_ref,
                  xs_scr, a_scr, b_scr, h_scr, *, tt):
    c = x_ref.shape[1]
    t = pl.program_id(1)

    @pl.when(t == 0)
    def _():
        xs_scr[0:SUBLANES, :] = jnp.zeros((SUBLANES, c), F32)
        h_scr[...] = jnp.zeros_like(h_scr)

    x = x_ref[...]
    xs_scr[SUBLANES:SUBLANES + tt, :] = x
    cw = cw_ref[...]
    xc = (xs_scr[SUBLANES - 3:SUBLANES - 3 + tt, :] * cw[0:1, :]
          + xs_scr[SUBLANES - 2:SUBLANES - 2 + tt, :] * cw[1:2, :]
          + xs_scr[SUBLANES - 1:SUBLANES - 1 + tt, :] * cw[2:3, :]
          + x * cw[3:4, :]) + cb_ref[...]
    xs_scr[0:SUBLANES, :] = x[tt - SUBLANES:tt, :]

    xcb = xc.astype(BF16)
    bw = c // LRU_BLOCKS
    ra = jnp.concatenate(
        [jnp.dot(xcb[:, n * bw:(n + 1) * bw], wa_ref[n], preferred_element_type=F32)
         for n in range(LRU_BLOCKS)], axis=1) + ba_ref[...]
    rx = jnp.concatenate(
        [jnp.dot(xcb[:, n * bw:(n + 1) * bw], wx_ref[n], preferred_element_type=F32)
         for n in range(LRU_BLOCKS)], axis=1) + bx_ref[...]
    r = jax.nn.sigmoid(ra)
    gi = jax.nn.sigmoid(rx)
    nlam = -lam_ref[...]
    softplus = jnp.maximum(nlam, 0.0) + jnp.log1p(jnp.exp(-jnp.abs(nlam)))
    log_a = (-LRU_C) * r * softplus
    a_scr[...] = jnp.exp(log_a)
    b_scr[...] = jnp.sqrt(-_expm1(2.0 * log_a)) * (gi * xc)

    row = lax.broadcasted_iota(I32, (SUBLANES, c), 0)

    def group(gidx, h):
        off = pl.multiple_of(gidx * SUBLANES, SUBLANES)
        a8 = a_scr[pl.ds(off, SUBLANES), :]
        b8 = b_scr[pl.ds(off, SUBLANES), :]
        for s in (1, 2, 4):
            keep = row >= s
            a_sh = jnp.where(keep, pltpu.roll(a8, s, 0), 1.0)
            b_sh = jnp.where(keep, pltpu.roll(b8, s, 0), 0.0)
            b8 = a8 * b_sh + b8
            a8 = a8 * a_sh
        h8 = a8 * h + b8
        b_scr[pl.ds(off, SUBLANES), :] = h8
        return h8[SUBLANES - 1:SUBLANES, :]

    h_scr[...] = lax.fori_loop(0, tt // SUBLANES, group, h_scr[...])
    o_ref[...] = (b_scr[...] * jax.nn.gelu(g_ref[...], approximate=True)).astype(o_ref.dtype)


def _rglru(proj, b, t, conv_w, conv_b, wa, ba, wx, bx, lam):
    c = LRU_WIDTH
    tt = _tile(t, 256, SUBLANES)
    nt = t // tt
    row = lambda v: v.reshape(1, c)
    wspec = pl.BlockSpec(wa.shape, lambda bi, ti: (0, 0, 0))
    vspec = pl.BlockSpec((1, c), lambda bi, ti: (0, 0))
    return pl.pallas_call(
        functools.partial(_rglru_kernel, tt=tt),
        grid=(b, nt),
        in_specs=[pl.BlockSpec((tt, c), lambda bi, ti: (bi * nt + ti, 0)),
                  pl.BlockSpec((tt, c), lambda bi, ti: (bi * nt + ti, 1)),
                  pl.BlockSpec(conv_w.shape, lambda bi, ti: (0, 0)), vspec,
                  wspec, vspec, wspec, vspec, vspec],
        out_specs=pl.BlockSpec((tt, c), lambda bi, ti: (bi * nt + ti, 0)),
        out_shape=jax.ShapeDtypeStruct((b * t, c), BF16),
        scratch_shapes=[pltpu.VMEM((tt + SUBLANES, c), F32), pltpu.VMEM((tt, c), F32),
                        pltpu.VMEM((tt, c), F32), pltpu.VMEM((1, c), F32)],
        compiler_params=_params(("parallel", "arbitrary")),
        name="rglru",
    )(proj, proj, conv_w, row(conv_b), wa.astype(BF16), row(ba), wx.astype(BF16), row(bx), row(lam))


def _inv_freq_lanes(d):
    f = ROPE_THETA ** (-jnp.arange(0, d, 2, dtype=F32) / d)
    return jnp.tile(jnp.concatenate([f, f]), LANES // d).reshape(1, LANES)


def _rope_tables(pos_ref, invf_ref, half):
    ang = pos_ref[...].astype(F32) * invf_ref[...]
    lane = lax.broadcasted_iota(I32, ang.shape, 1)
    first = (lane & (2 * half - 1)) < half
    return jnp.cos(ang), jnp.where(first, -jnp.sin(ang), jnp.sin(ang)), first


def _swap_halves(x, half, first):
    if 2 * half == LANES:
        return pltpu.roll(x, half, 1)
    return jnp.where(first, pltpu.roll(x, LANES - half, 1), pltpu.roll(x, half, 1))


def _qk_prep_kernel(x_ref, pos_ref, invf_ref, gq_ref, gk_ref, q_ref, k_ref):
    cosf, sinf, first = _rope_tables(pos_ref, invf_ref, ATT_HEAD_DIM // 2)
    for which, g_ref, o_ref in ((0, gq_ref, q_ref), (1, gk_ref, k_ref)):
        for h in range(ATT_HEADS):
            lo = (which * ATT_HEADS + h) * ATT_HEAD_DIM
            s = x_ref[:, lo:lo + ATT_HEAD_DIM]
            y = s * lax.rsqrt(jnp.mean(s * s, axis=-1, keepdims=True) + EPS) * g_ref[...]
            y = y * cosf + _swap_halves(y, ATT_HEAD_DIM // 2, first) * sinf
            o_ref[0, h] = y.astype(o_ref.dtype)


def _qk_prep(proj, pos, b, t, gq, gk):
    tm = _tile(t, 256, SUBLANES)
    nt = t // tm
    hd = ATT_HEADS * ATT_HEAD_DIM
    ospec = pl.BlockSpec((1, ATT_HEADS, tm, ATT_HEAD_DIM), lambda bi, ti: (bi, 0, ti, 0))
    oshape = jax.ShapeDtypeStruct((b, ATT_HEADS, t, ATT_HEAD_DIM), BF16)
    vspec = pl.BlockSpec((1, LANES), lambda bi, ti: (0, 0))
    return pl.pallas_call(
        _qk_prep_kernel,
        grid=(b, nt),
        in_specs=[pl.BlockSpec((tm, 2 * hd), lambda bi, ti: (bi * nt + ti, 1)),
                  pl.BlockSpec((tm, 1), lambda bi, ti: (bi * nt + ti, 0)), vspec, vspec, vspec],
        out_specs=[ospec, ospec],
        out_shape=[oshape, oshape],
        compiler_params=_params(("parallel", "parallel")),
        name="qk_prep",
    )(proj, pos, _inv_freq_lanes(ATT_HEAD_DIM), gq.reshape(1, LANES), gk.reshape(1, LANES))


def _idx_prep_kernel(x_ref, pos_ref, invf_ref, qi_ref, ki_ref):
    half = IDX_HEAD_DIM // 2
    cosf, sinf, first = _rope_tables(pos_ref, invf_ref, half)
    lane = lax.broadcasted_iota(I32, cosf.shape, 1)
    left = lane < IDX_HEAD_DIM

    def rope(x):
        return x * cosf + _swap_halves(x, half, first) * sinf

    def split(y):
        hi = y.astype(BF16).astype(F32)
        return hi, y - hi

    for j in range(IDX_HEADS // 2):
        hi, lo = split(rope(x_ref[:, j * LANES:(j + 1) * LANES]))
        even = jnp.where(left, hi, pltpu.roll(lo, IDX_HEAD_DIM, 1)).astype(BF16)
        odd = jnp.where(left, pltpu.roll(hi, IDX_HEAD_DIM, 1), lo).astype(BF16)
        for h, v in ((2 * j, even), (2 * j + 1, odd)):
            qi_ref[0, h, :, 0:LANES] = v
            qi_ref[0, h, :, LANES:2 * LANES] = v
    kcol = IDX_HEADS * IDX_HEAD_DIM
    khi, klo = split(rope(x_ref[:, kcol:kcol + LANES]))
    ki_ref[0, :, 0:LANES] = jnp.where(left, khi, pltpu.roll(khi, IDX_HEAD_DIM, 1)).astype(BF16)
    ki_ref[0, :, LANES:2 * LANES] = jnp.where(left, klo, pltpu.roll(klo, IDX_HEAD_DIM, 1)).astype(BF16)


def _idx_prep(idx, pos, b, t):
    tm = _tile(t, 256, SUBLANES)
    nt = t // tm
    w = idx.shape[1]
    return pl.pallas_call(
        _idx_prep_kernel,
        grid=(b, nt),
        in_specs=[pl.BlockSpec((tm, w), lambda bi, ti: (bi * nt + ti, 0)),
                  pl.BlockSpec((tm, 1), lambda bi, ti: (bi * nt + ti, 0)),
                  pl.BlockSpec((1, LANES), lambda bi, ti: (0, 0))],
        out_specs=[pl.BlockSpec((1, IDX_HEADS, tm, 2 * LANES), lambda bi, ti: (bi, 0, ti, 0)),
                   pl.BlockSpec((1, tm, 2 * LANES), lambda bi, ti: (bi, ti, 0))],
        out_shape=[jax.ShapeDtypeStruct((b, IDX_HEADS, t, 2 * LANES), BF16),
                   jax.ShapeDtypeStruct((b, t, 2 * LANES), BF16)],
        compiler_params=_params(("parallel", "parallel")),
        name="idx_prep",
    )(idx, pos, _inv_freq_lanes(IDX_HEAD_DIM))


def _mla_prep_kernel(m_ref, pos_ref, invf_ref, qa_ref, kva_ref, wuq_ref, wukv_ref,
                     qnn_ref, qnr_ref, knn_ref, knr_ref, q_ref, k_ref, v_ref):
    half = MLA_ROPE // 2
    cosf, sinf, first = _rope_tables(pos_ref, invf_ref, half)
    lane = lax.broadcasted_iota(I32, cosf.shape, 1)
    left = lane < MLA_ROPE

    def rope(x):
        return x * cosf + _swap_halves(x, half, first) * sinf

    def norm(x, g_ref):
        return (x * lax.rsqrt(jnp.mean(x * x, axis=-1, keepdims=True) + EPS) * g_ref[...]).astype(BF16)

    cq = norm(m_ref[:, 0:MLA_Q_LORA], qa_ref)
    ckv = norm(m_ref[:, MLA_Q_LORA:MLA_Q_LORA + MLA_KV_LORA], kva_ref)
    kr = m_ref[:, MLA_Q_LORA + MLA_KV_LORA:MLA_Q_LORA + MLA_KV_LORA + LANES]
    qf = jnp.dot(cq, wuq_ref[...], preferred_element_type=F32)
    kvf = jnp.dot(ckv, wukv_ref[...], preferred_element_type=F32)
    nope_w = MLA_HEADS * MLA_NOPE
    v_ref[...] = kvf[:, nope_w:nope_w + MLA_HEADS * MLA_V].astype(v_ref.dtype)

    for j in range(MLA_HEADS // 2):
        rs = qf[:, nope_w + j * LANES:nope_w + (j + 1) * LANES]
        sq = rs * rs
        ss_pair = (jnp.sum(jnp.where(left, sq, 0.0), axis=-1, keepdims=True),
                   jnp.sum(jnp.where(left, 0.0, sq), axis=-1, keepdims=True))
        for par in range(2):
            h = 2 * j + par
            nope = qf[:, h * MLA_NOPE:(h + 1) * MLA_NOPE]
            ms = (jnp.sum(nope * nope, axis=-1, keepdims=True) + ss_pair[par]) * (1.0 / MLA_QK)
            rsq = lax.rsqrt(ms + EPS)
            q_ref[0, h, :, 0:LANES] = (nope * rsq * qnn_ref[...]).astype(q_ref.dtype)
            rr = rope(rs * rsq * qnr_ref[...])
            if par == 1:
                rr = pltpu.roll(rr, MLA_ROPE, 1)
            q_ref[0, h, :, LANES:2 * LANES] = jnp.where(left, rr, 0.0).astype(q_ref.dtype)

    ss_kr = jnp.sum(jnp.where(left, kr * kr, 0.0), axis=-1, keepdims=True)
    base = jnp.where(left, rope(kr * knr_ref[...]), 0.0)
    for h in range(MLA_HEADS):
        nope = kvf[:, h * MLA_NOPE:(h + 1) * MLA_NOPE]
        ms = (jnp.sum(nope * nope, axis=-1, keepdims=True) + ss_kr) * (1.0 / MLA_QK)
        rsq = lax.rsqrt(ms + EPS)
        k_ref[0, h, :, 0:LANES] = (nope * rsq * knn_ref[...]).astype(k_ref.dtype)
        k_ref[0, h, :, LANES:2 * LANES] = (base * rsq).astype(k_ref.dtype)


def _mla_prep(mla, pos, b, t, qa, kva, w_uq, w_ukv, qn, kn):
    tm = _tile(t, 256, SUBLANES)
    nt = t // tm
    wq = w_uq.reshape(MLA_Q_LORA, MLA_HEADS, MLA_QK)
    wq = jnp.concatenate([wq[:, :, :MLA_NOPE].reshape(MLA_Q_LORA, -1),
                          wq[:, :, MLA_NOPE:].reshape(MLA_Q_LORA, -1)], axis=1).astype(BF16)
    wkv = w_ukv.reshape(MLA_KV_LORA, MLA_HEADS, MLA_NOPE + MLA_V)
    wkv = jnp.concatenate([wkv[:, :, :MLA_NOPE].reshape(MLA_KV_LORA, -1),
                           wkv[:, :, MLA_NOPE:].reshape(MLA_KV_LORA, -1)], axis=1).astype(BF16)
    dup = lambda g: jnp.tile(g[MLA_NOPE:], 2).reshape(1, LANES)
    full = lambda a: pl.BlockSpec(a.shape, lambda bi, ti: (0,) * a.ndim)
    consts = [_inv_freq_lanes(MLA_ROPE), qa.reshape(1, -1), kva.reshape(1, -1), wq, wkv,
              qn[:MLA_NOPE].reshape(1, LANES), dup(qn), kn[:MLA_NOPE].reshape(1, LANES), dup(kn)]
    hspec = pl.BlockSpec((1, MLA_HEADS, tm, 2 * LANES), lambda bi, ti: (bi, 0, ti, 0))
    hshape = jax.ShapeDtypeStruct((b, MLA_HEADS, t, 2 * LANES), BF16)
    vw = MLA_HEADS * MLA_V
    return pl.pallas_call(
        _mla_prep_kernel,
        grid=(b, nt),
        in_specs=[pl.BlockSpec((tm, mla.shape[1]), lambda bi, ti: (bi * nt + ti, 0)),
                  pl.BlockSpec((tm, 1), lambda bi, ti: (bi * nt + ti, 0))] + [full(a) for a in consts],
        out_specs=[hspec, hspec, pl.BlockSpec((tm, vw), lambda bi, ti: (bi * nt + ti, 0))],
        out_shape=[hshape, hshape, jax.ShapeDtypeStruct((b * t, vw), BF16)],
        compiler_params=_params(("parallel", "parallel")),
        name="mla_prep",
    )(mla, pos, *consts)


def _attn_kernel(*refs, sparse, n_sel, tq, scale, idx_scale, t_total):
    if sparse:
        (q_ref, k_ref, vt_ref, qi_ref, ki_ref, wt_ref, o_ref, m_scr, l_scr, acc_scr,
         score_scr, bias_scr, thr_scr, tie_scr) = refs
    else:
        q_ref, k_ref, vt_ref, o_ref, m_scr, l_scr, acc_scr = refs
    n_heads, dv = vt_ref.shape[1], vt_ref.shape[3]
    qb = pl.program_id(1)
    nkc = qb + 1
    kio = lax.broadcasted_iota(I32, (tq, tq), 0)
    qio = lax.broadcasted_iota(I32, (tq, tq), 1)
    neg_inf = -jnp.inf

    def chunk_off(c):
        return pl.multiple_of(c * tq, tq)

    def causal(c):
        return (c * tq + kio) <= (qb * tq + qio)

    if sparse:
        n_idx_heads = qi_ref.shape[1]

        def score_chunk(c, carry):
            koff = chunk_off(c)
            ki = ki_ref[0, pl.ds(koff, tq), :]
            acc = jnp.zeros((tq, tq), F32)
            for h in range(n_idx_heads):
                s = lax.dot_general(ki, qi_ref[0, h], NT_DIMS, preferred_element_type=F32)
                acc = acc + jnp.maximum(s, 0.0) * wt_ref[0, h:h + 1, :]
            score_scr[pl.ds(koff, tq), :] = jnp.where(causal(c), acc * idx_scale, neg_inf)
            return carry

        lax.fori_loop(0, nkc, score_chunk, 0)

        def key_to_float(key):
            val = pltpu.bitcast(jnp.where(key < 0, key ^ 0x7FFFFFFF, key), F32)
            return jnp.where(key < NEG_INF_KEY, neg_inf, val)

        def count(pred):
            def body(c, acc):
                x = score_scr[pl.ds(chunk_off(c), tq), :]
                m = jnp.where(pred(x, c), 1.0, 0.0)
                return acc + jnp.sum(m.reshape(tq // SUBLANES, SUBLANES, tq), axis=0)
            acc = lax.fori_loop(0, nkc, body, jnp.zeros((SUBLANES, tq), F32))
            return jnp.sum(acc, axis=0, keepdims=True)

        def count_ge(key):
            cand = key_to_float(key)
            return count(lambda x, c: x >= cand)

        k_sel = jnp.float32(n_sel)
        zero = jnp.zeros((1, tq), I32)
        thr_key = jnp.where(count_ge(zero) >= k_sel, zero, jnp.full((1, tq), INT_MIN, I32))

        def bit_step(i, key):
            cand = key | lax.shift_left(jnp.int32(1), jnp.int32(30) - i)
            return jnp.where(count_ge(cand) >= k_sel, cand, key)

        thr_key = lax.fori_loop(0, 31, bit_step, thr_key)
        thr = key_to_float(thr_key)
        n_ge = count(lambda x, c: x >= thr)
        thr_scr[0:1, :] = thr
        tie_scr[0:1, :] = jnp.full((1, tq), t_total, I32)

        @pl.when(jnp.max(n_ge) > k_sel)
        def _():
            need = k_sel - count(lambda x, c: x > thr)

            def below(x, c, m):
                return jnp.logical_and(x == thr, (c * tq + kio) < m)

            def idx_step(i, lo):
                cand = lo | lax.shift_left(jnp.int32(1), jnp.int32(t_total.bit_length() - 1) - i)
                return jnp.where(count(lambda x, c: below(x, c, cand)) < need, cand, lo)

            lo = lax.fori_loop(0, t_total.bit_length(), idx_step, zero)
            tie_scr[0:1, :] = lo + 1

        thr = thr_scr[0:1, :]
        tie_end = tie_scr[0:1, :]

        def bias_chunk(c, carry):
            koff = chunk_off(c)
            x = score_scr[pl.ds(koff, tq), :]
            tie_ok = jnp.logical_and(x == thr, (c * tq + kio) < tie_end)
            keep = jnp.logical_and(jnp.logical_or(x > thr, tie_ok), causal(c))
            bias_scr[pl.ds(koff, tq), :] = jnp.where(keep, 0.0, neg_inf)
            return carry

        lax.fori_loop(0, nkc, bias_chunk, 0)

    m_scr[...] = jnp.full(m_scr.shape, neg_inf, F32)
    l_scr[...] = jnp.zeros(l_scr.shape, F32)
    acc_scr[...] = jnp.zeros(acc_scr.shape, F32)

    def chunk_step(c, masked):
        koff = chunk_off(c)
        if sparse:
            bias = bias_scr[pl.ds(koff, tq), :]
        elif masked:
            allowed = causal(c)
        def qk(h):
            return lax.dot_general(k_ref[0, h, pl.ds(koff, tq), :], q_ref[0, h], NT_DIMS,
                                   preferred_element_type=F32)

        queued = [qk(h) for h in range(QK_LOOKAHEAD)]
        for h in range(n_heads):
            s = queued.pop(0) * (scale * LOG2_E)
            if h + QK_LOOKAHEAD < n_heads:
                queued.append(qk(h + QK_LOOKAHEAD))
            if sparse:
                s = s + bias
            elif masked:
                s = jnp.where(allowed, s, neg_inf)
            m_old = m_scr[h]
            m_new = jnp.maximum(m_old, jnp.max(s, axis=0, keepdims=True))
            m_ref = jnp.where(m_new == neg_inf, 0.0, m_new)
            alpha = jnp.exp2(m_old - m_ref)
            p = jnp.exp2(s - m_ref)
            l_scr[h] = alpha * l_scr[h] + jnp.sum(p, axis=0, keepdims=True)
            acc_scr[h] = alpha * acc_scr[h] + jnp.dot(vt_ref[0, h, c], p.astype(BF16),
                                                      preferred_element_type=F32)
            m_scr[h] = m_new

    def off_diagonal(c, carry):
        chunk_step(c, False)
        return carry

    if sparse:
        lax.fori_loop(0, nkc, off_diagonal, 0)
    else:
        lax.fori_loop(0, qb, off_diagonal, 0)
        chunk_step(qb, True)
    for h in range(n_heads):
        o_ref[0, h] = (acc_scr[h] / l_scr[h]).astype(o_ref.dtype)


def _attention(q, k, v, b, t, scale, sparse_inputs=None, n_sel=0):
    h, dq = q.shape[1], q.shape[3]
    dv = v.shape[1] // h
    tq = _tile(t, 256, LANES)
    nq = t // tq
    vt = v.reshape(b, nq, tq, h, dv).transpose(0, 3, 1, 4, 2)
    in_specs = [pl.BlockSpec((1, h, tq, dq), lambda bi, qi: (bi, 0, qi, 0)),
                pl.BlockSpec((1, h, t, dq), lambda bi, qi: (bi, 0, 0, 0)),
                pl.BlockSpec((1, h, nq, dv, tq), lambda bi, qi: (bi, 0, 0, 0, 0))]
    args = [q, k, vt]
    scratch = [pltpu.VMEM((h, 1, tq), F32), pltpu.VMEM((h, 1, tq), F32), pltpu.VMEM((h, dv, tq), F32)]
    sparse = sparse_inputs is not None
    if sparse:
        qi_, ki_, wt_ = sparse_inputs
        hi, dk = qi_.shape[1], qi_.shape[3]
        in_specs += [pl.BlockSpec((1, hi, tq, dk), lambda bi, qi: (bi, 0, qi, 0)),
                     pl.BlockSpec((1, t, dk), lambda bi, qi: (bi, 0, 0)),
                     pl.BlockSpec((1, hi, tq), lambda bi, qi: (bi, 0, qi))]
        args += [qi_, ki_, wt_]
        scratch += [pltpu.VMEM((t, tq), F32), pltpu.VMEM((t, tq), F32),
                    pltpu.VMEM((SUBLANES, tq), F32), pltpu.VMEM((SUBLANES, tq), I32)]
    out = pl.pallas_call(
        functools.partial(_attn_kernel, sparse=sparse, n_sel=n_sel, tq=tq, scale=scale,
                          idx_scale=IDX_HEADS ** -0.5 * IDX_HEAD_DIM ** -0.5, t_total=t),
        grid=(b, nq),
        in_specs=in_specs,
        out_specs=pl.BlockSpec((1, h, dv, tq), lambda bi, qi: (bi, 0, 0, qi)),
        out_shape=jax.ShapeDtypeStruct((b, h, dv, t), BF16),
        scratch_shapes=scratch,
        compiler_params=_params(("parallel", "arbitrary")),
        name="sparse_attention" if sparse else "dense_attention",
    )(*args)
    return out.transpose(0, 3, 1, 2).reshape(b * t, h * dv)


def _merge_kernel(hn_ref, a_ref, b_ref, c_ref, wg_ref, wb_ref, o_ref):
    hn = hn_ref[...]
    acc = None
    for n, br_ref in enumerate((a_ref, b_ref, c_ref)):
        gate = jax.nn.sigmoid(lax.dot_general(hn, wg_ref[n], NT_DIMS, preferred_element_type=F32))
        term = gate * jnp.dot(br_ref[...], wb_ref[n], preferred_element_type=F32)
        acc = term if acc is None else acc + term
    o_ref[...] = acc.astype(o_ref.dtype)


def _merge(hn, branches, w_gate, w_branch):
    m, d = hn.shape
    bw = w_branch.shape[1]
    tm = _tile(m, 512, SUBLANES)
    tn = _tile(d, 512)
    bspec = pl.BlockSpec((tm, bw), lambda i, j: (i, 0))
    return pl.pallas_call(
        _merge_kernel,
        grid=(m // tm, d // tn),
        in_specs=[pl.BlockSpec((tm, d), lambda i, j: (i, 0)), bspec, bspec, bspec,
                  pl.BlockSpec((N_BRANCH, tn, d), lambda i, j: (0, j, 0)),
                  pl.BlockSpec((N_BRANCH, bw, tn), lambda i, j: (0, 0, j))],
        out_specs=pl.BlockSpec((tm, tn), lambda i, j: (i, j)),
        out_shape=jax.ShapeDtypeStruct((m, d), BF16),
        compiler_params=_params(("parallel", "arbitrary")),
        name="gated_merge",
    )(hn, *branches, w_gate, w_branch)


def _ple_kernel(h_ref, hn_ref, p_ref, wg_ref, wp_ref, o_ref):
    gate = jax.nn.sigmoid(jnp.dot(hn_ref[...], wg_ref[...], preferred_element_type=F32))
    emb = jnp.dot(p_ref[...].astype(BF16), wp_ref[...], preferred_element_type=F32)
    o_ref[...] = h_ref[...] + emb * gate


def _ple(h, hn, p, w_gate, w_proj):
    m, d = h.shape
    pd = p.shape[1]
    tm = _tile(m, 1024, SUBLANES)
    tn = _tile(d, 512)
    return pl.pallas_call(
        _ple_kernel,
        grid=(m // tm, d // tn),
        in_specs=[pl.BlockSpec((tm, tn), lambda i, j: (i, j)), pl.BlockSpec((tm, d), lambda i, j: (i, 0)),
                  pl.BlockSpec((tm, pd), lambda i, j: (i, 0)), pl.BlockSpec((d, tn), lambda i, j: (0, j)),
                  pl.BlockSpec((pd, tn), lambda i, j: (0, j))],
        out_specs=pl.BlockSpec((tm, tn), lambda i, j: (i, j)),
        out_shape=jax.ShapeDtypeStruct((m, d), F32),
        compiler_params=_params(("parallel", "arbitrary")),
        name="ple",
    )(h, hn, p, w_gate, w_proj)


def _ffn_kernel(be_ref, nu_ref, x_ref, wg_ref, wu_ref, wd_ref, *rest, has_res):
    if has_res:
        res_ref, o_ref = rest
    else:
        (o_ref,) = rest
    i, j = pl.program_id(0), pl.program_id(1)

    @pl.when(j == 0)
    def _():
        o_ref[...] = res_ref[...] if has_res else jnp.zeros_like(o_ref)

    @pl.when(i < nu_ref[0])
    def _():
        x = x_ref[...].astype(BF16)
        g = jnp.dot(x, wg_ref[0].astype(BF16), preferred_element_type=F32)
        u = jnp.dot(x, wu_ref[0].astype(BF16), preferred_element_type=F32)
        act = (g * jax.nn.sigmoid(g) * u).astype(BF16)
        o_ref[...] += jnp.dot(act, wd_ref[0].astype(BF16), preferred_element_type=F32)


def _ffn(x, w_gate, w_up, w_down, blk_expert, n_used, tm, res=None):
    r, d = x.shape
    f = w_gate.shape[2]
    tf = _tile(f, 512)
    nf = f // tf
    assert nf >= 2

    def jj(i, j, nu):
        return jnp.where(i < nu[0], j, nf - 1)

    rows_mode = pl.Buffered(1) if w_gate.dtype == F32 else None
    in_specs = [pl.BlockSpec((tm, d), lambda i, j, be, nu: (i, 0), pipeline_mode=rows_mode),
                pl.BlockSpec((1, d, tf), lambda i, j, be, nu: (be[i], 0, jj(i, j, nu))),
                pl.BlockSpec((1, d, tf), lambda i, j, be, nu: (be[i], 0, jj(i, j, nu))),
                pl.BlockSpec((1, tf, d), lambda i, j, be, nu: (be[i], jj(i, j, nu), 0))]
    args = [x, w_gate, w_up, w_down]
    if res is not None:
        in_specs.append(pl.BlockSpec((tm, d), lambda i, j, be, nu: (i, 0)))
        args.append(res)
    return pl.pallas_call(
        functools.partial(_ffn_kernel, has_res=res is not None),
        grid_spec=pltpu.PrefetchScalarGridSpec(
            num_scalar_prefetch=2,
            grid=(r // tm, nf),
            in_specs=in_specs,
            out_specs=pl.BlockSpec((tm, d), lambda i, j, be, nu: (i, 0), pipeline_mode=rows_mode)),
        out_shape=jax.ShapeDtypeStruct((r, d), F32),
        compiler_params=_params(("arbitrary", "arbitrary")),
        name="swiglu",
    )(blk_expert, n_used, *args)


def _router_kernel(h_ref, g_ref, rhi_ref, rlo_ref, hn_ref, info_ref, cnt_ref, run_scr, *, n_experts):
    i = pl.program_id(0)
    tm = h_ref.shape[0]

    @pl.when(i == 0)
    def _():
        run_scr[...] = jnp.zeros_like(run_scr)

    x = h_ref[...]
    xn = x * lax.rsqrt(jnp.mean(x * x, axis=-1, keepdims=True) + EPS) * g_ref[...]
    hi = xn.astype(BF16)
    hn_ref[...] = xn
    lo = (xn - hi.astype(F32)).astype(BF16)
    logits = (jnp.dot(hi, rhi_ref[...], preferred_element_type=F32)
              + jnp.dot(hi, rlo_ref[...], preferred_element_type=F32)
              + jnp.dot(lo, rhi_ref[...], preferred_element_type=F32))
    lane = lax.broadcasted_iota(I32, logits.shape, 1)
    lane_f = lane.astype(F32)
    logits = jnp.where(lane < n_experts, logits, -jnp.inf)

    def top(vals):
        v = jnp.max(vals, axis=-1, keepdims=True)
        idx = jnp.min(jnp.where(vals == v, lane_f, float(LANES)), axis=-1, keepdims=True)
        return v, idx

    v1, i1 = top(logits)
    v2, i2 = top(jnp.where(lane_f == i1, -jnp.inf, logits))
    e2 = jnp.exp(v2 - v1)
    g1 = 1.0 / (1.0 + e2)
    g2 = e2 / (1.0 + e2)

    oh1 = lane_f == i1
    oh2 = lane_f == i2
    both = jnp.where(jnp.logical_or(oh1, oh2), 1.0, 0.0)
    r_io = lax.broadcasted_iota(I32, (tm, tm), 0)
    c_io = lax.broadcasted_iota(I32, (tm, tm), 1)
    strict_lower = jnp.where(c_io < r_io, 1.0, 0.0).astype(BF16)
    before = jnp.dot(strict_lower, both.astype(BF16), preferred_element_type=F32) + run_scr[0:1, :]
    rank1 = jnp.sum(jnp.where(oh1, before, 0.0), axis=-1, keepdims=True)
    rank2 = jnp.sum(jnp.where(oh2, before, 0.0), axis=-1, keepdims=True)
    run_scr[0:1, :] = run_scr[0:1, :] + jnp.sum(both, axis=0, keepdims=True)

    info = jnp.zeros(logits.shape, F32)
    for col, val in enumerate((i1, i2, g1, g2, rank1, rank2)):
        info = jnp.where(lane == col, val, info)
    info_ref[...] = info
    cnt_ref[...] = jnp.broadcast_to(run_scr[0:1, :], cnt_ref.shape)


def _router(h, g, router):
    m, d = h.shape
    e = router.shape[1]
    tm = _tile(m, 256, SUBLANES)
    rp = jnp.zeros((d, LANES), F32).at[:, :e].set(router)
    rhi = rp.astype(BF16)
    rlo = (rp - rhi.astype(F32)).astype(BF16)
    return pl.pallas_call(
        functools.partial(_router_kernel, n_experts=e),
        grid=(m // tm,),
        in_specs=[pl.BlockSpec((tm, d), lambda i: (i, 0)), pl.BlockSpec((1, d), lambda i: (0, 0)),
                  pl.BlockSpec((d, LANES), lambda i: (0, 0)), pl.BlockSpec((d, LANES), lambda i: (0, 0))],
        out_specs=[pl.BlockSpec((tm, d), lambda i: (i, 0)), pl.BlockSpec((tm, LANES), lambda i: (i, 0)),
                   pl.BlockSpec((SUBLANES, LANES), lambda i: (0, 0))],
        out_shape=[jax.ShapeDtypeStruct((m, d), F32), jax.ShapeDtypeStruct((m, LANES), F32),
                   jax.ShapeDtypeStruct((SUBLANES, LANES), F32)],
        scratch_shapes=[pltpu.VMEM((SUBLANES, LANES), F32)],
        compiler_params=_params(("arbitrary",)),
        name="router",
    )(h, g.reshape(1, d), rhi, rlo)


def _row_copy(src, dst, sem, s, d):
    return pltpu.make_async_copy(src.at[pl.ds(s, 1), :], dst.at[pl.ds(d, 1), :], sem)


def _dispatch_kernel(dest_ref, x_ref, xs_in_hbm, xs_hbm, sem, *, rows):
    del xs_in_hbm

    def issue(r, carry):
        for kk in range(TOP_K):
            _row_copy(x_ref, xs_hbm, sem, r, dest_ref[0, 0, TOP_K * r + kk]).start()
        return carry

    lax.fori_loop(0, rows, issue, 0, unroll=8)
    for kk in range(TOP_K):
        pltpu.make_async_copy(x_ref, xs_hbm.at[pl.ds(0, rows), :], sem).wait()


def _dispatch(x, dest, n_rows):
    m, d = x.shape
    rows = _tile(m, 512, SUBLANES)
    return pl.pallas_call(
        functools.partial(_dispatch_kernel, rows=rows),
        grid=(m // rows,),
        in_specs=[pl.BlockSpec((1, 1, TOP_K * rows), lambda i: (i, 0, 0), memory_space=pltpu.SMEM),
                  pl.BlockSpec((rows, d), lambda i: (i, 0)), pl.BlockSpec(memory_space=pl.ANY)],
        out_specs=pl.BlockSpec(memory_space=pl.ANY),
        out_shape=jax.ShapeDtypeStruct((n_rows, d), x.dtype),
        scratch_shapes=[pltpu.SemaphoreType.DMA(())],
        input_output_aliases={2: 0},
        compiler_params=_params(("arbitrary",)),
        name="moe_dispatch",
    )(dest.reshape(m // rows, 1, TOP_K * rows), x, jnp.zeros((n_rows, d), x.dtype))


def _combine_kernel(dest_ref, h_ref, info_ref, g_ref, ys_hbm, o_ref, on_ref, buf, sem, *, rows):
    def issue(r, carry):
        for kk in range(TOP_K):
            pltpu.make_async_copy(ys_hbm.at[pl.ds(dest_ref[0, 0, TOP_K * r + kk], 1), :],
                                  buf.at[kk, pl.ds(r, 1), :], sem).start()
        return carry

    lax.fori_loop(0, rows, issue, 0, unroll=8)
    for kk in range(TOP_K):
        pltpu.make_async_copy(ys_hbm.at[pl.ds(0, rows), :], buf.at[kk], sem).wait()
    info = info_ref[...]
    out = h_ref[...] + (info[:, 2:3] * buf[0] + info[:, 3:4] * buf[1])
    o_ref[...] = out
    on_ref[...] = (out * lax.rsqrt(jnp.mean(out * out, axis=-1, keepdims=True) + EPS)
                   * g_ref[...]).astype(on_ref.dtype)


def _combine(h, info, dest, ys, g):
    m, d = h.shape
    rows = _tile(m, 256, SUBLANES)
    return pl.pallas_call(
        functools.partial(_combine_kernel, rows=rows),
        grid=(m // rows,),
        in_specs=[pl.BlockSpec((1, 1, TOP_K * rows), lambda i: (i, 0, 0), memory_space=pltpu.SMEM),
                  pl.BlockSpec((rows, d), lambda i: (i, 0)), pl.BlockSpec((rows, LANES), lambda i: (i, 0)),
                  pl.BlockSpec((1, d), lambda i: (0, 0)), pl.BlockSpec(memory_space=pl.ANY)],
        out_specs=[pl.BlockSpec((rows, d), lambda i: (i, 0)), pl.BlockSpec((rows, d), lambda i: (i, 0))],
        out_shape=[jax.ShapeDtypeStruct((m, d), F32), jax.ShapeDtypeStruct((m, d), BF16)],
        scratch_shapes=[pltpu.VMEM((TOP_K, rows, d), F32), pltpu.SemaphoreType.DMA(())],
        compiler_params=_params(("arbitrary",)),
        name="moe_combine",
    )(dest.reshape(m // rows, 1, TOP_K * rows), h, info, g.reshape(1, d), ys)


def _moe(h, g_ffn, router, w_gate, w_up, w_down, g_next):
    m, d = h.shape
    e = router.shape[1]
    hn, info, cnt = _router(h, g_ffn, router)
    counts = cnt[0, :e].astype(I32)
    padded = (counts + MOE_ROWS - 1) // MOE_ROWS * MOE_ROWS
    pad_ends = jnp.cumsum(padded)
    pad_starts = pad_ends - padded
    n_blocks = -(-(m * TOP_K) // MOE_ROWS) + e
    ids = info[:, 0:TOP_K].astype(I32)
    dest = jnp.sum(jnp.where(ids[:, :, None] == jnp.arange(e)[None, None, :], pad_starts[None, None, :], 0),
                   axis=-1) + info[:, 4:4 + TOP_K].astype(I32)
    blk_start = jnp.arange(n_blocks, dtype=I32) * MOE_ROWS
    blk_expert = jnp.minimum(jnp.sum(blk_start[:, None] >= pad_ends[None, :], axis=1), e - 1).astype(I32)
    n_used = (pad_ends[e - 1] // MOE_ROWS).astype(I32).reshape(1)
    xs = _dispatch(hn, dest, n_blocks * MOE_ROWS)
    ys = _ffn(xs, w_gate, w_up, w_down, blk_expert, n_used, MOE_ROWS)
    return _combine(h, info, dest, ys, g_next)


N_MAIN = 2 * LRU_WIDTH + 2 * ATT_HEADS * ATT_HEAD_DIM
O_V = N_MAIN
O_IDX = O_V + ATT_HEADS * ATT_HEAD_DIM
N_IDX = IDX_HEADS * IDX_HEAD_DIM + IDX_HEAD_DIM + IDX_HEADS
O_MLA = O_IDX + N_IDX
N_MLA = MLA_Q_LORA + MLA_KV_LORA + MLA_ROPE
O_GATE = O_MLA + N_MLA


def _split_w_in(w):
    d = w.shape[0]
    wt = jnp.swapaxes(w, 0, 1)

    def rows(lo, hi):
        return jnp.pad(wt[lo:hi], ((0, -(hi - lo) % LANES), (0, 0))).astype(BF16)

    gates = rows(O_GATE, O_GATE + N_BRANCH * d).reshape(N_BRANCH, d, d)
    return rows(0, O_V), rows(O_V, O_IDX), rows(O_IDX, O_MLA), rows(O_MLA, O_GATE), gates


def kernel(x, p, positions, ln_mix, w_in, conv_w, conv_b, lru_wa, lru_ba, lru_wx, lru_bx, lru_lambda,
           att_q_norm, att_k_norm, mla_qa_norm, mla_kva_norm, mla_w_uq, mla_w_ukv, mla_q_norm,
           mla_k_norm, w_branch, w_out, ln_ffn, dense_w_gate, dense_w_up, dense_w_down, moe_router,
           moe_w_gate, moe_w_up, moe_w_down, ple_norm, ple_w_gate, ple_w_proj):
    b, t, d = x.shape
    m = b * t
    depth = w_in.shape[0]
    n_sel = min(TOPK_MAX, t // 4)
    pos = positions.reshape(m, 1).astype(I32)
    h = x.reshape(m, d)
    idx_w_off = IDX_HEADS * IDX_HEAD_DIM + IDX_HEAD_DIM
    for i in range(depth):
        w_main, w_v, w_idx, w_mla, w_gates = _split_w_in(w_in[i])
        hn = _rmsnorm(h, ln_mix[i])
        proj = _matmul(hn, w_main, F32, transposed=True, name="in_proj_main")
        v = _matmul(hn, w_v, BF16, transposed=True, name="in_proj_v")
        idx = _matmul(hn, w_idx, F32, tn_cap=w_idx.shape[0], transposed=True, name="in_proj_idx")
        mla = _matmul(hn, w_mla, F32, tn_cap=w_mla.shape[0], transposed=True, name="in_proj_mla")

        out_a = _rglru(proj, b, t, conv_w[i], conv_b[i], lru_wa[i], lru_ba[i], lru_wx[i], lru_bx[i],
                       lru_lambda[i])

        q, k = _qk_prep(proj, pos, b, t, att_q_norm[i], att_k_norm[i])
        qi, ki = _idx_prep(idx, pos, b, t)
        wt = idx[:, idx_w_off:idx_w_off + IDX_HEADS].reshape(b, t, IDX_HEADS).transpose(0, 2, 1)
        out_b = _attention(q, k, v, b, t, ATT_HEAD_DIM ** -0.5, sparse_inputs=(qi, ki, wt), n_sel=n_sel)

        mq, mk, mv = _mla_prep(mla, pos, b, t, mla_qa_norm[i], mla_kva_norm[i], mla_w_uq[i],
                               mla_w_ukv[i], mla_q_norm[i], mla_k_norm[i])
        out_c = _attention(mq, mk, mv, b, t, MLA_QK ** -0.5)

        merged = _merge(hn, (out_a, out_b, out_c), w_gates, w_branch[i].astype(BF16))
        h = _matmul(merged, w_out[i].astype(BF16), F32, res=h, name="out_proj")

        if i % 2 == 0:
            j = i // 2
            hn2 = _rmsnorm(h, ln_ffn[i])
            tm = _tile(m, 512, SUBLANES)
            h = _ffn(hn2, dense_w_gate[j:j + 1].astype(BF16), dense_w_up[j:j + 1].astype(BF16),
                     dense_w_down[j:j + 1].astype(BF16), jnp.zeros((m // tm,), I32),
                     jnp.full((1,), m // tm, I32), tm, res=h)
            hn3 = _rmsnorm(h, ple_norm[i])
        else:
            j = i // 2
            h, hn3 = _moe(h, ln_ffn[i], moe_router[j], moe_w_gate[j], moe_w_up[j], moe_w_down[j],
                          ple_norm[i])
        h = _ple(h, hn3, p[i].reshape(m, -1), ple_w_gate[i].astype(BF16), ple_w_proj[i].astype(BF16))
    return h.reshape(b, t, d)
```

```python
import functools

import jax
import jax.numpy as jnp
from jax import lax
from jax.experimental import pallas as pl
from jax.experimental.pallas import tpu as pltpu

F32, BF16, I32 = jnp.float32, jnp.bfloat16, jnp.int32

EPS = 1e-6
ROPE_THETA = 10000.0
LRU_C = 8.0
LRU_WIDTH = 1024
LRU_BLOCKS = 8
ATT_HEADS = 8
ATT_HEAD_DIM = 128
IDX_HEADS = 16
IDX_HEAD_DIM = 64
TOPK_MAX = 256
MLA_HEADS = 8
MLA_Q_LORA = 768
MLA_KV_LORA = 512
MLA_NOPE = 128
MLA_ROPE = 64
MLA_QK = MLA_NOPE + MLA_ROPE
MLA_V = 128
N_BRANCH = 3
N_EXPERTS = 8
TOP_K = 2

LANES = 128
SUBLANES = 8
VMEM_LIMIT_BYTES = 56 << 20
MOE_ROWS = 768
INT_MIN = -(2 ** 31)
NEG_INF_KEY = 0x807FFFFF - 2 ** 32

NT_DIMS = (((1,), (1,)), ((), ()))
LOG2_E = 1.4426950408889634
QK_LOOKAHEAD = 4


def _params(sem):
    return pltpu.CompilerParams(dimension_semantics=sem, vmem_limit_bytes=VMEM_LIMIT_BYTES)


def _tile(n, cap, unit=LANES):
    if n <= cap:
        return n
    best = None
    for t in range(unit, cap + 1, unit):
        if n % t == 0:
            best = t
    assert best is not None, (n, cap)
    return best


def _rmsnorm_kernel(x_ref, g_ref, o_ref):
    x = x_ref[...]
    ms = jnp.mean(x * x, axis=-1, keepdims=True)
    o_ref[...] = (x * lax.rsqrt(ms + EPS) * g_ref[...]).astype(o_ref.dtype)


def _rmsnorm(x, g):
    m, d = x.shape
    tm = _tile(m, 512, SUBLANES)
    return pl.pallas_call(
        _rmsnorm_kernel,
        grid=(m // tm,),
        in_specs=[pl.BlockSpec((tm, d), lambda i: (i, 0)), pl.BlockSpec((1, d), lambda i: (0, 0))],
        out_specs=pl.BlockSpec((tm, d), lambda i: (i, 0)),
        out_shape=jax.ShapeDtypeStruct((m, d), BF16),
        compiler_params=_params(("parallel",)),
        name="rmsnorm",
    )(x, g.reshape(1, d))


def _mm_kernel(x_ref, w_ref, o_ref):
    o_ref[...] = jnp.dot(x_ref[...], w_ref[...], preferred_element_type=F32).astype(o_ref.dtype)


def _mm_nt_kernel(x_ref, wt_ref, o_ref):
    o_ref[...] = lax.dot_general(x_ref[...], wt_ref[...], NT_DIMS,
                                 preferred_element_type=F32).astype(o_ref.dtype)


def _mm_res_kernel(x_ref, w_ref, r_ref, o_ref):
    o_ref[...] = r_ref[...] + jnp.dot(x_ref[...], w_ref[...], preferred_element_type=F32)


def _matmul(x, w, out_dtype, res=None, tm_cap=1024, tn_cap=1024, transposed=False, name="matmul"):
    m, k = x.shape
    n = w.shape[0] if transposed else w.shape[1]
    tm = _tile(m, tm_cap, SUBLANES)
    tn = _tile(n, tn_cap)
    if transposed:
        w_spec = pl.BlockSpec((tn, k), lambda i, j: (j, 0))
    else:
        w_spec = pl.BlockSpec((k, tn), lambda i, j: (0, j))
    in_specs = [pl.BlockSpec((tm, k), lambda i, j: (i, 0)), w_spec]
    args = [x, w]
    body = _mm_nt_kernel if transposed else _mm_kernel
    if res is not None:
        assert not transposed
        in_specs.append(pl.BlockSpec((tm, tn), lambda i, j: (i, j)))
        args.append(res)
        body = _mm_res_kernel
    return pl.pallas_call(
        body,
        grid=(m // tm, n // tn),
        in_specs=in_specs,
        out_specs=pl.BlockSpec((tm, tn), lambda i, j: (i, j)),
        out_shape=jax.ShapeDtypeStruct((m, n), out_dtype),
        compiler_params=_params(("parallel", "arbitrary")),
        name=name,
    )(*args)


def _store_heads_t(o_ref, rt):
    n_heads, dv = o_ref.shape[1], o_ref.shape[3]
    for h in range(n_heads):
        o_ref[0, h, 0] = rt[h * dv:(h + 1) * dv, :]


def _values_t_kernel(x_ref, wt_ref, o_ref):
    _store_heads_t(o_ref, lax.dot_general(wt_ref[...], x_ref[...], NT_DIMS,
                                          preferred_element_type=F32).astype(o_ref.dtype))


def _values_t(x, wt, b, t, n_heads):
    k = x.shape[1]
    dv = wt.shape[0] // n_heads
    tk = _tile(t, 256, LANES)
    nk = t // tk
    return pl.pallas_call(
        _values_t_kernel,
        grid=(b, nk),
        in_specs=[pl.BlockSpec((tk, k), lambda bi, ci: (bi * nk + ci, 0)),
                  pl.BlockSpec(wt.shape, lambda bi, ci: (0, 0))],
        out_specs=pl.BlockSpec((1, n_heads, 1, dv, tk), lambda bi, ci: (bi, 0, ci, 0, 0)),
        out_shape=jax.ShapeDtypeStruct((b, n_heads, nk, dv, tk), BF16),
        compiler_params=_params(("parallel", "parallel")),
        name="in_proj_v",
    )(x, wt)


EXPM1_SERIES_BOUND = 0.25
EXPM1_SERIES_TERMS = 10


def _expm1(y):
    poly = jnp.full_like(y, 1.0 / 3628800.0)
    fact = 3628800.0
    for n in range(EXPM1_SERIES_TERMS, 1, -1):
        fact /= n
        poly = poly * y + 1.0 / fact
    return jnp.where(jnp.abs(y) < EXPM1_SERIES_BOUND, poly * y, jnp.exp(y) - 1.0)


def _rglru_kernel(x_ref, g_ref, cw_ref, cb_ref, wa_ref, ba_ref, wx_ref, bx_ref, lam_ref, o_ref,
                  xs_scr, a_scr, b_scr, h_scr, *, tt):
    c = x_ref.shape[1]
    t = pl.program_id(1)

    @pl.when(t == 0)
    def _():
        xs_scr[0:SUBLANES, :] = jnp.zeros((SUBLANES, c), F32)
        h_scr[...] = jnp.zeros_like(h_scr)

    x = x_ref[...]
    xs_scr[SUBLANES:SUBLANES + tt, :] = x
    cw = cw_ref[...]
    xc = (xs_scr[SUBLANES - 3:SUBLANES - 3 + tt, :] * cw[0:1, :]
          + xs_scr[SUBLANES - 2:SUBLANES - 2 + tt, :] * cw[1:2, :]
          + xs_scr[SUBLANES - 1:SUBLANES - 1 + tt, :] * cw[2:3, :]
          + x * cw[3:4, :]) + cb_ref[...]
    xs_scr[0:SUBLANES, :] = x[tt - SUBLANES:tt, :]

    xcb = xc.astype(BF16)
    bw = c // LRU_BLOCKS
    ra = jnp.concatenate(
        [jnp.dot(xcb[:, n * bw:(n + 1) * bw], wa_ref[n], preferred_element_type=F32)
         for n in range(LRU_BLOCKS)], axis=1) + ba_ref[...]
    rx = jnp.concatenate(
        [jnp.dot(xcb[:, n * bw:(n + 1) * bw], wx_ref[n], preferred_element_type=F32)
         for n in range(LRU_BLOCKS)], axis=1) + bx_ref[...]
    r = jax.nn.sigmoid(ra)
    gi = jax.nn.sigmoid(rx)
    nlam = -lam_ref[...]
    softplus = jnp.maximum(nlam, 0.0) + jnp.log1p(jnp.exp(-jnp.abs(nlam)))
    log_a = (-LRU_C) * r * softplus
    a_scr[...] = jnp.exp(log_a)
    b_scr[...] = jnp.sqrt(-_expm1(2.0 * log_a)) * (gi * xc)

    row = lax.broadcasted_iota(I32, (SUBLANES, c), 0)

    def group(gidx, h):
        off = pl.multiple_of(gidx * SUBLANES, SUBLANES)
        a8 = a_scr[pl.ds(off, SUBLANES), :]
        b8 = b_scr[pl.ds(off, SUBLANES), :]
        for s in (1, 2, 4):
            keep = row >= s
            a_sh = jnp.where(keep, pltpu.roll(a8, s, 0), 1.0)
            b_sh = jnp.where(keep, pltpu.roll(b8, s, 0), 0.0)
            b8 = a8 * b_sh + b8
            a8 = a8 * a_sh
        h8 = a8 * h + b8
        b_scr[pl.ds(off, SUBLANES), :] = h8
        return h8[SUBLANES - 1:SUBLANES, :]

    h_scr[...] = lax.fori_loop(0, tt // SUBLANES, group, h_scr[...])
    o_ref[...] = (b_scr[...] * jax.nn.gelu(g_ref[...], approximate=True)).astype(o_ref.dtype)


def _rglru(proj, b, t, conv_w, conv_b, wa, ba, wx, bx, lam):
    c = LRU_WIDTH
    tt = _tile(t, 256, SUBLANES)
    nt = t // tt
    row = lambda v: v.reshape(1, c)
    wspec = pl.BlockSpec(wa.shape, lambda bi, ti: (0, 0, 0))
    vspec = pl.BlockSpec((1, c), lambda bi, ti: (0, 0))
    return pl.pallas_call(
        functools.partial(_rglru_kernel, tt=tt),
        grid=(b, nt),
        in_specs=[pl.BlockSpec((tt, c), lambda bi, ti: (bi * nt + ti, 0)),
                  pl.BlockSpec((tt, c), lambda bi, ti: (bi * nt + ti, 1)),
                  pl.BlockSpec(conv_w.shape, lambda bi, ti: (0, 0)), vspec,
                  wspec, vspec, wspec, vspec, vspec],
        out_specs=pl.BlockSpec((tt, c), lambda bi, ti: (bi * nt + ti, 0)),
        out_shape=jax.ShapeDtypeStruct((b * t, c), BF16),
        scratch_shapes=[pltpu.VMEM((tt + SUBLANES, c), F32), pltpu.VMEM((tt, c), F32),
                        pltpu.VMEM((tt, c), F32), pltpu.VMEM((1, c), F32)],
        compiler_params=_params(("parallel", "arbitrary")),
        name="rglru",
    )(proj, proj, conv_w, row(conv_b), wa.astype(BF16), row(ba), wx.astype(BF16), row(bx), row(lam))


def _inv_freq_lanes(d):
    f = ROPE_THETA ** (-jnp.arange(0, d, 2, dtype=F32) / d)
    return jnp.tile(jnp.concatenate([f, f]), LANES // d).reshape(1, LANES)


def _rope_tables(pos_ref, invf_ref, half):
    ang = pos_ref[...].astype(F32) * invf_ref[...]
    lane = lax.broadcasted_iota(I32, ang.shape, 1)
    first = (lane & (2 * half - 1)) < half
    return jnp.cos(ang), jnp.where(first, -jnp.sin(ang), jnp.sin(ang)), first


def _swap_halves(x, half, first):
    if 2 * half == LANES:
        return pltpu.roll(x, half, 1)
    return jnp.where(first, pltpu.roll(x, LANES - half, 1), pltpu.roll(x, half, 1))


def _qk_prep_kernel(x_ref, pos_ref, invf_ref, gq_ref, gk_ref, q_ref, k_ref):
    cosf, sinf, first = _rope_tables(pos_ref, invf_ref, ATT_HEAD_DIM // 2)
    for which, g_ref, o_ref in ((0, gq_ref, q_ref), (1, gk_ref, k_ref)):
        for h in range(ATT_HEADS):
            lo = (which * ATT_HEADS + h) * ATT_HEAD_DIM
            s = x_ref[:, lo:lo + ATT_HEAD_DIM]
            y = s * lax.rsqrt(jnp.mean(s * s, axis=-1, keepdims=True) + EPS) * g_ref[...]
            y = y * cosf + _swap_halves(y, ATT_HEAD_DIM // 2, first) * sinf
            o_ref[0, h] = y.astype(o_ref.dtype)


def _qk_prep(proj, pos, b, t, gq, gk):
    tm = _tile(t, 256, SUBLANES)
    nt = t // tm
    hd = ATT_HEADS * ATT_HEAD_DIM
    ospec = pl.BlockSpec((1, ATT_HEADS, tm, ATT_HEAD_DIM), lambda bi, ti: (bi, 0, ti, 0))
    oshape = jax.ShapeDtypeStruct((b, ATT_HEADS, t, ATT_HEAD_DIM), BF16)
    vspec = pl.BlockSpec((1, LANES), lambda bi, ti: (0, 0))
    return pl.pallas_call(
        _qk_prep_kernel,
        grid=(b, nt),
        in_specs=[pl.BlockSpec((tm, 2 * hd), lambda bi, ti: (bi * nt + ti, 1)),
                  pl.BlockSpec((tm, 1), lambda bi, ti: (bi * nt + ti, 0)), vspec, vspec, vspec],
        out_specs=[ospec, ospec],
        out_shape=[oshape, oshape],
        compiler_params=_params(("parallel", "parallel")),
        name="qk_prep",
    )(proj, pos, _inv_freq_lanes(ATT_HEAD_DIM), gq.reshape(1, LANES), gk.reshape(1, LANES))


def _idx_prep_kernel(x_ref, pos_ref, invf_ref, qi_ref, ki_ref):
    half = IDX_HEAD_DIM // 2
    cosf, sinf, first = _rope_tables(pos_ref, invf_ref, half)
    lane = lax.broadcasted_iota(I32, cosf.shape, 1)
    left = lane < IDX_HEAD_DIM

    def rope(x):
        return x * cosf + _swap_halves(x, half, first) * sinf

    def split(y):
        hi = y.astype(BF16).astype(F32)
        return hi, y - hi

    for j in range(IDX_HEADS // 2):
        hi, lo = split(rope(x_ref[:, j * LANES:(j + 1) * LANES]))
        even = jnp.where(left, hi, pltpu.roll(lo, IDX_HEAD_DIM, 1)).astype(BF16)
        odd = jnp.where(left, pltpu.roll(hi, IDX_HEAD_DIM, 1), lo).astype(BF16)
        for h, v in ((2 * j, even), (2 * j + 1, odd)):
            qi_ref[0, h, :, 0:LANES] = v
            qi_ref[0, h, :, LANES:2 * LANES] = v
    kcol = IDX_HEADS * IDX_HEAD_DIM
    khi, klo = split(rope(x_ref[:, kcol:kcol + LANES]))
    ki_ref[0, :, 0:LANES] = jnp.where(left, khi, pltpu.roll(khi, IDX_HEAD_DIM, 1)).astype(BF16)
    ki_ref[0, :, LANES:2 * LANES] = jnp.where(left, klo, pltpu.roll(klo, IDX_HEAD_DIM, 1)).astype(BF16)


def _idx_prep(idx, pos, b, t):
    tm = _tile(t, 256, SUBLANES)
    nt = t // tm
    w = idx.shape[1]
    return pl.pallas_call(
        _idx_prep_kernel,
        grid=(b, nt),
        in_specs=[pl.BlockSpec((tm, w), lambda bi, ti: (bi * nt + ti, 0)),
                  pl.BlockSpec((tm, 1), lambda bi, ti: (bi * nt + ti, 0)),
                  pl.BlockSpec((1, LANES), lambda bi, ti: (0, 0))],
        out_specs=[pl.BlockSpec((1, IDX_HEADS, tm, 2 * LANES), lambda bi, ti: (bi, 0, ti, 0)),
                   pl.BlockSpec((1, tm, 2 * LANES), lambda bi, ti: (bi, ti, 0))],
        out_shape=[jax.ShapeDtypeStruct((b, IDX_HEADS, t, 2 * LANES), BF16),
                   jax.ShapeDtypeStruct((b, t, 2 * LANES), BF16)],
        compiler_params=_params(("parallel", "parallel")),
        name="idx_prep",
    )(idx, pos, _inv_freq_lanes(IDX_HEAD_DIM))


def _mla_prep_kernel(m_ref, pos_ref, invf_ref, qa_ref, kva_ref, wuq_ref, wuk_ref, wuvt_ref,
                     qnn_ref, qnr_ref, knn_ref, knr_ref, q_ref, k_ref, vt_ref):
    half = MLA_ROPE // 2
    cosf, sinf, first = _rope_tables(pos_ref, invf_ref, half)
    lane = lax.broadcasted_iota(I32, cosf.shape, 1)
    left = lane < MLA_ROPE

    def rope(x):
        return x * cosf + _swap_halves(x, half, first) * sinf

    def norm(x, g_ref):
        return (x * lax.rsqrt(jnp.mean(x * x, axis=-1, keepdims=True) + EPS) * g_ref[...]).astype(BF16)

    cq = norm(m_ref[:, 0:MLA_Q_LORA], qa_ref)
    ckv = norm(m_ref[:, MLA_Q_LORA:MLA_Q_LORA + MLA_KV_LORA], kva_ref)
    kr = m_ref[:, MLA_Q_LORA + MLA_KV_LORA:MLA_Q_LORA + MLA_KV_LORA + LANES]
    qf = jnp.dot(cq, wuq_ref[...], preferred_element_type=F32)
    kvf = jnp.dot(ckv, wuk_ref[...], preferred_element_type=F32)
    nope_w = MLA_HEADS * MLA_NOPE
    _store_heads_t(vt_ref, lax.dot_general(wuvt_ref[...], ckv, NT_DIMS,
                                           preferred_element_type=F32).astype(vt_ref.dtype))

    for j in range(MLA_HEADS // 2):
        rs = qf[:, nope_w + j * LANES:nope_w + (j + 1) * LANES]
        sq = rs * rs
        ss_pair = (jnp.sum(jnp.where(left, sq, 0.0), axis=-1, keepdims=True),
                   jnp.sum(jnp.where(left, 0.0, sq), axis=-1, keepdims=True))
        for par in range(2):
            h = 2 * j + par
            nope = qf[:, h * MLA_NOPE:(h + 1) * MLA_NOPE]
            ms = (jnp.sum(nope * nope, axis=-1, keepdims=True) + ss_pair[par]) * (1.0 / MLA_QK)
            rsq = lax.rsqrt(ms + EPS)
            q_ref[0, h, :, 0:LANES] = (nope * rsq * qnn_ref[...]).astype(q_ref.dtype)
            rr = rope(rs * rsq * qnr_ref[...])
            if par == 1:
                rr = pltpu.roll(rr, MLA_ROPE, 1)
            q_ref[0, h, :, LANES:2 * LANES] = jnp.where(left, rr, 0.0).astype(q_ref.dtype)

    ss_kr = jnp.sum(jnp.where(left, kr * kr, 0.0), axis=-1, keepdims=True)
    base = jnp.where(left, rope(kr * knr_ref[...]), 0.0)
    for h in range(MLA_HEADS):
        nope = kvf[:, h * MLA_NOPE:(h + 1) * MLA_NOPE]
        ms = (jnp.sum(nope * nope, axis=-1, keepdims=True) + ss_kr) * (1.0 / MLA_QK)
        rsq = lax.rsqrt(ms + EPS)
        k_ref[0, h, :, 0:LANES] = (nope * rsq * knn_ref[...]).astype(k_ref.dtype)
        k_ref[0, h, :, LANES:2 * LANES] = (base * rsq).astype(k_ref.dtype)


def _mla_prep(mla, pos, b, t, qa, kva, w_uq, w_ukv, qn, kn):
    tm = _tile(t, 256, SUBLANES)
    nt = t // tm
    wq = w_uq.reshape(MLA_Q_LORA, MLA_HEADS, MLA_QK)
    wq = jnp.concatenate([wq[:, :, :MLA_NOPE].reshape(MLA_Q_LORA, -1),
                          wq[:, :, MLA_NOPE:].reshape(MLA_Q_LORA, -1)], axis=1).astype(BF16)
    wkv = w_ukv.reshape(MLA_KV_LORA, MLA_HEADS, MLA_NOPE + MLA_V)
    wk = wkv[:, :, :MLA_NOPE].reshape(MLA_KV_LORA, -1).astype(BF16)
    wvt = wkv[:, :, MLA_NOPE:].reshape(MLA_KV_LORA, -1).T.astype(BF16)
    dup = lambda g: jnp.tile(g[MLA_NOPE:], 2).reshape(1, LANES)
    full = lambda a: pl.BlockSpec(a.shape, lambda bi, ti: (0,) * a.ndim)
    consts = [_inv_freq_lanes(MLA_ROPE), qa.reshape(1, -1), kva.reshape(1, -1), wq, wk, wvt,
              qn[:MLA_NOPE].reshape(1, LANES), dup(qn), kn[:MLA_NOPE].reshape(1, LANES), dup(kn)]
    hspec = pl.BlockSpec((1, MLA_HEADS, tm, 2 * LANES), lambda bi, ti: (bi, 0, ti, 0))
    hshape = jax.ShapeDtypeStruct((b, MLA_HEADS, t, 2 * LANES), BF16)
    return pl.pallas_call(
        _mla_prep_kernel,
        grid=(b, nt),
        in_specs=[pl.BlockSpec((tm, mla.shape[1]), lambda bi, ti: (bi * nt + ti, 0)),
                  pl.BlockSpec((tm, 1), lambda bi, ti: (bi * nt + ti, 0))] + [full(a) for a in consts],
        out_specs=[hspec, hspec,
                   pl.BlockSpec((1, MLA_HEADS, 1, MLA_V, tm), lambda bi, ti: (bi, 0, ti, 0, 0))],
        out_shape=[hshape, hshape, jax.ShapeDtypeStruct((b, MLA_HEADS, nt, MLA_V, tm), BF16)],
        compiler_params=_params(("parallel", "parallel")),
        name="mla_prep",
    )(mla, pos, *consts)


def _attn_kernel(*refs, sparse, n_sel, tq, scale, idx_scale, t_total):
    if sparse:
        (q_ref, k_ref, vt_ref, qi_ref, ki_ref, wt_ref, o_ref, m_scr, l_scr, acc_scr,
         score_scr, bias_scr, thr_scr, tie_scr) = refs
    else:
        q_ref, k_ref, vt_ref, o_ref, m_scr, l_scr, acc_scr = refs
    n_heads, dv = vt_ref.shape[1], vt_ref.shape[3]
    qb = pl.program_id(1)
    nkc = qb + 1
    kio = lax.broadcasted_iota(I32, (tq, tq), 0)
    qio = lax.broadcasted_iota(I32, (tq, tq), 1)
    neg_inf = -jnp.inf

    def chunk_off(c):
        return pl.multiple_of(c * tq, tq)

    def causal(c):
        return (c * tq + kio) <= (qb * tq + qio)

    if sparse:
        n_idx_heads = qi_ref.shape[1]

        def score_chunk(c, carry):
            koff = chunk_off(c)
            ki = ki_ref[0, pl.ds(koff, tq), :]
            acc = jnp.zeros((tq, tq), F32)
            for h in range(n_idx_heads):
                s = lax.dot_general(ki, qi_ref[0, h], NT_DIMS, preferred_element_type=F32)
                acc = acc + jnp.maximum(s, 0.0) * wt_ref[0, h:h + 1, :]
            score_scr[pl.ds(koff, tq), :] = jnp.where(causal(c), acc * idx_scale, neg_inf)
            return carry

        lax.fori_loop(0, nkc, score_chunk, 0)

        def key_to_float(key):
            val = pltpu.bitcast(jnp.where(key < 0, key ^ 0x7FFFFFFF, key), F32)
            return jnp.where(key < NEG_INF_KEY, neg_inf, val)

        def count(pred):
            def body(c, acc):
                x = score_scr[pl.ds(chunk_off(c), tq), :]
                m = jnp.where(pred(x, c), 1.0, 0.0)
                return acc + jnp.sum(m.reshape(tq // SUBLANES, SUBLANES, tq), axis=0)
            acc = lax.fori_loop(0, nkc, body, jnp.zeros((SUBLANES, tq), F32))
            return jnp.sum(acc, axis=0, keepdims=True)

        def count_ge(key):
            cand = key_to_float(key)
            return count(lambda x, c: x >= cand)

        k_sel = jnp.float32(n_sel)
        zero = jnp.zeros((1, tq), I32)
        thr_key = jnp.where(count_ge(zero) >= k_sel, zero, jnp.full((1, tq), INT_MIN, I32))

        def bit_step(i, key):
            cand = key | lax.shift_left(jnp.int32(1), jnp.int32(30) - i)
            return jnp.where(count_ge(cand) >= k_sel, cand, key)

        thr_key = lax.fori_loop(0, 31, bit_step, thr_key)
        thr = key_to_float(thr_key)
        n_ge = count(lambda x, c: x >= thr)
        thr_scr[0:1, :] = thr
        tie_scr[0:1, :] = jnp.full((1, tq), t_total, I32)

        @pl.when(jnp.max(n_ge) > k_sel)
        def _():
            need = k_sel - count(lambda x, c: x > thr)

            def below(x, c, m):
                return jnp.logical_and(x == thr, (c * tq + kio) < m)

            def idx_step(i, lo):
                cand = lo | lax.shift_left(jnp.int32(1), jnp.int32(t_total.bit_length() - 1) - i)
                return jnp.where(count(lambda x, c: below(x, c, cand)) < need, cand, lo)

            lo = lax.fori_loop(0, t_total.bit_length(), idx_step, zero)
            tie_scr[0:1, :] = lo + 1

        thr = thr_scr[0:1, :]
        tie_end = tie_scr[0:1, :]

        def bias_chunk(c, carry):
            koff = chunk_off(c)
            x = score_scr[pl.ds(koff, tq), :]
            tie_ok = jnp.logical_and(x == thr, (c * tq + kio) < tie_end)
            keep = jnp.logical_and(jnp.logical_or(x > thr, tie_ok), causal(c))
            bias_scr[pl.ds(koff, tq), :] = jnp.where(keep, 0.0, neg_inf)
            return carry

        lax.fori_loop(0, nkc, bias_chunk, 0)

    m_scr[...] = jnp.full(m_scr.shape, neg_inf, F32)
    l_scr[...] = jnp.zeros(l_scr.shape, F32)
    acc_scr[...] = jnp.zeros(acc_scr.shape, F32)

    def chunk_step(c, masked):
        koff = chunk_off(c)
        if sparse:
            bias = bias_scr[pl.ds(koff, tq), :]
        elif masked:
            allowed = causal(c)
        def qk(h):
            return lax.dot_general(k_ref[0, h, pl.ds(koff, tq), :], q_ref[0, h], NT_DIMS,
                                   preferred_element_type=F32)

        queued = [qk(h) for h in range(QK_LOOKAHEAD)]
        for h in range(n_heads):
            s = queued.pop(0) * (scale * LOG2_E)
            if h + QK_LOOKAHEAD < n_heads:
                queued.append(qk(h + QK_LOOKAHEAD))
            if sparse:
                s = s + bias
            elif masked:
                s = jnp.where(allowed, s, neg_inf)
            m_old = m_scr[h]
            m_new = jnp.maximum(m_old, jnp.max(s, axis=0, keepdims=True))
            m_ref = jnp.where(m_new == neg_inf, 0.0, m_new)
            alpha = jnp.exp2(m_old - m_ref)
            p = jnp.exp2(s - m_ref)
            l_scr[h] = alpha * l_scr[h] + jnp.sum(p, axis=0, keepdims=True)
            acc_scr[h] = alpha * acc_scr[h] + jnp.dot(vt_ref[0, h, c], p.astype(BF16),
                                                      preferred_element_type=F32)
            m_scr[h] = m_new

    def off_diagonal(c, carry):
        chunk_step(c, False)
        return carry

    if sparse:
        lax.fori_loop(0, nkc, off_diagonal, 0)
    else:
        lax.fori_loop(0, qb, off_diagonal, 0)
        chunk_step(qb, True)
    for h in range(n_heads):
        o_ref[:, h * dv:(h + 1) * dv] = (acc_scr[h] / l_scr[h]).T.astype(o_ref.dtype)


def _attention(q, k, vt, b, t, scale, sparse_inputs=None, n_sel=0):
    h, dq = q.shape[1], q.shape[3]
    nq, dv, tq = vt.shape[2:]
    assert nq * tq == t
    in_specs = [pl.BlockSpec((1, h, tq, dq), lambda bi, qi: (bi, 0, qi, 0)),
                pl.BlockSpec((1, h, t, dq), lambda bi, qi: (bi, 0, 0, 0)),
                pl.BlockSpec((1, h, nq, dv, tq), lambda bi, qi: (bi, 0, 0, 0, 0))]
    args = [q, k, vt]
    scratch = [pltpu.VMEM((h, 1, tq), F32), pltpu.VMEM((h, 1, tq), F32), pltpu.VMEM((h, dv, tq), F32)]
    sparse = sparse_inputs is not None
    if sparse:
        qi_, ki_, wt_ = sparse_inputs
        hi, dk = qi_.shape[1], qi_.shape[3]
        in_specs += [pl.BlockSpec((1, hi, tq, dk), lambda bi, qi: (bi, 0, qi, 0)),
                     pl.BlockSpec((1, t, dk), lambda bi, qi: (bi, 0, 0)),
                     pl.BlockSpec((1, hi, tq), lambda bi, qi: (bi, 0, qi))]
        args += [qi_, ki_, wt_]
        scratch += [pltpu.VMEM((t, tq), F32), pltpu.VMEM((t, tq), F32),
                    pltpu.VMEM((SUBLANES, tq), F32), pltpu.VMEM((SUBLANES, tq), I32)]
    return pl.pallas_call(
        functools.partial(_attn_kernel, sparse=sparse, n_sel=n_sel, tq=tq, scale=scale,
                          idx_scale=IDX_HEADS ** -0.5 * IDX_HEAD_DIM ** -0.5, t_total=t),
        grid=(b, nq),
        in_specs=in_specs,
        out_specs=pl.BlockSpec((tq, h * dv), lambda bi, qi: (bi * nq + qi, 0)),
        out_shape=jax.ShapeDtypeStruct((b * t, h * dv), BF16),
        scratch_shapes=scratch,
        compiler_params=_params(("parallel", "arbitrary")),
        name="sparse_attention" if sparse else "dense_attention",
    )(*args)


def _merge_kernel(hn_ref, a_ref, b_ref, c_ref, wg_ref, wb_ref, o_ref):
    hn = hn_ref[...]
    acc = None
    for n, br_ref in enumerate((a_ref, b_ref, c_ref)):
        gate = jax.nn.sigmoid(lax.dot_general(hn, wg_ref[n], NT_DIMS, preferred_element_type=F32))
        term = gate * jnp.dot(br_ref[...], wb_ref[n], preferred_element_type=F32)
        acc = term if acc is None else acc + term
    o_ref[...] = acc.astype(o_ref.dtype)


def _merge(hn, branches, w_gate, w_branch):
    m, d = hn.shape
    bw = w_branch.shape[1]
    tm = _tile(m, 512, SUBLANES)
    tn = _tile(d, 512)
    bspec = pl.BlockSpec((tm, bw), lambda i, j: (i, 0))
    return pl.pallas_call(
        _merge_kernel,
        grid=(m // tm, d // tn),
        in_specs=[pl.BlockSpec((tm, d), lambda i, j: (i, 0)), bspec, bspec, bspec,
                  pl.BlockSpec((N_BRANCH, tn, d), lambda i, j: (0, j, 0)),
                  pl.BlockSpec((N_BRANCH, bw, tn), lambda i, j: (0, 0, j))],
        out_specs=pl.BlockSpec((tm, tn), lambda i, j: (i, j)),
        out_shape=jax.ShapeDtypeStruct((m, d), BF16),
        compiler_params=_params(("parallel", "arbitrary")),
        name="gated_merge",
    )(hn, *branches, w_gate, w_branch)


def _ple_kernel(h_ref, hn_ref, p_ref, wg_ref, wp_ref, o_ref):
    gate = jax.nn.sigmoid(jnp.dot(hn_ref[...], wg_ref[...], preferred_element_type=F32))
    emb = jnp.dot(p_ref[...].astype(BF16), wp_ref[...], preferred_element_type=F32)
    o_ref[...] = h_ref[...] + emb * gate


def _ple(h, hn, p, w_gate, w_proj):
    m, d = h.shape
    pd = p.shape[1]
    tm = _tile(m, 1024, SUBLANES)
    tn = _tile(d, 512)
    return pl.pallas_call(
        _ple_kernel,
        grid=(m // tm, d // tn),
        in_specs=[pl.BlockSpec((tm, tn), lambda i, j: (i, j)), pl.BlockSpec((tm, d), lambda i, j: (i, 0)),
                  pl.BlockSpec((tm, pd), lambda i, j: (i, 0)), pl.BlockSpec((d, tn), lambda i, j: (0, j)),
                  pl.BlockSpec((pd, tn), lambda i, j: (0, j))],
        out_specs=pl.BlockSpec((tm, tn), lambda i, j: (i, j)),
        out_shape=jax.ShapeDtypeStruct((m, d), F32),
        compiler_params=_params(("parallel", "arbitrary")),
        name="ple",
    )(h, hn, p, w_gate, w_proj)


def _ffn_kernel(be_ref, nu_ref, x_ref, wg_ref, wu_ref, wd_ref, *rest, has_res):
    if has_res:
        res_ref, o_ref = rest
    else:
        (o_ref,) = rest
    i, j = pl.program_id(0), pl.program_id(1)

    @pl.when(j == 0)
    def _():
        o_ref[...] = res_ref[...] if has_res else jnp.zeros_like(o_ref)

    @pl.when(i < nu_ref[0])
    def _():
        x = x_ref[...].astype(BF16)
        g = jnp.dot(x, wg_ref[0].astype(BF16), preferred_element_type=F32)
        u = jnp.dot(x, wu_ref[0].astype(BF16), preferred_element_type=F32)
        act = (g * jax.nn.sigmoid(g) * u).astype(BF16)
        o_ref[...] += jnp.dot(act, wd_ref[0].astype(BF16), preferred_element_type=F32)


def _ffn(x, w_gate, w_up, w_down, blk_expert, n_used, tm, res=None):
    r, d = x.shape
    f = w_gate.shape[2]
    tf = _tile(f, 512)
    nf = f // tf
    assert nf >= 2

    def jj(i, j, nu):
        return jnp.where(i < nu[0], j, nf - 1)

    rows_mode = pl.Buffered(1) if w_gate.dtype == F32 else None
    in_specs = [pl.BlockSpec((tm, d), lambda i, j, be, nu: (i, 0), pipeline_mode=rows_mode),
                pl.BlockSpec((1, d, tf), lambda i, j, be, nu: (be[i], 0, jj(i, j, nu))),
                pl.BlockSpec((1, d, tf), lambda i, j, be, nu: (be[i], 0, jj(i, j, nu))),
                pl.BlockSpec((1, tf, d), lambda i, j, be, nu: (be[i], jj(i, j, nu), 0))]
    args = [x, w_gate, w_up, w_down]
    if res is not None:
        in_specs.append(pl.BlockSpec((tm, d), lambda i, j, be, nu: (i, 0)))
        args.append(res)
    return pl.pallas_call(
        functools.partial(_ffn_kernel, has_res=res is not None),
        grid_spec=pltpu.PrefetchScalarGridSpec(
            num_scalar_prefetch=2,
            grid=(r // tm, nf),
            in_specs=in_specs,
            out_specs=pl.BlockSpec((tm, d), lambda i, j, be, nu: (i, 0), pipeline_mode=rows_mode)),
        out_shape=jax.ShapeDtypeStruct((r, d), F32),
        compiler_params=_params(("arbitrary", "arbitrary")),
        name="swiglu",
    )(blk_expert, n_used, *args)


def _router_kernel(h_ref, g_ref, rhi_ref, rlo_ref, hn_ref, info_ref, cnt_ref, run_scr, *, n_experts):
    i = pl.program_id(0)
    tm = h_ref.shape[0]

    @pl.when(i == 0)
    def _():
        run_scr[...] = jnp.zeros_like(run_scr)

    x = h_ref[...]
    xn = x * lax.rsqrt(jnp.mean(x * x, axis=-1, keepdims=True) + EPS) * g_ref[...]
    hi = xn.astype(BF16)
    hn_ref[...] = xn
    lo = (xn - hi.astype(F32)).astype(BF16)
    logits = (jnp.dot(hi, rhi_ref[...], preferred_element_type=F32)
              + jnp.dot(hi, rlo_ref[...], preferred_element_type=F32)
              + jnp.dot(lo, rhi_ref[...], preferred_element_type=F32))
    lane = lax.broadcasted_iota(I32, logits.shape, 1)
    lane_f = lane.astype(F32)
    logits = jnp.where(lane < n_experts, logits, -jnp.inf)

    def top(vals):
        v = jnp.max(vals, axis=-1, keepdims=True)
        idx = jnp.min(jnp.where(vals == v, lane_f, float(LANES)), axis=-1, keepdims=True)
        return v, idx

    v1, i1 = top(logits)
    v2, i2 = top(jnp.where(lane_f == i1, -jnp.inf, logits))
    e2 = jnp.exp(v2 - v1)
    g1 = 1.0 / (1.0 + e2)
    g2 = e2 / (1.0 + e2)

    oh1 = lane_f == i1
    oh2 = lane_f == i2
    both = jnp.where(jnp.logical_or(oh1, oh2), 1.0, 0.0)
    r_io = lax.broadcasted_iota(I32, (tm, tm), 0)
    c_io = lax.broadcasted_iota(I32, (tm, tm), 1)
    strict_lower = jnp.where(c_io < r_io, 1.0, 0.0).astype(BF16)
    before = jnp.dot(strict_lower, both.astype(BF16), preferred_element_type=F32) + run_scr[0:1, :]
    rank1 = jnp.sum(jnp.where(oh1, before, 0.0), axis=-1, keepdims=True)
    rank2 = jnp.sum(jnp.where(oh2, before, 0.0), axis=-1, keepdims=True)
    run_scr[0:1, :] = run_scr[0:1, :] + jnp.sum(both, axis=0, keepdims=True)

    info = jnp.zeros(logits.shape, F32)
    for col, val in enumerate((i1, i2, g1, g2, rank1, rank2)):
        info = jnp.where(lane == col, val, info)
    info_ref[...] = info
    cnt_ref[...] = jnp.broadcast_to(run_scr[0:1, :], cnt_ref.shape)


def _router(h, g, router):
    m, d = h.shape
    e = router.shape[1]
    tm = _tile(m, 256, SUBLANES)
    rp = jnp.zeros((d, LANES), F32).at[:, :e].set(router)
    rhi = rp.astype(BF16)
    rlo = (rp - rhi.astype(F32)).astype(BF16)
    return pl.pallas_call(
        functools.partial(_router_kernel, n_experts=e),
        grid=(m // tm,),
        in_specs=[pl.BlockSpec((tm, d), lambda i: (i, 0)), pl.BlockSpec((1, d), lambda i: (0, 0)),
                  pl.BlockSpec((d, LANES), lambda i: (0, 0)), pl.BlockSpec((d, LANES), lambda i: (0, 0))],
        out_specs=[pl.BlockSpec((tm, d), lambda i: (i, 0)), pl.BlockSpec((tm, LANES), lambda i: (i, 0)),
                   pl.BlockSpec((SUBLANES, LANES), lambda i: (0, 0))],
        out_shape=[jax.ShapeDtypeStruct((m, d), F32), jax.ShapeDtypeStruct((m, LANES), F32),
                   jax.ShapeDtypeStruct((SUBLANES, LANES), F32)],
        scratch_shapes=[pltpu.VMEM((SUBLANES, LANES), F32)],
        compiler_params=_params(("arbitrary",)),
        name="router",
    )(h, g.reshape(1, d), rhi, rlo)


def _row_copy(src, dst, sem, s, d):
    return pltpu.make_async_copy(src.at[pl.ds(s, 1), :], dst.at[pl.ds(d, 1), :], sem)


def _dispatch_kernel(dest_ref, x_ref, xs_in_hbm, xs_hbm, sem, *, rows):
    del xs_in_hbm

    def issue(r, carry):
        for kk in range(TOP_K):
            _row_copy(x_ref, xs_hbm, sem, r, dest_ref[0, 0, TOP_K * r + kk]).start()
        return carry

    lax.fori_loop(0, rows, issue, 0, unroll=8)
    for kk in range(TOP_K):
        pltpu.make_async_copy(x_ref, xs_hbm.at[pl.ds(0, rows), :], sem).wait()


def _dispatch(x, dest, n_rows):
    m, d = x.shape
    rows = _tile(m, 512, SUBLANES)
    return pl.pallas_call(
        functools.partial(_dispatch_kernel, rows=rows),
        grid=(m // rows,),
        in_specs=[pl.BlockSpec((1, 1, TOP_K * rows), lambda i: (i, 0, 0), memory_space=pltpu.SMEM),
                  pl.BlockSpec((rows, d), lambda i: (i, 0)), pl.BlockSpec(memory_space=pl.ANY)],
        out_specs=pl.BlockSpec(memory_space=pl.ANY),
        out_shape=jax.ShapeDtypeStruct((n_rows, d), x.dtype),
        scratch_shapes=[pltpu.SemaphoreType.DMA(())],
        input_output_aliases={2: 0},
        compiler_params=_params(("arbitrary",)),
        name="moe_dispatch",
    )(dest.reshape(m // rows, 1, TOP_K * rows), x, jnp.zeros((n_rows, d), x.dtype))


def _combine_kernel(dest_ref, h_ref, info_ref, g_ref, ys_hbm, o_ref, on_ref, buf, sem, *, rows):
    def issue(r, carry):
        for kk in range(TOP_K):
            pltpu.make_async_copy(ys_hbm.at[pl.ds(dest_ref[0, 0, TOP_K * r + kk], 1), :],
                                  buf.at[kk, pl.ds(r, 1), :], sem).start()
        return carry

    lax.fori_loop(0, rows, issue, 0, unroll=8)
    for kk in range(TOP_K):
        pltpu.make_async_copy(ys_hbm.at[pl.ds(0, rows), :], buf.at[kk], sem).wait()
    info = info_ref[...]
    out = h_ref[...] + (info[:, 2:3] * buf[0] + info[:, 3:4] * buf[1])
    o_ref[...] = out
    on_ref[...] = (out * lax.rsqrt(jnp.mean(out * out, axis=-1, keepdims=True) + EPS)
                   * g_ref[...]).astype(on_ref.dtype)


def _combine(h, info, dest, ys, g):
    m, d = h.shape
    rows = _tile(m, 256, SUBLANES)
    return pl.pallas_call(
        functools.partial(_combine_kernel, rows=rows),
        grid=(m // rows,),
        in_specs=[pl.BlockSpec((1, 1, TOP_K * rows), lambda i: (i, 0, 0), memory_space=pltpu.SMEM),
                  pl.BlockSpec((rows, d), lambda i: (i, 0)), pl.BlockSpec((rows, LANES), lambda i: (i, 0)),
                  pl.BlockSpec((1, d), lambda i: (0, 0)), pl.BlockSpec(memory_space=pl.ANY)],
        out_specs=[pl.BlockSpec((rows, d), lambda i: (i, 0)), pl.BlockSpec((rows, d), lambda i: (i, 0))],
        out_shape=[jax.ShapeDtypeStruct((m, d), F32), jax.ShapeDtypeStruct((m, d), BF16)],
        scratch_shapes=[pltpu.VMEM((TOP_K, rows, d), F32), pltpu.SemaphoreType.DMA(())],
        compiler_params=_params(("arbitrary",)),
        name="moe_combine",
    )(dest.reshape(m // rows, 1, TOP_K * rows), h, info, g.reshape(1, d), ys)


def _moe(h, g_ffn, router, w_gate, w_up, w_down, g_next):
    m, d = h.shape
    e = router.shape[1]
    hn, info, cnt = _router(h, g_ffn, router)
    counts = cnt[0, :e].astype(I32)
    padded = (counts + MOE_ROWS - 1) // MOE_ROWS * MOE_ROWS
    pad_ends = jnp.cumsum(padded)
    pad_starts = pad_ends - padded
    n_blocks = -(-(m * TOP_K) // MOE_ROWS) + e
    ids = info[:, 0:TOP_K].astype(I32)
    dest = jnp.sum(jnp.where(ids[:, :, None] == jnp.arange(e)[None, None, :], pad_starts[None, None, :], 0),
                   axis=-1) + info[:, 4:4 + TOP_K].astype(I32)
    blk_start = jnp.arange(n_blocks, dtype=I32) * MOE_ROWS
    blk_expert = jnp.minimum(jnp.sum(blk_start[:, None] >= pad_ends[None, :], axis=1), e - 1).astype(I32)
    n_used = (pad_ends[e - 1] // MOE_ROWS).astype(I32).reshape(1)
    xs = _dispatch(hn, dest, n_blocks * MOE_ROWS)
    ys = _ffn(xs, w_gate, w_up, w_down, blk_expert, n_used, MOE_ROWS)
    return _combine(h, info, dest, ys, g_next)


N_MAIN = 2 * LRU_WIDTH + 2 * ATT_HEADS * ATT_HEAD_DIM
O_V = N_MAIN
O_IDX = O_V + ATT_HEADS * ATT_HEAD_DIM
N_IDX = IDX_HEADS * IDX_HEAD_DIM + IDX_HEAD_DIM + IDX_HEADS
O_MLA = O_IDX + N_IDX
N_MLA = MLA_Q_LORA + MLA_KV_LORA + MLA_ROPE
O_GATE = O_MLA + N_MLA


def _split_w_in(w):
    d = w.shape[0]
    wt = jnp.swapaxes(w, 0, 1)

    def rows(lo, hi):
        return jnp.pad(wt[lo:hi], ((0, -(hi - lo) % LANES), (0, 0))).astype(BF16)

    gates = rows(O_GATE, O_GATE + N_BRANCH * d).reshape(N_BRANCH, d, d)
    return rows(0, O_V), rows(O_V, O_IDX), rows(O_IDX, O_MLA), rows(O_MLA, O_GATE), gates


def kernel(x, p, positions, ln_mix, w_in, conv_w, conv_b, lru_wa, lru_ba, lru_wx, lru_bx, lru_lambda,
           att_q_norm, att_k_norm, mla_qa_norm, mla_kva_norm, mla_w_uq, mla_w_ukv, mla_q_norm,
           mla_k_norm, w_branch, w_out, ln_ffn, dense_w_gate, dense_w_up, dense_w_down, moe_router,
           moe_w_gate, moe_w_up, moe_w_down, ple_norm, ple_w_gate, ple_w_proj):
    b, t, d = x.shape
    m = b * t
    depth = w_in.shape[0]
    n_sel = min(TOPK_MAX, t // 4)
    pos = positions.reshape(m, 1).astype(I32)
    h = x.reshape(m, d)
    idx_w_off = IDX_HEADS * IDX_HEAD_DIM + IDX_HEAD_DIM
    for i in range(depth):
        w_main, w_v, w_idx, w_mla, w_gates = _split_w_in(w_in[i])
        hn = _rmsnorm(h, ln_mix[i])
        proj = _matmul(hn, w_main, F32, transposed=True, name="in_proj_main")
        vt = _values_t(hn, w_v, b, t, ATT_HEADS)
        idx = _matmul(hn, w_idx, F32, tn_cap=w_idx.shape[0], transposed=True, name="in_proj_idx")
        mla = _matmul(hn, w_mla, F32, tn_cap=w_mla.shape[0], transposed=True, name="in_proj_mla")

        out_a = _rglru(proj, b, t, conv_w[i], conv_b[i], lru_wa[i], lru_ba[i], lru_wx[i], lru_bx[i],
                       lru_lambda[i])

        q, k = _qk_prep(proj, pos, b, t, att_q_norm[i], att_k_norm[i])
        qi, ki = _idx_prep(idx, pos, b, t)
        wt = idx[:, idx_w_off:idx_w_off + IDX_HEADS].reshape(b, t, IDX_HEADS).transpose(0, 2, 1)
        out_b = _attention(q, k, vt, b, t, ATT_HEAD_DIM ** -0.5, sparse_inputs=(qi, ki, wt), n_sel=n_sel)

        mq, mk, mv = _mla_prep(mla, pos, b, t, mla_qa_norm[i], mla_kva_norm[i], mla_w_uq[i],
                               mla_w_ukv[i], mla_q_norm[i], mla_k_norm[i])
        out_c = _attention(mq, mk, mv, b, t, MLA_QK ** -0.5)

        merged = _merge(hn, (out_a, out_b, out_c), w_gates, w_branch[i].astype(BF16))
        h = _matmul(merged, w_out[i].astype(BF16), F32, res=h, name="out_proj")

        if i % 2 == 0:
            j = i // 2
            hn2 = _rmsnorm(h, ln_ffn[i])
            tm = _tile(m, 512, SUBLANES)
            h = _ffn(hn2, dense_w_gate[j:j + 1].astype(BF16), dense_w_up[j:j + 1].astype(BF16),
                     dense_w_down[j:j + 1].astype(BF16), jnp.zeros((m // tm,), I32),
                     jnp.full((1,), m // tm, I32), tm, res=h)
            hn3 = _rmsnorm(h, ple_norm[i])
        else:
            j = i // 2
            h, hn3 = _moe(h, ln_ffn[i], moe_router[j], moe_w_gate[j], moe_w_up[j], moe_w_down[j],
                          ple_norm[i])
        h = _ple(h, hn3, p[i].reshape(m, -1), ple_w_gate[i].astype(BF16), ple_w_proj[i].astype(BF16))
    return h.reshape(b, t, d)
```

```python
import functools

import jax
import jax.numpy as jnp
from jax import lax
from jax.experimental import pallas as pl
from jax.experimental.pallas import tpu as pltpu

F32, BF16, I32 = jnp.float32, jnp.bfloat16, jnp.int32

EPS = 1e-6
ROPE_THETA = 10000.0
LRU_C = 8.0
LRU_WIDTH = 1024
LRU_BLOCKS = 8
ATT_HEADS = 8
ATT_HEAD_DIM = 128
IDX_HEADS = 16
IDX_HEAD_DIM = 64
TOPK_MAX = 256
MLA_HEADS = 8
MLA_Q_LORA = 768
MLA_KV_LORA = 512
MLA_NOPE = 128
MLA_ROPE = 64
MLA_QK = MLA_NOPE + MLA_ROPE
MLA_V = 128
N_BRANCH = 3
N_EXPERTS = 8
TOP_K = 2

LANES = 128
SUBLANES = 8
VMEM_LIMIT_BYTES = 56 << 20
MOE_ROWS = 768
INT_MIN = -(2 ** 31)
NEG_INF_KEY = 0x807FFFFF - 2 ** 32

NT_DIMS = (((1,), (1,)), ((), ()))
LOG2_E = 1.4426950408889634
QK_LOOKAHEAD = 4


def _params(sem):
    return pltpu.CompilerParams(dimension_semantics=sem, vmem_limit_bytes=VMEM_LIMIT_BYTES)


def _tile(n, cap, unit=LANES):
    if n <= cap:
        return n
    best = None
    for t in range(unit, cap + 1, unit):
        if n % t == 0:
            best = t
    assert best is not None, (n, cap)
    return best


def _rmsnorm_kernel(x_ref, g_ref, o_ref):
    x = x_ref[...]
    ms = jnp.mean(x * x, axis=-1, keepdims=True)
    o_ref[...] = (x * lax.rsqrt(ms + EPS) * g_ref[...]).astype(o_ref.dtype)


def _rmsnorm(x, g):
    m, d = x.shape
    tm = _tile(m, 512, SUBLANES)
    return pl.pallas_call(
        _rmsnorm_kernel,
        grid=(m // tm,),
        in_specs=[pl.BlockSpec((tm, d), lambda i: (i, 0)), pl.BlockSpec((1, d), lambda i: (0, 0))],
        out_specs=pl.BlockSpec((tm, d), lambda i: (i, 0)),
        out_shape=jax.ShapeDtypeStruct((m, d), BF16),
        compiler_params=_params(("parallel",)),
        name="rmsnorm",
    )(x, g.reshape(1, d))


def _mm_kernel(x_ref, w_ref, o_ref):
    o_ref[...] = jnp.dot(x_ref[...], w_ref[...], preferred_element_type=F32).astype(o_ref.dtype)


def _mm_nt_kernel(x_ref, wt_ref, o_ref):
    o_ref[...] = lax.dot_general(x_ref[...], wt_ref[...], NT_DIMS,
                                 preferred_element_type=F32).astype(o_ref.dtype)


def _mm_res_kernel(x_ref, w_ref, r_ref, o_ref):
    o_ref[...] = r_ref[...] + jnp.dot(x_ref[...], w_ref[...], preferred_element_type=F32)


def _matmul(x, w, out_dtype, res=None, tm_cap=1024, tn_cap=1024, transposed=False, name="matmul"):
    m, k = x.shape
    n = w.shape[0] if transposed else w.shape[1]
    tm = _tile(m, tm_cap, SUBLANES)
    tn = _tile(n, tn_cap)
    if transposed:
        w_spec = pl.BlockSpec((tn, k), lambda i, j: (j, 0))
    else:
        w_spec = pl.BlockSpec((k, tn), lambda i, j: (0, j))
    in_specs = [pl.BlockSpec((tm, k), lambda i, j: (i, 0)), w_spec]
    args = [x, w]
    body = _mm_nt_kernel if transposed else _mm_kernel
    if res is not None:
        assert not transposed
        in_specs.append(pl.BlockSpec((tm, tn), lambda i, j: (i, j)))
        args.append(res)
        body = _mm_res_kernel
    return pl.pallas_call(
        body,
        grid=(m // tm, n // tn),
        in_specs=in_specs,
        out_specs=pl.BlockSpec((tm, tn), lambda i, j: (i, j)),
        out_shape=jax.ShapeDtypeStruct((m, n), out_dtype),
        compiler_params=_params(("parallel", "arbitrary")),
        name=name,
    )(*args)


def _store_heads_t(o_ref, rt):
    n_heads, dv = o_ref.shape[1], o_ref.shape[3]
    for h in range(n_heads):
        o_ref[0, h, 0] = rt[h * dv:(h + 1) * dv, :]


def _values_t_kernel(x_ref, wt_ref, o_ref):
    _store_heads_t(o_ref, lax.dot_general(wt_ref[...], x_ref[...], NT_DIMS,
                                          preferred_element_type=F32).astype(o_ref.dtype))


def _values_t(x, wt, b, t, n_heads):
    k = x.shape[1]
    dv = wt.shape[0] // n_heads
    tk = _tile(t, 256, LANES)
    nk = t // tk
    return pl.pallas_call(
        _values_t_kernel,
        grid=(b, nk),
        in_specs=[pl.BlockSpec((tk, k), lambda bi, ci: (bi * nk + ci, 0)),
                  pl.BlockSpec(wt.shape, lambda bi, ci: (0, 0))],
        out_specs=pl.BlockSpec((1, n_heads, 1, dv, tk), lambda bi, ci: (bi, 0, ci, 0, 0)),
        out_shape=jax.ShapeDtypeStruct((b, n_heads, nk, dv, tk), BF16),
        compiler_params=_params(("parallel", "parallel")),
        name="in_proj_v",
    )(x, wt)


EXPM1_SERIES_BOUND = 0.25
EXPM1_SERIES_TERMS = 10


def _expm1(y):
    poly = jnp.full_like(y, 1.0 / 3628800.0)
    fact = 3628800.0
    for n in range(EXPM1_SERIES_TERMS, 1, -1):
        fact /= n
        poly = poly * y + 1.0 / fact
    return jnp.where(jnp.abs(y) < EXPM1_SERIES_BOUND, poly * y, jnp.exp(y) - 1.0)


def _rglru_kernel(x_ref, g_ref, cw_ref, cb_ref, wa_ref, ba_ref, wx_ref, bx_ref, lam_ref, o_ref,
                  xs_scr, a_scr, b_scr, h_scr, *, tt):
    c = x_ref.shape[1]
    t = pl.program_id(1)

    @pl.when(t == 0)
    def _():
        xs_scr[0:SUBLANES, :] = jnp.zeros((SUBLANES, c), F32)
        h_scr[...] = jnp.zeros_like(h_scr)

    x = x_ref[...]
    xs_scr[SUBLANES:SUBLANES + tt, :] = x
    cw = cw_ref[...]
    xc = (xs_scr[SUBLANES - 3:SUBLANES - 3 + tt, :] * cw[0:1, :]
          + xs_scr[SUBLANES - 2:SUBLANES - 2 + tt, :] * cw[1:2, :]
          + xs_scr[SUBLANES - 1:SUBLANES - 1 + tt, :] * cw[2:3, :]
          + x * cw[3:4, :]) + cb_ref[...]
    xs_scr[0:SUBLANES, :] = x[tt - SUBLANES:tt, :]

    xcb = xc.astype(BF16)
    bw = c // LRU_BLOCKS
    ra = jnp.concatenate(
        [jnp.dot(xcb[:, n * bw:(n + 1) * bw], wa_ref[n], preferred_element_type=F32)
         for n in range(LRU_BLOCKS)], axis=1) + ba_ref[...]
    rx = jnp.concatenate(
        [jnp.dot(xcb[:, n * bw:(n + 1) * bw], wx_ref[n], preferred_element_type=F32)
         for n in range(LRU_BLOCKS)], axis=1) + bx_ref[...]
    r = jax.nn.sigmoid(ra)
    gi = jax.nn.sigmoid(rx)
    nlam = -lam_ref[...]
    softplus = jnp.maximum(nlam, 0.0) + jnp.log1p(jnp.exp(-jnp.abs(nlam)))
    log_a = (-LRU_C) * r * softplus
    a_scr[...] = jnp.exp(log_a)
    b_scr[...] = jnp.sqrt(-_expm1(2.0 * log_a)) * (gi * xc)

    row = lax.broadcasted_iota(I32, (SUBLANES, c), 0)

    def group(gidx, h):
        off = pl.multiple_of(gidx * SUBLANES, SUBLANES)
        a8 = a_scr[pl.ds(off, SUBLANES), :]
        b8 = b_scr[pl.ds(off, SUBLANES), :]
        for s in (1, 2, 4):
            keep = row >= s
            a_sh = jnp.where(keep, pltpu.roll(a8, s, 0), 1.0)
            b_sh = jnp.where(keep, pltpu.roll(b8, s, 0), 0.0)
            b8 = a8 * b_sh + b8
            a8 = a8 * a_sh
        h8 = a8 * h + b8
        b_scr[pl.ds(off, SUBLANES), :] = h8
        return h8[SUBLANES - 1:SUBLANES, :]

    h_scr[...] = lax.fori_loop(0, tt // SUBLANES, group, h_scr[...])
    o_ref[...] = (b_scr[...] * jax.nn.gelu(g_ref[...], approximate=True)).astype(o_ref.dtype)


def _rglru(proj, b, t, conv_w, conv_b, wa, ba, wx, bx, lam):
    c = LRU_WIDTH
    tt = _tile(t, 256, SUBLANES)
    nt = t // tt
    row = lambda v: v.reshape(1, c)
    wspec = pl.BlockSpec(wa.shape, lambda bi, ti: (0, 0, 0))
    vspec = pl.BlockSpec((1, c), lambda bi, ti: (0, 0))
    return pl.pallas_call(
        functools.partial(_rglru_kernel, tt=tt),
        grid=(b, nt),
        in_specs=[pl.BlockSpec((tt, c), lambda bi, ti: (bi * nt + ti, 0)),
                  pl.BlockSpec((tt, c), lambda bi, ti: (bi * nt + ti, 1)),
                  pl.BlockSpec(conv_w.shape, lambda bi, ti: (0, 0)), vspec,
                  wspec, vspec, wspec, vspec, vspec],
        out_specs=pl.BlockSpec((tt, c), lambda bi, ti: (bi * nt + ti, 0)),
        out_shape=jax.ShapeDtypeStruct((b * t, c), BF16),
        scratch_shapes=[pltpu.VMEM((tt + SUBLANES, c), F32), pltpu.VMEM((tt, c), F32),
                        pltpu.VMEM((tt, c), F32), pltpu.VMEM((1, c), F32)],
        compiler_params=_params(("parallel", "arbitrary")),
        name="rglru",
    )(proj, proj, conv_w, row(conv_b), wa.astype(BF16), row(ba), wx.astype(BF16), row(bx), row(lam))


def _inv_freq_lanes(d):
    f = ROPE_THETA ** (-jnp.arange(0, d, 2, dtype=F32) / d)
    return jnp.tile(jnp.concatenate([f, f]), LANES // d).reshape(1, LANES)


def _rope_tables(pos_ref, invf_ref, half):
    ang = pos_ref[...].astype(F32) * invf_ref[...]
    lane = lax.broadcasted_iota(I32, ang.shape, 1)
    first = (lane & (2 * half - 1)) < half
    return jnp.cos(ang), jnp.where(first, -jnp.sin(ang), jnp.sin(ang)), first


def _swap_halves(x, half, first):
    if 2 * half == LANES:
        return pltpu.roll(x, half, 1)
    return jnp.where(first, pltpu.roll(x, LANES - half, 1), pltpu.roll(x, half, 1))


def _qk_prep_kernel(x_ref, pos_ref, invf_ref, gq_ref, gk_ref, q_ref, k_ref):
    cosf, sinf, first = _rope_tables(pos_ref, invf_ref, ATT_HEAD_DIM // 2)
    for which, g_ref, o_ref in ((0, gq_ref, q_ref), (1, gk_ref, k_ref)):
        for h in range(ATT_HEADS):
            lo = (which * ATT_HEADS + h) * ATT_HEAD_DIM
            s = x_ref[:, lo:lo + ATT_HEAD_DIM]
            y = s * lax.rsqrt(jnp.mean(s * s, axis=-1, keepdims=True) + EPS) * g_ref[...]
            y = y * cosf + _swap_halves(y, ATT_HEAD_DIM // 2, first) * sinf
            o_ref[0, h] = y.astype(o_ref.dtype)


def _qk_prep(proj, pos, b, t, gq, gk):
    tm = _tile(t, 256, SUBLANES)
    nt = t // tm
    hd = ATT_HEADS * ATT_HEAD_DIM
    ospec = pl.BlockSpec((1, ATT_HEADS, tm, ATT_HEAD_DIM), lambda bi, ti: (bi, 0, ti, 0))
    oshape = jax.ShapeDtypeStruct((b, ATT_HEADS, t, ATT_HEAD_DIM), BF16)
    vspec = pl.BlockSpec((1, LANES), lambda bi, ti: (0, 0))
    return pl.pallas_call(
        _qk_prep_kernel,
        grid=(b, nt),
        in_specs=[pl.BlockSpec((tm, 2 * hd), lambda bi, ti: (bi * nt + ti, 1)),
                  pl.BlockSpec((tm, 1), lambda bi, ti: (bi * nt + ti, 0)), vspec, vspec, vspec],
        out_specs=[ospec, ospec],
        out_shape=[oshape, oshape],
        compiler_params=_params(("parallel", "parallel")),
        name="qk_prep",
    )(proj, pos, _inv_freq_lanes(ATT_HEAD_DIM), gq.reshape(1, LANES), gk.reshape(1, LANES))


def _idx_prep_kernel(x_ref, pos_ref, invf_ref, qi_ref, ki_ref):
    half = IDX_HEAD_DIM // 2
    cosf, sinf, first = _rope_tables(pos_ref, invf_ref, half)
    lane = lax.broadcasted_iota(I32, cosf.shape, 1)
    left = lane < IDX_HEAD_DIM

    def rope(x):
        return x * cosf + _swap_halves(x, half, first) * sinf

    def split(y):
        hi = y.astype(BF16).astype(F32)
        return hi, y - hi

    for j in range(IDX_HEADS // 2):
        hi, lo = split(rope(x_ref[:, j * LANES:(j + 1) * LANES]))
        even = jnp.where(left, hi, pltpu.roll(lo, IDX_HEAD_DIM, 1)).astype(BF16)
        odd = jnp.where(left, pltpu.roll(hi, IDX_HEAD_DIM, 1), lo).astype(BF16)
        for h, v in ((2 * j, even), (2 * j + 1, odd)):
            qi_ref[0, h, :, 0:LANES] = v
            qi_ref[0, h, :, LANES:2 * LANES] = v
    kcol = IDX_HEADS * IDX_HEAD_DIM
    khi, klo = split(rope(x_ref[:, kcol:kcol + LANES]))
    ki_ref[0, :, 0:LANES] = jnp.where(left, khi, pltpu.roll(khi, IDX_HEAD_DIM, 1)).astype(BF16)
    ki_ref[0, :, LANES:2 * LANES] = jnp.where(left, klo, pltpu.roll(klo, IDX_HEAD_DIM, 1)).astype(BF16)


def _idx_prep(idx, pos, b, t):
    tm = _tile(t, 256, SUBLANES)
    nt = t // tm
    w = idx.shape[1]
    return pl.pallas_call(
        _idx_prep_kernel,
        grid=(b, nt),
        in_specs=[pl.BlockSpec((tm, w), lambda bi, ti: (bi * nt + ti, 0)),
                  pl.BlockSpec((tm, 1), lambda bi, ti: (bi * nt + ti, 0)),
                  pl.BlockSpec((1, LANES), lambda bi, ti: (0, 0))],
        out_specs=[pl.BlockSpec((1, IDX_HEADS, tm, 2 * LANES), lambda bi, ti: (bi, 0, ti, 0)),
                   pl.BlockSpec((1, tm, 2 * LANES), lambda bi, ti: (bi, ti, 0))],
        out_shape=[jax.ShapeDtypeStruct((b, IDX_HEADS, t, 2 * LANES), BF16),
                   jax.ShapeDtypeStruct((b, t, 2 * LANES), BF16)],
        compiler_params=_params(("parallel", "parallel")),
        name="idx_prep",
    )(idx, pos, _inv_freq_lanes(IDX_HEAD_DIM))


def _mla_prep_kernel(m_ref, pos_ref, invf_ref, qa_ref, kva_ref, wuq_ref, wuk_ref, wuvt_ref,
                     qnn_ref, qnr_ref, knn_ref, knr_ref, q_ref, k_ref, vt_ref):
    half = MLA_ROPE // 2
    cosf, sinf, first = _rope_tables(pos_ref, invf_ref, half)
    lane = lax.broadcasted_iota(I32, cosf.shape, 1)
    left = lane < MLA_ROPE

    def rope(x):
        return x * cosf + _swap_halves(x, half, first) * sinf

    def norm(x, g_ref):
        return (x * lax.rsqrt(jnp.mean(x * x, axis=-1, keepdims=True) + EPS) * g_ref[...]).astype(BF16)

    cq = norm(m_ref[:, 0:MLA_Q_LORA], qa_ref)
    ckv = norm(m_ref[:, MLA_Q_LORA:MLA_Q_LORA + MLA_KV_LORA], kva_ref)
    kr = m_ref[:, MLA_Q_LORA + MLA_KV_LORA:MLA_Q_LORA + MLA_KV_LORA + LANES]
    qf = jnp.dot(cq, wuq_ref[...], preferred_element_type=F32)
    kvf = jnp.dot(ckv, wuk_ref[...], preferred_element_type=F32)
    nope_w = MLA_HEADS * MLA_NOPE
    _store_heads_t(vt_ref, lax.dot_general(wuvt_ref[...], ckv, NT_DIMS,
                                           preferred_element_type=F32).astype(vt_ref.dtype))

    for j in range(MLA_HEADS // 2):
        rs = qf[:, nope_w + j * LANES:nope_w + (j + 1) * LANES]
        sq = rs * rs
        ss_pair = (jnp.sum(jnp.where(left, sq, 0.0), axis=-1, keepdims=True),
                   jnp.sum(jnp.where(left, 0.0, sq), axis=-1, keepdims=True))
        for par in range(2):
            h = 2 * j + par
            nope = qf[:, h * MLA_NOPE:(h + 1) * MLA_NOPE]
            ms = (jnp.sum(nope * nope, axis=-1, keepdims=True) + ss_pair[par]) * (1.0 / MLA_QK)
            rsq = lax.rsqrt(ms + EPS)
            q_ref[0, h, :, 0:LANES] = (nope * rsq * qnn_ref[...]).astype(q_ref.dtype)
            rr = rope(rs * rsq * qnr_ref[...])
            if par == 1:
                rr = pltpu.roll(rr, MLA_ROPE, 1)
            q_ref[0, h, :, LANES:2 * LANES] = jnp.where(left, rr, 0.0).astype(q_ref.dtype)

    ss_kr = jnp.sum(jnp.where(left, kr * kr, 0.0), axis=-1, keepdims=True)
    base = jnp.where(left, rope(kr * knr_ref[...]), 0.0)
    for h in range(MLA_HEADS):
        nope = kvf[:, h * MLA_NOPE:(h + 1) * MLA_NOPE]
        ms = (jnp.sum(nope * nope, axis=-1, keepdims=True) + ss_kr) * (1.0 / MLA_QK)
        rsq = lax.rsqrt(ms + EPS)
        k_ref[0, h, :, 0:LANES] = (nope * rsq * knn_ref[...]).astype(k_ref.dtype)
        k_ref[0, h, :, LANES:2 * LANES] = (base * rsq).astype(k_ref.dtype)


def _mla_prep(mla, pos, b, t, qa, kva, w_uq, w_ukv, qn, kn):
    tm = _tile(t, 256, SUBLANES)
    nt = t // tm
    wq = w_uq.reshape(MLA_Q_LORA, MLA_HEADS, MLA_QK)
    wq = jnp.concatenate([wq[:, :, :MLA_NOPE].reshape(MLA_Q_LORA, -1),
                          wq[:, :, MLA_NOPE:].reshape(MLA_Q_LORA, -1)], axis=1).astype(BF16)
    wkv = w_ukv.reshape(MLA_KV_LORA, MLA_HEADS, MLA_NOPE + MLA_V)
    wk = wkv[:, :, :MLA_NOPE].reshape(MLA_KV_LORA, -1).astype(BF16)
    wvt = wkv[:, :, MLA_NOPE:].reshape(MLA_KV_LORA, -1).T.astype(BF16)
    dup = lambda g: jnp.tile(g[MLA_NOPE:], 2).reshape(1, LANES)
    full = lambda a: pl.BlockSpec(a.shape, lambda bi, ti: (0,) * a.ndim)
    consts = [_inv_freq_lanes(MLA_ROPE), qa.reshape(1, -1), kva.reshape(1, -1), wq, wk, wvt,
              qn[:MLA_NOPE].reshape(1, LANES), dup(qn), kn[:MLA_NOPE].reshape(1, LANES), dup(kn)]
    hspec = pl.BlockSpec((1, MLA_HEADS, tm, 2 * LANES), lambda bi, ti: (bi, 0, ti, 0))
    hshape = jax.ShapeDtypeStruct((b, MLA_HEADS, t, 2 * LANES), BF16)
    return pl.pallas_call(
        _mla_prep_kernel,
        grid=(b, nt),
        in_specs=[pl.BlockSpec((tm, mla.shape[1]), lambda bi, ti: (bi * nt + ti, 0)),
                  pl.BlockSpec((tm, 1), lambda bi, ti: (bi * nt + ti, 0))] + [full(a) for a in consts],
        out_specs=[hspec, hspec,
                   pl.BlockSpec((1, MLA_HEADS, 1, MLA_V, tm), lambda bi, ti: (bi, 0, ti, 0, 0))],
        out_shape=[hshape, hshape, jax.ShapeDtypeStruct((b, MLA_HEADS, nt, MLA_V, tm), BF16)],
        compiler_params=_params(("parallel", "parallel")),
        name="mla_prep",
    )(mla, pos, *consts)


def _attn_kernel(*refs, sparse, n_sel, tq, scale, idx_scale, t_total):
    if sparse:
        (q_ref, k_ref, vt_ref, qi_ref, ki_ref, wt_ref, o_ref, m_scr, l_scr, acc_scr,
         score_scr, bias_scr, thr_scr, tie_scr) = refs
    else:
        q_ref, k_ref, vt_ref, o_ref, m_scr, l_scr, acc_scr = refs
    n_heads, dv = vt_ref.shape[1], vt_ref.shape[3]
    qb = pl.program_id(1)
    nkc = qb + 1
    kio = lax.broadcasted_iota(I32, (tq, tq), 0)
    qio = lax.broadcasted_iota(I32, (tq, tq), 1)
    neg_inf = -jnp.inf

    def chunk_off(c):
        return pl.multiple_of(c * tq, tq)

    def causal(c):
        return (c * tq + kio) <= (qb * tq + qio)

    if sparse:
        n_idx_heads = qi_ref.shape[1]

        def score_chunk(c, carry):
            koff = chunk_off(c)
            ki = ki_ref[0, pl.ds(koff, tq), :]
            acc = jnp.zeros((tq, tq), F32)
            for h in range(n_idx_heads):
                s = lax.dot_general(ki, qi_ref[0, h], NT_DIMS, preferred_element_type=F32)
                acc = acc + jnp.maximum(s, 0.0) * wt_ref[0, h:h + 1, :]
            score_scr[pl.ds(koff, tq), :] = jnp.where(causal(c), acc * idx_scale, neg_inf)
            return carry

        lax.fori_loop(0, nkc, score_chunk, 0)

        def key_to_float(key):
            val = pltpu.bitcast(jnp.where(key < 0, key ^ 0x7FFFFFFF, key), F32)
            return jnp.where(key < NEG_INF_KEY, neg_inf, val)

        def count(pred):
            def hits(c):
                x = score_scr[pl.ds(chunk_off(c), tq), :]
                m = jnp.where(pred(x, c), 1.0, 0.0)
                return jnp.sum(m.reshape(tq // SUBLANES, SUBLANES, tq), axis=0)

            def body(pair, acc):
                c1 = 2 * pair
                has_second = c1 + 1 < nkc
                second = hits(jnp.where(has_second, c1 + 1, c1))
                return acc + hits(c1) + jnp.where(has_second, 1.0, 0.0) * second
            acc = lax.fori_loop(0, (nkc + 1) // 2, body, jnp.zeros((SUBLANES, tq), F32))
            return jnp.sum(acc, axis=0, keepdims=True)

        def count_ge(key):
            cand = key_to_float(key)
            return count(lambda x, c: x >= cand)

        k_sel = jnp.float32(n_sel)
        zero = jnp.zeros((1, tq), I32)
        thr_key = jnp.where(count_ge(zero) >= k_sel, zero, jnp.full((1, tq), INT_MIN, I32))

        def bit_step(i, key):
            cand = key | lax.shift_left(jnp.int32(1), jnp.int32(30) - i)
            return jnp.where(count_ge(cand) >= k_sel, cand, key)

        thr_key = lax.fori_loop(0, 31, bit_step, thr_key)
        thr = key_to_float(thr_key)
        n_ge = count(lambda x, c: x >= thr)
        thr_scr[0:1, :] = thr
        tie_scr[0:1, :] = jnp.full((1, tq), t_total, I32)

        @pl.when(jnp.max(n_ge) > k_sel)
        def _():
            need = k_sel - count(lambda x, c: x > thr)

            def below(x, c, m):
                return jnp.logical_and(x == thr, (c * tq + kio) < m)

            def idx_step(i, lo):
                cand = lo | lax.shift_left(jnp.int32(1), jnp.int32(t_total.bit_length() - 1) - i)
                return jnp.where(count(lambda x, c: below(x, c, cand)) < need, cand, lo)

            lo = lax.fori_loop(0, t_total.bit_length(), idx_step, zero)
            tie_scr[0:1, :] = lo + 1

        thr = thr_scr[0:1, :]
        tie_end = tie_scr[0:1, :]

        def bias_chunk(c, carry):
            koff = chunk_off(c)
            x = score_scr[pl.ds(koff, tq), :]
            tie_ok = jnp.logical_and(x == thr, (c * tq + kio) < tie_end)
            keep = jnp.logical_and(jnp.logical_or(x > thr, tie_ok), causal(c))
            bias_scr[pl.ds(koff, tq), :] = jnp.where(keep, 0.0, neg_inf)
            return carry

        lax.fori_loop(0, nkc, bias_chunk, 0)

    m_scr[...] = jnp.full(m_scr.shape, neg_inf, F32)
    l_scr[...] = jnp.zeros(l_scr.shape, F32)
    acc_scr[...] = jnp.zeros(acc_scr.shape, F32)

    def chunk_step(c, masked):
        koff = chunk_off(c)
        if sparse:
            bias = bias_scr[pl.ds(koff, tq), :]
        elif masked:
            allowed = causal(c)
        def qk(h):
            return lax.dot_general(k_ref[0, h, pl.ds(koff, tq), :], q_ref[0, h], NT_DIMS,
                                   preferred_element_type=F32)

        queued = [qk(h) for h in range(QK_LOOKAHEAD)]
        for h in range(n_heads):
            s = queued.pop(0) * (scale * LOG2_E)
            if h + QK_LOOKAHEAD < n_heads:
                queued.append(qk(h + QK_LOOKAHEAD))
            if sparse:
                s = s + bias
            elif masked:
                s = jnp.where(allowed, s, neg_inf)
            m_old = m_scr[h]
            m_new = jnp.maximum(m_old, jnp.max(s, axis=0, keepdims=True))
            m_ref = jnp.where(m_new == neg_inf, 0.0, m_new)
            alpha = jnp.exp2(m_old - m_ref)
            p = jnp.exp2(s - m_ref)
            l_scr[h] = alpha * l_scr[h] + jnp.sum(p, axis=0, keepdims=True)
            acc_scr[h] = alpha * acc_scr[h] + jnp.dot(vt_ref[0, h, c], p.astype(BF16),
                                                      preferred_element_type=F32)
            m_scr[h] = m_new

    def off_diagonal(c, carry):
        chunk_step(c, False)
        return carry

    if sparse:
        lax.fori_loop(0, nkc, off_diagonal, 0)
    else:
        lax.fori_loop(0, qb, off_diagonal, 0)
        chunk_step(qb, True)
    for h in range(n_heads):
        o_ref[:, h * dv:(h + 1) * dv] = (acc_scr[h] / l_scr[h]).T.astype(o_ref.dtype)


def _attention(q, k, vt, b, t, scale, sparse_inputs=None, n_sel=0):
    h, dq = q.shape[1], q.shape[3]
    nq, dv, tq = vt.shape[2:]
    assert nq * tq == t
    in_specs = [pl.BlockSpec((1, h, tq, dq), lambda bi, qi: (bi, 0, qi, 0)),
                pl.BlockSpec((1, h, t, dq), lambda bi, qi: (bi, 0, 0, 0)),
                pl.BlockSpec((1, h, nq, dv, tq), lambda bi, qi: (bi, 0, 0, 0, 0))]
    args = [q, k, vt]
    scratch = [pltpu.VMEM((h, 1, tq), F32), pltpu.VMEM((h, 1, tq), F32), pltpu.VMEM((h, dv, tq), F32)]
    sparse = sparse_inputs is not None
    if sparse:
        qi_, ki_, wt_ = sparse_inputs
        hi, dk = qi_.shape[1], qi_.shape[3]
        in_specs += [pl.BlockSpec((1, hi, tq, dk), lambda bi, qi: (bi, 0, qi, 0)),
                     pl.BlockSpec((1, t, dk), lambda bi, qi: (bi, 0, 0)),
                     pl.BlockSpec((1, hi, tq), lambda bi, qi: (bi, 0, qi))]
        args += [qi_, ki_, wt_]
        scratch += [pltpu.VMEM((t, tq), F32), pltpu.VMEM((t, tq), F32),
                    pltpu.VMEM((SUBLANES, tq), F32), pltpu.VMEM((SUBLANES, tq), I32)]
    return pl.pallas_call(
        functools.partial(_attn_kernel, sparse=sparse, n_sel=n_sel, tq=tq, scale=scale,
                          idx_scale=IDX_HEADS ** -0.5 * IDX_HEAD_DIM ** -0.5, t_total=t),
        grid=(b, nq),
        in_specs=in_specs,
        out_specs=pl.BlockSpec((tq, h * dv), lambda bi, qi: (bi * nq + qi, 0)),
        out_shape=jax.ShapeDtypeStruct((b * t, h * dv), BF16),
        scratch_shapes=scratch,
        compiler_params=_params(("parallel", "arbitrary")),
        name="sparse_attention" if sparse else "dense_attention",
    )(*args)


def _merge_kernel(hn_ref, a_ref, b_ref, c_ref, wg_ref, wb_ref, o_ref):
    hn = hn_ref[...]
    acc = None
    for n, br_ref in enumerate((a_ref, b_ref, c_ref)):
        gate = jax.nn.sigmoid(lax.dot_general(hn, wg_ref[n], NT_DIMS, preferred_element_type=F32))
        term = gate * jnp.dot(br_ref[...], wb_ref[n], preferred_element_type=F32)
        acc = term if acc is None else acc + term
    o_ref[...] = acc.astype(o_ref.dtype)


def _merge(hn, branches, w_gate, w_branch):
    m, d = hn.shape
    bw = w_branch.shape[1]
    tm = _tile(m, 512, SUBLANES)
    tn = _tile(d, 512)
    bspec = pl.BlockSpec((tm, bw), lambda i, j: (i, 0))
    return pl.pallas_call(
        _merge_kernel,
        grid=(m // tm, d // tn),
        in_specs=[pl.BlockSpec((tm, d), lambda i, j: (i, 0)), bspec, bspec, bspec,
                  pl.BlockSpec((N_BRANCH, tn, d), lambda i, j: (0, j, 0)),
                  pl.BlockSpec((N_BRANCH, bw, tn), lambda i, j: (0, 0, j))],
        out_specs=pl.BlockSpec((tm, tn), lambda i, j: (i, j)),
        out_shape=jax.ShapeDtypeStruct((m, d), BF16),
        compiler_params=_params(("parallel", "arbitrary")),
        name="gated_merge",
    )(hn, *branches, w_gate, w_branch)


def _ple_kernel(h_ref, hn_ref, p_ref, wg_ref, wp_ref, o_ref):
    gate = jax.nn.sigmoid(jnp.dot(hn_ref[...], wg_ref[...], preferred_element_type=F32))
    emb = jnp.dot(p_ref[...].astype(BF16), wp_ref[...], preferred_element_type=F32)
    o_ref[...] = h_ref[...] + emb * gate


def _ple(h, hn, p, w_gate, w_proj):
    m, d = h.shape
    pd = p.shape[1]
    tm = _tile(m, 1024, SUBLANES)
    tn = _tile(d, 1024)
    return pl.pallas_call(
        _ple_kernel,
        grid=(m // tm, d // tn),
        in_specs=[pl.BlockSpec((tm, tn), lambda i, j: (i, j)), pl.BlockSpec((tm, d), lambda i, j: (i, 0)),
                  pl.BlockSpec((tm, pd), lambda i, j: (i, 0)), pl.BlockSpec((d, tn), lambda i, j: (0, j)),
                  pl.BlockSpec((pd, tn), lambda i, j: (0, j))],
        out_specs=pl.BlockSpec((tm, tn), lambda i, j: (i, j)),
        out_shape=jax.ShapeDtypeStruct((m, d), F32),
        compiler_params=_params(("parallel", "arbitrary")),
        name="ple",
    )(h, hn, p, w_gate, w_proj)


def _ffn_kernel(be_ref, nu_ref, x_ref, wg_ref, wu_ref, wd_ref, *rest, has_res):
    if has_res:
        res_ref, o_ref = rest
    else:
        (o_ref,) = rest
    i, j = pl.program_id(0), pl.program_id(1)

    @pl.when(j == 0)
    def _():
        o_ref[...] = res_ref[...] if has_res else jnp.zeros_like(o_ref)

    @pl.when(i < nu_ref[0])
    def _():
        x = x_ref[...].astype(BF16)
        g = jnp.dot(x, wg_ref[0].astype(BF16), preferred_element_type=F32)
        u = jnp.dot(x, wu_ref[0].astype(BF16), preferred_element_type=F32)
        act = (g * jax.nn.sigmoid(g) * u).astype(BF16)
        o_ref[...] += jnp.dot(act, wd_ref[0].astype(BF16), preferred_element_type=F32)


def _ffn(x, w_gate, w_up, w_down, blk_expert, n_used, tm, res=None):
    r, d = x.shape
    f = w_gate.shape[2]
    tf = _tile(f, 512)
    nf = f // tf
    assert nf >= 2

    def jj(i, j, nu):
        return jnp.where(i < nu[0], j, nf - 1)

    rows_mode = pl.Buffered(1) if w_gate.dtype == F32 else None
    in_specs = [pl.BlockSpec((tm, d), lambda i, j, be, nu: (i, 0), pipeline_mode=rows_mode),
                pl.BlockSpec((1, d, tf), lambda i, j, be, nu: (be[i], 0, jj(i, j, nu))),
                pl.BlockSpec((1, d, tf), lambda i, j, be, nu: (be[i], 0, jj(i, j, nu))),
                pl.BlockSpec((1, tf, d), lambda i, j, be, nu: (be[i], jj(i, j, nu), 0))]
    args = [x, w_gate, w_up, w_down]
    if res is not None:
        in_specs.append(pl.BlockSpec((tm, d), lambda i, j, be, nu: (i, 0)))
        args.append(res)
    return pl.pallas_call(
        functools.partial(_ffn_kernel, has_res=res is not None),
        grid_spec=pltpu.PrefetchScalarGridSpec(
            num_scalar_prefetch=2,
            grid=(r // tm, nf),
            in_specs=in_specs,
            out_specs=pl.BlockSpec((tm, d), lambda i, j, be, nu: (i, 0), pipeline_mode=rows_mode)),
        out_shape=jax.ShapeDtypeStruct((r, d), F32),
        compiler_params=_params(("arbitrary", "arbitrary")),
        name="swiglu",
    )(blk_expert, n_used, *args)


def _router_kernel(h_ref, g_ref, rhi_ref, rlo_ref, hn_ref, info_ref, cnt_ref, run_scr, *, n_experts):
    i = pl.program_id(0)
    tm = h_ref.shape[0]

    @pl.when(i == 0)
    def _():
        run_scr[...] = jnp.zeros_like(run_scr)

    x = h_ref[...]
    xn = x * lax.rsqrt(jnp.mean(x * x, axis=-1, keepdims=True) + EPS) * g_ref[...]
    hi = xn.astype(BF16)
    hn_ref[...] = xn
    lo = (xn - hi.astype(F32)).astype(BF16)
    logits = (jnp.dot(hi, rhi_ref[...], preferred_element_type=F32)
              + jnp.dot(hi, rlo_ref[...], preferred_element_type=F32)
              + jnp.dot(lo, rhi_ref[...], preferred_element_type=F32))
    lane = lax.broadcasted_iota(I32, logits.shape, 1)
    lane_f = lane.astype(F32)
    logits = jnp.where(lane < n_experts, logits, -jnp.inf)

    def top(vals):
        v = jnp.max(vals, axis=-1, keepdims=True)
        idx = jnp.min(jnp.where(vals == v, lane_f, float(LANES)), axis=-1, keepdims=True)
        return v, idx

    v1, i1 = top(logits)
    v2, i2 = top(jnp.where(lane_f == i1, -jnp.inf, logits))
    e2 = jnp.exp(v2 - v1)
    g1 = 1.0 / (1.0 + e2)
    g2 = e2 / (1.0 + e2)

    oh1 = lane_f == i1
    oh2 = lane_f == i2
    both = jnp.where(jnp.logical_or(oh1, oh2), 1.0, 0.0)
    r_io = lax.broadcasted_iota(I32, (tm, tm), 0)
    c_io = lax.broadcasted_iota(I32, (tm, tm), 1)
    strict_lower = jnp.where(c_io < r_io, 1.0, 0.0).astype(BF16)
    before = jnp.dot(strict_lower, both.astype(BF16), preferred_element_type=F32) + run_scr[0:1, :]
    rank1 = jnp.sum(jnp.where(oh1, before, 0.0), axis=-1, keepdims=True)
    rank2 = jnp.sum(jnp.where(oh2, before, 0.0), axis=-1, keepdims=True)
    run_scr[0:1, :] = run_scr[0:1, :] + jnp.sum(both, axis=0, keepdims=True)

    info = jnp.zeros(logits.shape, F32)
    for col, val in enumerate((i1, i2, g1, g2, rank1, rank2)):
        info = jnp.where(lane == col, val, info)
    info_ref[...] = info
    cnt_ref[...] = jnp.broadcast_to(run_scr[0:1, :], cnt_ref.shape)


def _router(h, g, router):
    m, d = h.shape
    e = router.shape[1]
    tm = _tile(m, 256, SUBLANES)
    rp = jnp.zeros((d, LANES), F32).at[:, :e].set(router)
    rhi = rp.astype(BF16)
    rlo = (rp - rhi.astype(F32)).astype(BF16)
    return pl.pallas_call(
        functools.partial(_router_kernel, n_experts=e),
        grid=(m // tm,),
        in_specs=[pl.BlockSpec((tm, d), lambda i: (i, 0)), pl.BlockSpec((1, d), lambda i: (0, 0)),
                  pl.BlockSpec((d, LANES), lambda i: (0, 0)), pl.BlockSpec((d, LANES), lambda i: (0, 0))],
        out_specs=[pl.BlockSpec((tm, d), lambda i: (i, 0)), pl.BlockSpec((tm, LANES), lambda i: (i, 0)),
                   pl.BlockSpec((SUBLANES, LANES), lambda i: (0, 0))],
        out_shape=[jax.ShapeDtypeStruct((m, d), F32), jax.ShapeDtypeStruct((m, LANES), F32),
                   jax.ShapeDtypeStruct((SUBLANES, LANES), F32)],
        scratch_shapes=[pltpu.VMEM((SUBLANES, LANES), F32)],
        compiler_params=_params(("arbitrary",)),
        name="router",
    )(h, g.reshape(1, d), rhi, rlo)


def _row_copy(src, dst, sem, s, d):
    return pltpu.make_async_copy(src.at[pl.ds(s, 1), :], dst.at[pl.ds(d, 1), :], sem)


def _dispatch_kernel(dest_ref, x_ref, xs_in_hbm, xs_hbm, sem, *, rows):
    del xs_in_hbm

    def issue(r, carry):
        for kk in range(TOP_K):
            _row_copy(x_ref, xs_hbm, sem, r, dest_ref[0, 0, TOP_K * r + kk]).start()
        return carry

    lax.fori_loop(0, rows, issue, 0, unroll=8)
    for kk in range(TOP_K):
        pltpu.make_async_copy(x_ref, xs_hbm.at[pl.ds(0, rows), :], sem).wait()


def _dispatch(x, dest, n_rows):
    m, d = x.shape
    rows = _tile(m, 512, SUBLANES)
    return pl.pallas_call(
        functools.partial(_dispatch_kernel, rows=rows),
        grid=(m // rows,),
        in_specs=[pl.BlockSpec((1, 1, TOP_K * rows), lambda i: (i, 0, 0), memory_space=pltpu.SMEM),
                  pl.BlockSpec((rows, d), lambda i: (i, 0)), pl.BlockSpec(memory_space=pl.ANY)],
        out_specs=pl.BlockSpec(memory_space=pl.ANY),
        out_shape=jax.ShapeDtypeStruct((n_rows, d), x.dtype),
        scratch_shapes=[pltpu.SemaphoreType.DMA(())],
        input_output_aliases={2: 0},
        compiler_params=_params(("arbitrary",)),
        name="moe_dispatch",
    )(dest.reshape(m // rows, 1, TOP_K * rows), x, jnp.zeros((n_rows, d), x.dtype))


def _combine_kernel(dest_ref, h_ref, info_ref, g_ref, ys_hbm, o_ref, on_ref, buf, sem, *, rows):
    def issue(r, carry):
        for kk in range(TOP_K):
            pltpu.make_async_copy(ys_hbm.at[pl.ds(dest_ref[0, 0, TOP_K * r + kk], 1), :],
                                  buf.at[kk, pl.ds(r, 1), :], sem).start()
        return carry

    lax.fori_loop(0, rows, issue, 0, unroll=8)
    for kk in range(TOP_K):
        pltpu.make_async_copy(ys_hbm.at[pl.ds(0, rows), :], buf.at[kk], sem).wait()
    info = info_ref[...]
    out = h_ref[...] + (info[:, 2:3] * buf[0] + info[:, 3:4] * buf[1])
    o_ref[...] = out
    on_ref[...] = (out * lax.rsqrt(jnp.mean(out * out, axis=-1, keepdims=True) + EPS)
                   * g_ref[...]).astype(on_ref.dtype)


def _combine(h, info, dest, ys, g):
    m, d = h.shape
    rows = _tile(m, 512, SUBLANES)
    return pl.pallas_call(
        functools.partial(_combine_kernel, rows=rows),
        grid=(m // rows,),
        in_specs=[pl.BlockSpec((1, 1, TOP_K * rows), lambda i: (i, 0, 0), memory_space=pltpu.SMEM),
                  pl.BlockSpec((rows, d), lambda i: (i, 0)), pl.BlockSpec((rows, LANES), lambda i: (i, 0)),
                  pl.BlockSpec((1, d), lambda i: (0, 0)), pl.BlockSpec(memory_space=pl.ANY)],
        out_specs=[pl.BlockSpec((rows, d), lambda i: (i, 0)), pl.BlockSpec((rows, d), lambda i: (i, 0))],
        out_shape=[jax.ShapeDtypeStruct((m, d), F32), jax.ShapeDtypeStruct((m, d), BF16)],
        scratch_shapes=[pltpu.VMEM((TOP_K, rows, d), F32), pltpu.SemaphoreType.DMA(())],
        compiler_params=_params(("arbitrary",)),
        name="moe_combine",
    )(dest.reshape(m // rows, 1, TOP_K * rows), h, info, g.reshape(1, d), ys)


def _moe(h, g_ffn, router, w_gate, w_up, w_down, g_next):
    m, d = h.shape
    e = router.shape[1]
    hn, info, cnt = _router(h, g_ffn, router)
    counts = cnt[0, :e].astype(I32)
    padded = (counts + MOE_ROWS - 1) // MOE_ROWS * MOE_ROWS
    pad_ends = jnp.cumsum(padded)
    pad_starts = pad_ends - padded
    n_blocks = -(-(m * TOP_K) // MOE_ROWS) + e
    ids = info[:, 0:TOP_K].astype(I32)
    dest = jnp.sum(jnp.where(ids[:, :, None] == jnp.arange(e)[None, None, :], pad_starts[None, None, :], 0),
                   axis=-1) + info[:, 4:4 + TOP_K].astype(I32)
    blk_start = jnp.arange(n_blocks, dtype=I32) * MOE_ROWS
    blk_expert = jnp.minimum(jnp.sum(blk_start[:, None] >= pad_ends[None, :], axis=1), e - 1).astype(I32)
    n_used = (pad_ends[e - 1] // MOE_ROWS).astype(I32).reshape(1)
    xs = _dispatch(hn, dest, n_blocks * MOE_ROWS)
    ys = _ffn(xs, w_gate, w_up, w_down, blk_expert, n_used, MOE_ROWS)
    return _combine(h, info, dest, ys, g_next)


N_MAIN = 2 * LRU_WIDTH + 2 * ATT_HEADS * ATT_HEAD_DIM
O_V = N_MAIN
O_IDX = O_V + ATT_HEADS * ATT_HEAD_DIM
N_IDX = IDX_HEADS * IDX_HEAD_DIM + IDX_HEAD_DIM + IDX_HEADS
O_MLA = O_IDX + N_IDX
N_MLA = MLA_Q_LORA + MLA_KV_LORA + MLA_ROPE
O_GATE = O_MLA + N_MLA


def _split_w_in(w):
    d = w.shape[0]
    wt = jnp.swapaxes(w, 0, 1)

    def rows(lo, hi):
        return jnp.pad(wt[lo:hi], ((0, -(hi - lo) % LANES), (0, 0))).astype(BF16)

    gates = rows(O_GATE, O_GATE + N_BRANCH * d).reshape(N_BRANCH, d, d)
    return rows(0, O_V), rows(O_V, O_IDX), rows(O_IDX, O_MLA), rows(O_MLA, O_GATE), gates


def kernel(x, p, positions, ln_mix, w_in, conv_w, conv_b, lru_wa, lru_ba, lru_wx, lru_bx, lru_lambda,
           att_q_norm, att_k_norm, mla_qa_norm, mla_kva_norm, mla_w_uq, mla_w_ukv, mla_q_norm,
           mla_k_norm, w_branch, w_out, ln_ffn, dense_w_gate, dense_w_up, dense_w_down, moe_router,
           moe_w_gate, moe_w_up, moe_w_down, ple_norm, ple_w_gate, ple_w_proj):
    b, t, d = x.shape
    m = b * t
    depth = w_in.shape[0]
    n_sel = min(TOPK_MAX, t // 4)
    pos = positions.reshape(m, 1).astype(I32)
    h = x.reshape(m, d)
    idx_w_off = IDX_HEADS * IDX_HEAD_DIM + IDX_HEAD_DIM
    for i in range(depth):
        w_main, w_v, w_idx, w_mla, w_gates = _split_w_in(w_in[i])
        hn = _rmsnorm(h, ln_mix[i])
        proj = _matmul(hn, w_main, F32, transposed=True, name="in_proj_main")
        vt = _values_t(hn, w_v, b, t, ATT_HEADS)
        idx = _matmul(hn, w_idx, F32, tn_cap=w_idx.shape[0], transposed=True, name="in_proj_idx")
        mla = _matmul(hn, w_mla, F32, tn_cap=w_mla.shape[0], transposed=True, name="in_proj_mla")

        out_a = _rglru(proj, b, t, conv_w[i], conv_b[i], lru_wa[i], lru_ba[i], lru_wx[i], lru_bx[i],
                       lru_lambda[i])

        q, k = _qk_prep(proj, pos, b, t, att_q_norm[i], att_k_norm[i])
        qi, ki = _idx_prep(idx, pos, b, t)
        wt = idx[:, idx_w_off:idx_w_off + IDX_HEADS].reshape(b, t, IDX_HEADS).transpose(0, 2, 1)
        out_b = _attention(q, k, vt, b, t, ATT_HEAD_DIM ** -0.5, sparse_inputs=(qi, ki, wt), n_sel=n_sel)

        mq, mk, mv = _mla_prep(mla, pos, b, t, mla_qa_norm[i], mla_kva_norm[i], mla_w_uq[i],
                               mla_w_ukv[i], mla_q_norm[i], mla_k_norm[i])
        out_c = _attention(mq, mk, mv, b, t, MLA_QK ** -0.5)

        merged = _merge(hn, (out_a, out_b, out_c), w_gates, w_branch[i].astype(BF16))
        h = _matmul(merged, w_out[i].astype(BF16), F32, res=h, name="out_proj")

        if i % 2 == 0:
            j = i // 2
            hn2 = _rmsnorm(h, ln_ffn[i])
            tm = _tile(m, 512, SUBLANES)
            h = _ffn(hn2, dense_w_gate[j:j + 1].astype(BF16), dense_w_up[j:j + 1].astype(BF16),
                     dense_w_down[j:j + 1].astype(BF16), jnp.zeros((m // tm,), I32),
                     jnp.full((1,), m // tm, I32), tm, res=h)
            hn3 = _rmsnorm(h, ple_norm[i])
        else:
            j = i // 2
            h, hn3 = _moe(h, ln_ffn[i], moe_router[j], moe_w_gate[j], moe_w_up[j], moe_w_down[j],
                          ple_norm[i])
        h = _ple(h, hn3, p[i].reshape(m, -1), ple_w_gate[i].astype(BF16), ple_w_proj[i].astype(BF16))
    return h.reshape(b, t, d)
```

```python
import functools

import jax
import jax.numpy as jnp
from jax import lax
from jax.experimental import pallas as pl
from jax.experimental.pallas import tpu as pltpu

F32, BF16, I32 = jnp.float32, jnp.bfloat16, jnp.int32

EPS = 1e-6
ROPE_THETA = 10000.0
LRU_C = 8.0
LRU_WIDTH = 1024
LRU_BLOCKS = 8
ATT_HEADS = 8
ATT_HEAD_DIM = 128
IDX_HEADS = 16
IDX_HEAD_DIM = 64
TOPK_MAX = 256
MLA_HEADS = 8
MLA_Q_LORA = 768
MLA_KV_LORA = 512
MLA_NOPE = 128
MLA_ROPE = 64
MLA_QK = MLA_NOPE + MLA_ROPE
MLA_V = 128
N_BRANCH = 3
N_EXPERTS = 8
TOP_K = 2

LANES = 128
SUBLANES = 8
VMEM_LIMIT_BYTES = 56 << 20
MOE_ROWS = 768
INT_MIN = -(2 ** 31)
NEG_INF_KEY = 0x807FFFFF - 2 ** 32

NT_DIMS = (((1,), (1,)), ((), ()))
LOG2_E = 1.4426950408889634
QK_LOOKAHEAD = 4


def _params(sem):
    return pltpu.CompilerParams(dimension_semantics=sem, vmem_limit_bytes=VMEM_LIMIT_BYTES)


def _tile(n, cap, unit=LANES):
    if n <= cap:
        return n
    best = None
    for t in range(unit, cap + 1, unit):
        if n % t == 0:
            best = t
    assert best is not None, (n, cap)
    return best


def _rmsnorm_kernel(x_ref, g_ref, o_ref):
    x = x_ref[...]
    ms = jnp.mean(x * x, axis=-1, keepdims=True)
    o_ref[...] = (x * lax.rsqrt(ms + EPS) * g_ref[...]).astype(o_ref.dtype)


def _rmsnorm(x, g):
    m, d = x.shape
    tm = _tile(m, 512, SUBLANES)
    return pl.pallas_call(
        _rmsnorm_kernel,
        grid=(m // tm,),
        in_specs=[pl.BlockSpec((tm, d), lambda i: (i, 0)), pl.BlockSpec((1, d), lambda i: (0, 0))],
        out_specs=pl.BlockSpec((tm, d), lambda i: (i, 0)),
        out_shape=jax.ShapeDtypeStruct((m, d), BF16),
        compiler_params=_params(("parallel",)),
        name="rmsnorm",
    )(x, g.reshape(1, d))


def _mm_kernel(x_ref, w_ref, o_ref):
    o_ref[...] = jnp.dot(x_ref[...], w_ref[...], preferred_element_type=F32).astype(o_ref.dtype)


def _mm_nt_kernel(x_ref, wt_ref, o_ref):
    o_ref[...] = lax.dot_general(x_ref[...], wt_ref[...], NT_DIMS,
                                 preferred_element_type=F32).astype(o_ref.dtype)


def _mm_res_kernel(x_ref, w_ref, r_ref, o_ref):
    o_ref[...] = r_ref[...] + jnp.dot(x_ref[...], w_ref[...], preferred_element_type=F32)


def _matmul(x, w, out_dtype, res=None, tm_cap=1024, tn_cap=1024, transposed=False, name="matmul"):
    m, k = x.shape
    n = w.shape[0] if transposed else w.shape[1]
    tm = _tile(m, tm_cap, SUBLANES)
    tn = _tile(n, tn_cap)
    if transposed:
        w_spec = pl.BlockSpec((tn, k), lambda i, j: (j, 0))
    else:
        w_spec = pl.BlockSpec((k, tn), lambda i, j: (0, j))
    in_specs = [pl.BlockSpec((tm, k), lambda i, j: (i, 0)), w_spec]
    args = [x, w]
    body = _mm_nt_kernel if transposed else _mm_kernel
    if res is not None:
        assert not transposed
        in_specs.append(pl.BlockSpec((tm, tn), lambda i, j: (i, j)))
        args.append(res)
        body = _mm_res_kernel
    return pl.pallas_call(
        body,
        grid=(m // tm, n // tn),
        in_specs=in_specs,
        out_specs=pl.BlockSpec((tm, tn), lambda i, j: (i, j)),
        out_shape=jax.ShapeDtypeStruct((m, n), out_dtype),
        compiler_params=_params(("parallel", "arbitrary")),
        name=name,
    )(*args)


def _store_heads_t(o_ref, rt):
    n_heads, dv = o_ref.shape[1], o_ref.shape[3]
    for h in range(n_heads):
        o_ref[0, h, 0] = rt[h * dv:(h + 1) * dv, :]


def _values_t_kernel(x_ref, wt_ref, o_ref):
    _store_heads_t(o_ref, lax.dot_general(wt_ref[...], x_ref[...], NT_DIMS,
                                          preferred_element_type=F32).astype(o_ref.dtype))


def _values_t(x, wt, b, t, n_heads):
    k = x.shape[1]
    dv = wt.shape[0] // n_heads
    tk = _tile(t, 256, LANES)
    nk = t // tk
    return pl.pallas_call(
        _values_t_kernel,
        grid=(b, nk),
        in_specs=[pl.BlockSpec((tk, k), lambda bi, ci: (bi * nk + ci, 0)),
                  pl.BlockSpec(wt.shape, lambda bi, ci: (0, 0))],
        out_specs=pl.BlockSpec((1, n_heads, 1, dv, tk), lambda bi, ci: (bi, 0, ci, 0, 0)),
        out_shape=jax.ShapeDtypeStruct((b, n_heads, nk, dv, tk), BF16),
        compiler_params=_params(("parallel", "parallel")),
        name="in_proj_v",
    )(x, wt)


EXPM1_SERIES_BOUND = 0.25
EXPM1_SERIES_TERMS = 10


def _expm1(y):
    poly = jnp.full_like(y, 1.0 / 3628800.0)
    fact = 3628800.0
    for n in range(EXPM1_SERIES_TERMS, 1, -1):
        fact /= n
        poly = poly * y + 1.0 / fact
    return jnp.where(jnp.abs(y) < EXPM1_SERIES_BOUND, poly * y, jnp.exp(y) - 1.0)


def _rglru_kernel(x_ref, g_ref, cw_ref, cb_ref, wa_ref, ba_ref, wx_ref, bx_ref, lam_ref, o_ref,
                  xs_scr, a_scr, b_scr, h_scr, *, tt):
    c = x_ref.shape[1]
    t = pl.program_id(1)

    @pl.when(t == 0)
    def _():
        xs_scr[0:SUBLANES, :] = jnp.zeros((SUBLANES, c), F32)
        h_scr[...] = jnp.zeros_like(h_scr)

    x = x_ref[...]
    xs_scr[SUBLANES:SUBLANES + tt, :] = x
    cw = cw_ref[...]
    xc = (xs_scr[SUBLANES - 3:SUBLANES - 3 + tt, :] * cw[0:1, :]
          + xs_scr[SUBLANES - 2:SUBLANES - 2 + tt, :] * cw[1:2, :]
          + xs_scr[SUBLANES - 1:SUBLANES - 1 + tt, :] * cw[2:3, :]
          + x * cw[3:4, :]) + cb_ref[...]
    xs_scr[0:SUBLANES, :] = x[tt - SUBLANES:tt, :]

    xcb = xc.astype(BF16)
    bw = c // LRU_BLOCKS
    ra = jnp.concatenate(
        [jnp.dot(xcb[:, n * bw:(n + 1) * bw], wa_ref[n], preferred_element_type=F32)
         for n in range(LRU_BLOCKS)], axis=1) + ba_ref[...]
    rx = jnp.concatenate(
        [jnp.dot(xcb[:, n * bw:(n + 1) * bw], wx_ref[n], preferred_element_type=F32)
         for n in range(LRU_BLOCKS)], axis=1) + bx_ref[...]
    r = jax.nn.sigmoid(ra)
    gi = jax.nn.sigmoid(rx)
    nlam = -lam_ref[...]
    softplus = jnp.maximum(nlam, 0.0) + jnp.log1p(jnp.exp(-jnp.abs(nlam)))
    log_a = (-LRU_C) * r * softplus
    a_scr[...] = jnp.exp(log_a)
    b_scr[...] = jnp.sqrt(-_expm1(2.0 * log_a)) * (gi * xc)

    row = lax.broadcasted_iota(I32, (SUBLANES, c), 0)

    def group(gidx, h):
        off = pl.multiple_of(gidx * SUBLANES, SUBLANES)
        a8 = a_scr[pl.ds(off, SUBLANES), :]
        b8 = b_scr[pl.ds(off, SUBLANES), :]
        for s in (1, 2, 4):
            keep = row >= s
            a_sh = jnp.where(keep, pltpu.roll(a8, s, 0), 1.0)
            b_sh = jnp.where(keep, pltpu.roll(b8, s, 0), 0.0)
            b8 = a8 * b_sh + b8
            a8 = a8 * a_sh
        h8 = a8 * h + b8
        b_scr[pl.ds(off, SUBLANES), :] = h8
        return h8[SUBLANES - 1:SUBLANES, :]

    h_scr[...] = lax.fori_loop(0, tt // SUBLANES, group, h_scr[...])
    o_ref[...] = (b_scr[...] * jax.nn.gelu(g_ref[...], approximate=True)).astype(o_ref.dtype)


def _rglru(proj, b, t, conv_w, conv_b, wa, ba, wx, bx, lam):
    c = LRU_WIDTH
    tt = _tile(t, 256, SUBLANES)
    nt = t // tt
    row = lambda v: v.reshape(1, c)
    wspec = pl.BlockSpec(wa.shape, lambda bi, ti: (0, 0, 0))
    vspec = pl.BlockSpec((1, c), lambda bi, ti: (0, 0))
    return pl.pallas_call(
        functools.partial(_rglru_kernel, tt=tt),
        grid=(b, nt),
        in_specs=[pl.BlockSpec((tt, c), lambda bi, ti: (bi * nt + ti, 0)),
                  pl.BlockSpec((tt, c), lambda bi, ti: (bi * nt + ti, 1)),
                  pl.BlockSpec(conv_w.shape, lambda bi, ti: (0, 0)), vspec,
                  wspec, vspec, wspec, vspec, vspec],
        out_specs=pl.BlockSpec((tt, c), lambda bi, ti: (bi * nt + ti, 0)),
        out_shape=jax.ShapeDtypeStruct((b * t, c), BF16),
        scratch_shapes=[pltpu.VMEM((tt + SUBLANES, c), F32), pltpu.VMEM((tt, c), F32),
                        pltpu.VMEM((tt, c), F32), pltpu.VMEM((1, c), F32)],
        compiler_params=_params(("parallel", "arbitrary")),
        name="rglru",
    )(proj, proj, conv_w, row(conv_b), wa.astype(BF16), row(ba), wx.astype(BF16), row(bx), row(lam))


def _inv_freq_lanes(d):
    f = ROPE_THETA ** (-jnp.arange(0, d, 2, dtype=F32) / d)
    return jnp.tile(jnp.concatenate([f, f]), LANES // d).reshape(1, LANES)


def _rope_tables(pos_ref, invf_ref, half):
    ang = pos_ref[...].astype(F32) * invf_ref[...]
    lane = lax.broadcasted_iota(I32, ang.shape, 1)
    first = (lane & (2 * half - 1)) < half
    return jnp.cos(ang), jnp.where(first, -jnp.sin(ang), jnp.sin(ang)), first


def _swap_halves(x, half, first):
    if 2 * half == LANES:
        return pltpu.roll(x, half, 1)
    return jnp.where(first, pltpu.roll(x, LANES - half, 1), pltpu.roll(x, half, 1))


def _qk_prep_kernel(x_ref, pos_ref, invf_ref, gq_ref, gk_ref, q_ref, k_ref):
    cosf, sinf, first = _rope_tables(pos_ref, invf_ref, ATT_HEAD_DIM // 2)
    for which, g_ref, o_ref in ((0, gq_ref, q_ref), (1, gk_ref, k_ref)):
        for h in range(ATT_HEADS):
            lo = (which * ATT_HEADS + h) * ATT_HEAD_DIM
            s = x_ref[:, lo:lo + ATT_HEAD_DIM]
            y = s * lax.rsqrt(jnp.mean(s * s, axis=-1, keepdims=True) + EPS) * g_ref[...]
            y = y * cosf + _swap_halves(y, ATT_HEAD_DIM // 2, first) * sinf
            o_ref[0, h] = y.astype(o_ref.dtype)


def _qk_prep(proj, pos, b, t, gq, gk):
    tm = _tile(t, 256, SUBLANES)
    nt = t // tm
    hd = ATT_HEADS * ATT_HEAD_DIM
    ospec = pl.BlockSpec((1, ATT_HEADS, tm, ATT_HEAD_DIM), lambda bi, ti: (bi, 0, ti, 0))
    oshape = jax.ShapeDtypeStruct((b, ATT_HEADS, t, ATT_HEAD_DIM), BF16)
    vspec = pl.BlockSpec((1, LANES), lambda bi, ti: (0, 0))
    return pl.pallas_call(
        _qk_prep_kernel,
        grid=(b, nt),
        in_specs=[pl.BlockSpec((tm, 2 * hd), lambda bi, ti: (bi * nt + ti, 1)),
                  pl.BlockSpec((tm, 1), lambda bi, ti: (bi * nt + ti, 0)), vspec, vspec, vspec],
        out_specs=[ospec, ospec],
        out_shape=[oshape, oshape],
        compiler_params=_params(("parallel", "parallel")),
        name="qk_prep",
    )(proj, pos, _inv_freq_lanes(ATT_HEAD_DIM), gq.reshape(1, LANES), gk.reshape(1, LANES))


def _idx_prep_kernel(x_ref, pos_ref, invf_ref, qi_ref, ki_ref):
    half = IDX_HEAD_DIM // 2
    cosf, sinf, first = _rope_tables(pos_ref, invf_ref, half)
    lane = lax.broadcasted_iota(I32, cosf.shape, 1)
    left = lane < IDX_HEAD_DIM

    def rope(x):
        return x * cosf + _swap_halves(x, half, first) * sinf

    def split(y):
        hi = y.astype(BF16).astype(F32)
        return hi, y - hi

    for j in range(IDX_HEADS // 2):
        hi, lo = split(rope(x_ref[:, j * LANES:(j + 1) * LANES]))
        even = jnp.where(left, hi, pltpu.roll(lo, IDX_HEAD_DIM, 1)).astype(BF16)
        odd = jnp.where(left, pltpu.roll(hi, IDX_HEAD_DIM, 1), lo).astype(BF16)
        for h, v in ((2 * j, even), (2 * j + 1, odd)):
            qi_ref[0, h, :, 0:LANES] = v
            qi_ref[0, h, :, LANES:2 * LANES] = v
    kcol = IDX_HEADS * IDX_HEAD_DIM
    khi, klo = split(rope(x_ref[:, kcol:kcol + LANES]))
    ki_ref[0, :, 0:LANES] = jnp.where(left, khi, pltpu.roll(khi, IDX_HEAD_DIM, 1)).astype(BF16)
    ki_ref[0, :, LANES:2 * LANES] = jnp.where(left, klo, pltpu.roll(klo, IDX_HEAD_DIM, 1)).astype(BF16)


def _idx_prep(idx, pos, b, t):
    tm = _tile(t, 256, SUBLANES)
    nt = t // tm
    w = idx.shape[1]
    return pl.pallas_call(
        _idx_prep_kernel,
        grid=(b, nt),
        in_specs=[pl.BlockSpec((tm, w), lambda bi, ti: (bi * nt + ti, 0)),
                  pl.BlockSpec((tm, 1), lambda bi, ti: (bi * nt + ti, 0)),
                  pl.BlockSpec((1, LANES), lambda bi, ti: (0, 0))],
        out_specs=[pl.BlockSpec((1, IDX_HEADS, tm, 2 * LANES), lambda bi, ti: (bi, 0, ti, 0)),
                   pl.BlockSpec((1, tm, 2 * LANES), lambda bi, ti: (bi, ti, 0))],
        out_shape=[jax.ShapeDtypeStruct((b, IDX_HEADS, t, 2 * LANES), BF16),
                   jax.ShapeDtypeStruct((b, t, 2 * LANES), BF16)],
        compiler_params=_params(("parallel", "parallel")),
        name="idx_prep",
    )(idx, pos, _inv_freq_lanes(IDX_HEAD_DIM))


def _mla_prep_kernel(m_ref, pos_ref, invf_ref, qa_ref, kva_ref, wuq_ref, wuk_ref, wuvt_ref,
                     qnn_ref, qnr_ref, knn_ref, knr_ref, q_ref, k_ref, vt_ref):
    half = MLA_ROPE // 2
    cosf, sinf, first = _rope_tables(pos_ref, invf_ref, half)
    lane = lax.broadcasted_iota(I32, cosf.shape, 1)
    left = lane < MLA_ROPE

    def rope(x):
        return x * cosf + _swap_halves(x, half, first) * sinf

    def norm(x, g_ref):
        return (x * lax.rsqrt(jnp.mean(x * x, axis=-1, keepdims=True) + EPS) * g_ref[...]).astype(BF16)

    cq = norm(m_ref[:, 0:MLA_Q_LORA], qa_ref)
    ckv = norm(m_ref[:, MLA_Q_LORA:MLA_Q_LORA + MLA_KV_LORA], kva_ref)
    kr = m_ref[:, MLA_Q_LORA + MLA_KV_LORA:MLA_Q_LORA + MLA_KV_LORA + LANES]
    qf = jnp.dot(cq, wuq_ref[...], preferred_element_type=F32)
    kvf = jnp.dot(ckv, wuk_ref[...], preferred_element_type=F32)
    nope_w = MLA_HEADS * MLA_NOPE
    _store_heads_t(vt_ref, lax.dot_general(wuvt_ref[...], ckv, NT_DIMS,
                                           preferred_element_type=F32).astype(vt_ref.dtype))

    for j in range(MLA_HEADS // 2):
        rs = qf[:, nope_w + j * LANES:nope_w + (j + 1) * LANES]
        sq = rs * rs
        ss_pair = (jnp.sum(jnp.where(left, sq, 0.0), axis=-1, keepdims=True),
                   jnp.sum(jnp.where(left, 0.0, sq), axis=-1, keepdims=True))
        for par in range(2):
            h = 2 * j + par
            nope = qf[:, h * MLA_NOPE:(h + 1) * MLA_NOPE]
            ms = (jnp.sum(nope * nope, axis=-1, keepdims=True) + ss_pair[par]) * (1.0 / MLA_QK)
            rsq = lax.rsqrt(ms + EPS)
            q_ref[0, h, :, 0:LANES] = (nope * rsq * qnn_ref[...]).astype(q_ref.dtype)
            rr = rope(rs * rsq * qnr_ref[...])
            if par == 1:
                rr = pltpu.roll(rr, MLA_ROPE, 1)
            q_ref[0, h, :, LANES:2 * LANES] = jnp.where(left, rr, 0.0).astype(q_ref.dtype)

    ss_kr = jnp.sum(jnp.where(left, kr * kr, 0.0), axis=-1, keepdims=True)
    base = jnp.where(left, rope(kr * knr_ref[...]), 0.0)
    for h in range(MLA_HEADS):
        nope = kvf[:, h * MLA_NOPE:(h + 1) * MLA_NOPE]
        ms = (jnp.sum(nope * nope, axis=-1, keepdims=True) + ss_kr) * (1.0 / MLA_QK)
        rsq = lax.rsqrt(ms + EPS)
        k_ref[0, h, :, 0:LANES] = (nope * rsq * knn_ref[...]).astype(k_ref.dtype)
        k_ref[0, h, :, LANES:2 * LANES] = (base * rsq).astype(k_ref.dtype)


def _mla_prep(mla, pos, b, t, qa, kva, w_uq, w_ukv, qn, kn):
    tm = _tile(t, 256, SUBLANES)
    nt = t // tm
    wq = w_uq.reshape(MLA_Q_LORA, MLA_HEADS, MLA_QK)
    wq = jnp.concatenate([wq[:, :, :MLA_NOPE].reshape(MLA_Q_LORA, -1),
                          wq[:, :, MLA_NOPE:].reshape(MLA_Q_LORA, -1)], axis=1).astype(BF16)
    wkv = w_ukv.reshape(MLA_KV_LORA, MLA_HEADS, MLA_NOPE + MLA_V)
    wk = wkv[:, :, :MLA_NOPE].reshape(MLA_KV_LORA, -1).astype(BF16)
    wvt = wkv[:, :, MLA_NOPE:].reshape(MLA_KV_LORA, -1).T.astype(BF16)
    dup = lambda g: jnp.tile(g[MLA_NOPE:], 2).reshape(1, LANES)
    full = lambda a: pl.BlockSpec(a.shape, lambda bi, ti: (0,) * a.ndim)
    consts = [_inv_freq_lanes(MLA_ROPE), qa.reshape(1, -1), kva.reshape(1, -1), wq, wk, wvt,
              qn[:MLA_NOPE].reshape(1, LANES), dup(qn), kn[:MLA_NOPE].reshape(1, LANES), dup(kn)]
    hspec = pl.BlockSpec((1, MLA_HEADS, tm, 2 * LANES), lambda bi, ti: (bi, 0, ti, 0))
    hshape = jax.ShapeDtypeStruct((b, MLA_HEADS, t, 2 * LANES), BF16)
    return pl.pallas_call(
        _mla_prep_kernel,
        grid=(b, nt),
        in_specs=[pl.BlockSpec((tm, mla.shape[1]), lambda bi, ti: (bi * nt + ti, 0)),
                  pl.BlockSpec((tm, 1), lambda bi, ti: (bi * nt + ti, 0))] + [full(a) for a in consts],
        out_specs=[hspec, hspec,
                   pl.BlockSpec((1, MLA_HEADS, 1, MLA_V, tm), lambda bi, ti: (bi, 0, ti, 0, 0))],
        out_shape=[hshape, hshape, jax.ShapeDtypeStruct((b, MLA_HEADS, nt, MLA_V, tm), BF16)],
        compiler_params=_params(("parallel", "parallel")),
        name="mla_prep",
    )(mla, pos, *consts)


def _attn_kernel(*refs, sparse, n_sel, tq, scale, idx_scale, t_total):
    if sparse:
        (q_ref, k_ref, vt_ref, qi_ref, ki_ref, wt_ref, o_ref, m_scr, l_scr, acc_scr,
         score_scr, bias_scr, thr_scr, tie_scr) = refs
    else:
        q_ref, k_ref, vt_ref, o_ref, m_scr, l_scr, acc_scr = refs
    n_heads, dv = vt_ref.shape[1], vt_ref.shape[3]
    qb = pl.program_id(1)
    nkc = qb + 1
    kio = lax.broadcasted_iota(I32, (tq, tq), 0)
    qio = lax.broadcasted_iota(I32, (tq, tq), 1)
    neg_inf = -jnp.inf

    def chunk_off(c):
        return pl.multiple_of(c * tq, tq)

    def causal(c):
        return (c * tq + kio) <= (qb * tq + qio)

    if sparse:
        n_idx_heads = qi_ref.shape[1]

        def score_chunk(c, carry):
            koff = chunk_off(c)
            ki = ki_ref[0, pl.ds(koff, tq), :]
            acc = jnp.zeros((tq, tq), F32)
            for h in range(n_idx_heads):
                s = lax.dot_general(ki, qi_ref[0, h], NT_DIMS, preferred_element_type=F32)
                acc = acc + jnp.maximum(s, 0.0) * wt_ref[0, h:h + 1, :]
            score_scr[pl.ds(koff, tq), :] = jnp.where(causal(c), acc * idx_scale, neg_inf)
            return carry

        lax.fori_loop(0, nkc, score_chunk, 0)

        def key_to_float(key):
            val = pltpu.bitcast(jnp.where(key < 0, key ^ 0x7FFFFFFF, key), F32)
            return jnp.where(key < NEG_INF_KEY, neg_inf, val)

        def count(pred):
            def hits(c):
                x = score_scr[pl.ds(chunk_off(c), tq), :]
                m = jnp.where(pred(x, c), 1.0, 0.0)
                return jnp.sum(m.reshape(tq // SUBLANES, SUBLANES, tq), axis=0)

            def body(pair, acc):
                c1 = 2 * pair
                has_second = c1 + 1 < nkc
                second = hits(jnp.where(has_second, c1 + 1, c1))
                return acc + hits(c1) + jnp.where(has_second, 1.0, 0.0) * second
            acc = lax.fori_loop(0, (nkc + 1) // 2, body, jnp.zeros((SUBLANES, tq), F32))
            return jnp.sum(acc, axis=0, keepdims=True)

        def count_ge(key):
            cand = key_to_float(key)
            return count(lambda x, c: x >= cand)

        k_sel = jnp.float32(n_sel)
        zero = jnp.zeros((1, tq), I32)
        thr_key = jnp.where(count_ge(zero) >= k_sel, zero, jnp.full((1, tq), INT_MIN, I32))

        def bit_step(i, key):
            cand = key | lax.shift_left(jnp.int32(1), jnp.int32(30) - i)
            return jnp.where(count_ge(cand) >= k_sel, cand, key)

        thr_key = lax.fori_loop(0, 31, bit_step, thr_key)
        thr = key_to_float(thr_key)
        n_ge = count(lambda x, c: x >= thr)
        thr_scr[0:1, :] = thr
        tie_scr[0:1, :] = jnp.full((1, tq), t_total, I32)

        @pl.when(jnp.max(n_ge) > k_sel)
        def _():
            need = k_sel - count(lambda x, c: x > thr)

            def below(x, c, m):
                return jnp.logical_and(x == thr, (c * tq + kio) < m)

            def idx_step(i, lo):
                cand = lo | lax.shift_left(jnp.int32(1), jnp.int32(t_total.bit_length() - 1) - i)
                return jnp.where(count(lambda x, c: below(x, c, cand)) < need, cand, lo)

            lo = lax.fori_loop(0, t_total.bit_length(), idx_step, zero)
            tie_scr[0:1, :] = lo + 1

        thr = thr_scr[0:1, :]
        tie_end = tie_scr[0:1, :]

        def bias_chunk(c, carry):
            koff = chunk_off(c)
            x = score_scr[pl.ds(koff, tq), :]
            tie_ok = jnp.logical_and(x == thr, (c * tq + kio) < tie_end)
            keep = jnp.logical_and(jnp.logical_or(x > thr, tie_ok), causal(c))
            bias_scr[pl.ds(koff, tq), :] = jnp.where(keep, 0.0, neg_inf)
            return carry

        lax.fori_loop(0, nkc, bias_chunk, 0)

    m_scr[...] = jnp.full(m_scr.shape, neg_inf, F32)
    l_scr[...] = jnp.zeros(l_scr.shape, F32)
    acc_scr[...] = jnp.zeros(acc_scr.shape, F32)

    def chunk_step(c, masked):
        koff = chunk_off(c)
        if sparse:
            bias = bias_scr[pl.ds(koff, tq), :]
        elif masked:
            allowed = causal(c)
        def qk(h):
            return lax.dot_general(k_ref[0, h, pl.ds(koff, tq), :], q_ref[0, h], NT_DIMS,
                                   preferred_element_type=F32)

        queued = [qk(h) for h in range(QK_LOOKAHEAD)]
        for h in range(n_heads):
            s = queued.pop(0) * (scale * LOG2_E)
            if h + QK_LOOKAHEAD < n_heads:
                queued.append(qk(h + QK_LOOKAHEAD))
            if sparse:
                s = s + bias
            elif masked:
                s = jnp.where(allowed, s, neg_inf)
            m_old = m_scr[h]
            m_new = jnp.maximum(m_old, jnp.max(s, axis=0, keepdims=True))
            m_ref = jnp.where(m_new == neg_inf, 0.0, m_new)
            alpha = jnp.exp2(m_old - m_ref)
            p = jnp.exp2(s - m_ref)
            l_scr[h] = alpha * l_scr[h] + jnp.sum(p, axis=0, keepdims=True)
            acc_scr[h] = alpha * acc_scr[h] + jnp.dot(vt_ref[0, h, c], p.astype(BF16),
                                                      preferred_element_type=F32)
            m_scr[h] = m_new

    def off_diagonal(c, carry):
        chunk_step(c, False)
        return carry

    if sparse:
        lax.fori_loop(0, nkc, off_diagonal, 0)
    else:
        lax.fori_loop(0, qb, off_diagonal, 0)
        chunk_step(qb, True)
    for h in range(n_heads):
        o_ref[:, h * dv:(h + 1) * dv] = (acc_scr[h] / l_scr[h]).T.astype(o_ref.dtype)


def _attention(q, k, vt, b, t, scale, sparse_inputs=None, n_sel=0):
    h, dq = q.shape[1], q.shape[3]
    nq, dv, tq = vt.shape[2:]
    assert nq * tq == t
    in_specs = [pl.BlockSpec((1, h, tq, dq), lambda bi, qi: (bi, 0, qi, 0)),
                pl.BlockSpec((1, h, t, dq), lambda bi, qi: (bi, 0, 0, 0)),
                pl.BlockSpec((1, h, nq, dv, tq), lambda bi, qi: (bi, 0, 0, 0, 0))]
    args = [q, k, vt]
    scratch = [pltpu.VMEM((h, 1, tq), F32), pltpu.VMEM((h, 1, tq), F32), pltpu.VMEM((h, dv, tq), F32)]
    sparse = sparse_inputs is not None
    if sparse:
        qi_, ki_, wt_ = sparse_inputs
        hi, dk = qi_.shape[1], qi_.shape[3]
        in_specs += [pl.BlockSpec((1, hi, tq, dk), lambda bi, qi: (bi, 0, qi, 0)),
                     pl.BlockSpec((1, t, dk), lambda bi, qi: (bi, 0, 0)),
                     pl.BlockSpec((1, hi, tq), lambda bi, qi: (bi, 0, qi))]
        args += [qi_, ki_, wt_]
        scratch += [pltpu.VMEM((t, tq), F32), pltpu.VMEM((t, tq), F32),
                    pltpu.VMEM((SUBLANES, tq), F32), pltpu.VMEM((SUBLANES, tq), I32)]
    return pl.pallas_call(
        functools.partial(_attn_kernel, sparse=sparse, n_sel=n_sel, tq=tq, scale=scale,
                          idx_scale=IDX_HEADS ** -0.5 * IDX_HEAD_DIM ** -0.5, t_total=t),
        grid=(b, nq),
        in_specs=in_specs,
        out_specs=pl.BlockSpec((tq, h * dv), lambda bi, qi: (bi * nq + qi, 0)),
        out_shape=jax.ShapeDtypeStruct((b * t, h * dv), BF16),
        scratch_shapes=scratch,
        compiler_params=_params(("parallel", "arbitrary")),
        name="sparse_attention" if sparse else "dense_attention",
    )(*args)


def _merge_kernel(hn_ref, a_ref, b_ref, c_ref, wg_ref, wb_ref, o_ref):
    hn = hn_ref[...]
    acc = None
    for n, br_ref in enumerate((a_ref, b_ref, c_ref)):
        gate = jax.nn.sigmoid(lax.dot_general(hn, wg_ref[n], NT_DIMS, preferred_element_type=F32))
        term = gate * jnp.dot(br_ref[...], wb_ref[n], preferred_element_type=F32)
        acc = term if acc is None else acc + term
    o_ref[...] = acc.astype(o_ref.dtype)


def _merge(hn, branches, w_gate, w_branch):
    m, d = hn.shape
    bw = w_branch.shape[1]
    tm = _tile(m, 512, SUBLANES)
    tn = _tile(d, 512)
    bspec = pl.BlockSpec((tm, bw), lambda i, j: (i, 0))
    return pl.pallas_call(
        _merge_kernel,
        grid=(m // tm, d // tn),
        in_specs=[pl.BlockSpec((tm, d), lambda i, j: (i, 0)), bspec, bspec, bspec,
                  pl.BlockSpec((N_BRANCH, tn, d), lambda i, j: (0, j, 0)),
                  pl.BlockSpec((N_BRANCH, bw, tn), lambda i, j: (0, 0, j))],
        out_specs=pl.BlockSpec((tm, tn), lambda i, j: (i, j)),
        out_shape=jax.ShapeDtypeStruct((m, d), BF16),
        compiler_params=_params(("parallel", "arbitrary")),
        name="gated_merge",
    )(hn, *branches, w_gate, w_branch)


def _ple_kernel(h_ref, hn_ref, p_ref, wg_ref, wp_ref, o_ref):
    gate = jax.nn.sigmoid(jnp.dot(hn_ref[...], wg_ref[...], preferred_element_type=F32))
    emb = jnp.dot(p_ref[...].astype(BF16), wp_ref[...], preferred_element_type=F32)
    o_ref[...] = h_ref[...] + emb * gate


def _ple(h, hn, p, w_gate, w_proj):
    m, d = h.shape
    pd = p.shape[1]
    tm = _tile(m, 1024, SUBLANES)
    tn = _tile(d, 1024)
    return pl.pallas_call(
        _ple_kernel,
        grid=(m // tm, d // tn),
        in_specs=[pl.BlockSpec((tm, tn), lambda i, j: (i, j)), pl.BlockSpec((tm, d), lambda i, j: (i, 0)),
                  pl.BlockSpec((tm, pd), lambda i, j: (i, 0)), pl.BlockSpec((d, tn), lambda i, j: (0, j)),
                  pl.BlockSpec((pd, tn), lambda i, j: (0, j))],
        out_specs=pl.BlockSpec((tm, tn), lambda i, j: (i, j)),
        out_shape=jax.ShapeDtypeStruct((m, d), F32),
        compiler_params=_params(("parallel", "arbitrary")),
        name="ple",
    )(h, hn, p, w_gate, w_proj)


def _ffn_kernel(be_ref, nu_ref, x_ref, wg_ref, wu_ref, wd_ref, *rest, has_res):
    if has_res:
        res_ref, o_ref = rest
    else:
        (o_ref,) = rest
    i, j = pl.program_id(0), pl.program_id(1)

    @pl.when(j == 0)
    def _():
        o_ref[...] = res_ref[...] if has_res else jnp.zeros_like(o_ref)

    @pl.when(i < nu_ref[0])
    def _():
        x = x_ref[...].astype(BF16)
        g = jnp.dot(x, wg_ref[0].astype(BF16), preferred_element_type=F32)
        u = jnp.dot(x, wu_ref[0].astype(BF16), preferred_element_type=F32)
        act = (g * jax.nn.sigmoid(g) * u).astype(BF16)
        o_ref[...] += jnp.dot(act, wd_ref[0].astype(BF16), preferred_element_type=F32)


def _ffn(x, w_gate, w_up, w_down, blk_expert, n_used, tm, res=None):
    r, d = x.shape
    f = w_gate.shape[2]
    tf = _tile(f, 512)
    nf = f // tf
    assert nf >= 2

    def jj(i, j, nu):
        return jnp.where(i < nu[0], j, nf - 1)

    rows_mode = pl.Buffered(1) if w_gate.dtype == F32 else None
    in_specs = [pl.BlockSpec((tm, d), lambda i, j, be, nu: (i, 0), pipeline_mode=rows_mode),
                pl.BlockSpec((1, d, tf), lambda i, j, be, nu: (be[i], 0, jj(i, j, nu))),
                pl.BlockSpec((1, d, tf), lambda i, j, be, nu: (be[i], 0, jj(i, j, nu))),
                pl.BlockSpec((1, tf, d), lambda i, j, be, nu: (be[i], jj(i, j, nu), 0))]
    args = [x, w_gate, w_up, w_down]
    if res is not None:
        in_specs.append(pl.BlockSpec((tm, d), lambda i, j, be, nu: (i, 0)))
        args.append(res)
    return pl.pallas_call(
        functools.partial(_ffn_kernel, has_res=res is not None),
        grid_spec=pltpu.PrefetchScalarGridSpec(
            num_scalar_prefetch=2,
            grid=(r // tm, nf),
            in_specs=in_specs,
            out_specs=pl.BlockSpec((tm, d), lambda i, j, be, nu: (i, 0), pipeline_mode=rows_mode)),
        out_shape=jax.ShapeDtypeStruct((r, d), F32),
        compiler_params=_params(("arbitrary", "arbitrary")),
        name="swiglu",
    )(blk_expert, n_used, *args)


def _router_kernel(h_ref, g_ref, rhi_ref, rlo_ref, hn_ref, info_ref, cnt_ref, run_scr, *, n_experts):
    i = pl.program_id(0)
    tm = h_ref.shape[0]

    @pl.when(i == 0)
    def _():
        run_scr[...] = jnp.zeros_like(run_scr)

    x = h_ref[...]
    xn = x * lax.rsqrt(jnp.mean(x * x, axis=-1, keepdims=True) + EPS) * g_ref[...]
    hi = xn.astype(BF16)
    hn_ref[...] = xn
    lo = (xn - hi.astype(F32)).astype(BF16)
    logits = (jnp.dot(hi, rhi_ref[...], preferred_element_type=F32)
              + jnp.dot(hi, rlo_ref[...], preferred_element_type=F32)
              + jnp.dot(lo, rhi_ref[...], preferred_element_type=F32))
    lane = lax.broadcasted_iota(I32, logits.shape, 1)
    lane_f = lane.astype(F32)
    logits = jnp.where(lane < n_experts, logits, -jnp.inf)

    def top(vals):
        v = jnp.max(vals, axis=-1, keepdims=True)
        idx = jnp.min(jnp.where(vals == v, lane_f, float(LANES)), axis=-1, keepdims=True)
        return v, idx

    v1, i1 = top(logits)
    v2, i2 = top(jnp.where(lane_f == i1, -jnp.inf, logits))
    e2 = jnp.exp(v2 - v1)
    g1 = 1.0 / (1.0 + e2)
    g2 = e2 / (1.0 + e2)

    oh1 = lane_f == i1
    oh2 = lane_f == i2
    both = jnp.where(jnp.logical_or(oh1, oh2), 1.0, 0.0)
    r_io = lax.broadcasted_iota(I32, (tm, tm), 0)
    c_io = lax.broadcasted_iota(I32, (tm, tm), 1)
    strict_lower = jnp.where(c_io < r_io, 1.0, 0.0).astype(BF16)
    before = jnp.dot(strict_lower, both.astype(BF16), preferred_element_type=F32) + run_scr[0:1, :]
    rank1 = jnp.sum(jnp.where(oh1, before, 0.0), axis=-1, keepdims=True)
    rank2 = jnp.sum(jnp.where(oh2, before, 0.0), axis=-1, keepdims=True)
    run_scr[0:1, :] = run_scr[0:1, :] + jnp.sum(both, axis=0, keepdims=True)

    info = jnp.zeros(logits.shape, F32)
    for col, val in enumerate((i1, i2, g1, g2, rank1, rank2)):
        info = jnp.where(lane == col, val, info)
    info_ref[...] = info
    cnt_ref[...] = jnp.broadcast_to(run_scr[0:1, :], cnt_ref.shape)


def _router(h, g, router):
    m, d = h.shape
    e = router.shape[1]
    tm = _tile(m, 256, SUBLANES)
    rp = jnp.zeros((d, LANES), F32).at[:, :e].set(router)
    rhi = rp.astype(BF16)
    rlo = (rp - rhi.astype(F32)).astype(BF16)
    return pl.pallas_call(
        functools.partial(_router_kernel, n_experts=e),
        grid=(m // tm,),
        in_specs=[pl.BlockSpec((tm, d), lambda i: (i, 0)), pl.BlockSpec((1, d), lambda i: (0, 0)),
                  pl.BlockSpec((d, LANES), lambda i: (0, 0)), pl.BlockSpec((d, LANES), lambda i: (0, 0))],
        out_specs=[pl.BlockSpec((tm, d), lambda i: (i, 0)), pl.BlockSpec((tm, LANES), lambda i: (i, 0)),
                   pl.BlockSpec((SUBLANES, LANES), lambda i: (0, 0))],
        out_shape=[jax.ShapeDtypeStruct((m, d), F32), jax.ShapeDtypeStruct((m, LANES), F32),
                   jax.ShapeDtypeStruct((SUBLANES, LANES), F32)],
        scratch_shapes=[pltpu.VMEM((SUBLANES, LANES), F32)],
        compiler_params=_params(("arbitrary",)),
        name="router",
    )(h, g.reshape(1, d), rhi, rlo)


def _row_copy(src, dst, sem, s, d):
    return pltpu.make_async_copy(src.at[pl.ds(s, 1), :], dst.at[pl.ds(d, 1), :], sem)


def _dispatch_kernel(dest_ref, x_ref, xs_in_hbm, xs_hbm, sem, *, rows):
    del xs_in_hbm

    def issue(r, carry):
        for kk in range(TOP_K):
            _row_copy(x_ref, xs_hbm, sem, r, dest_ref[0, 0, TOP_K * r + kk]).start(priority=kk % 2)
        return carry

    lax.fori_loop(0, rows, issue, 0, unroll=8)
    for kk in range(TOP_K):
        pltpu.make_async_copy(x_ref, xs_hbm.at[pl.ds(0, rows), :], sem).wait()


def _dispatch(x, dest, n_rows):
    m, d = x.shape
    rows = _tile(m, 512, SUBLANES)
    return pl.pallas_call(
        functools.partial(_dispatch_kernel, rows=rows),
        grid=(m // rows,),
        in_specs=[pl.BlockSpec((1, 1, TOP_K * rows), lambda i: (i, 0, 0), memory_space=pltpu.SMEM),
                  pl.BlockSpec((rows, d), lambda i: (i, 0)), pl.BlockSpec(memory_space=pl.ANY)],
        out_specs=pl.BlockSpec(memory_space=pl.ANY),
        out_shape=jax.ShapeDtypeStruct((n_rows, d), x.dtype),
        scratch_shapes=[pltpu.SemaphoreType.DMA(())],
        input_output_aliases={2: 0},
        compiler_params=_params(("arbitrary",)),
        name="moe_dispatch",
    )(dest.reshape(m // rows, 1, TOP_K * rows), x, jnp.zeros((n_rows, d), x.dtype))


def _combine_kernel(dest_ref, h_ref, info_ref, g_ref, ys_hbm, o_ref, on_ref, buf, sem, *, rows):
    def issue(r, carry):
        for kk in range(TOP_K):
            pltpu.make_async_copy(ys_hbm.at[pl.ds(dest_ref[0, 0, TOP_K * r + kk], 1), :],
                                  buf.at[kk, pl.ds(r, 1), :], sem).start(priority=kk % 2)
        return carry

    lax.fori_loop(0, rows, issue, 0, unroll=8)
    for kk in range(TOP_K):
        pltpu.make_async_copy(ys_hbm.at[pl.ds(0, rows), :], buf.at[kk], sem).wait()
    info = info_ref[...]
    out = h_ref[...] + (info[:, 2:3] * buf[0] + info[:, 3:4] * buf[1])
    o_ref[...] = out
    on_ref[...] = (out * lax.rsqrt(jnp.mean(out * out, axis=-1, keepdims=True) + EPS)
                   * g_ref[...]).astype(on_ref.dtype)


def _combine(h, info, dest, ys, g):
    m, d = h.shape
    rows = _tile(m, 512, SUBLANES)
    return pl.pallas_call(
        functools.partial(_combine_kernel, rows=rows),
        grid=(m // rows,),
        in_specs=[pl.BlockSpec((1, 1, TOP_K * rows), lambda i: (i, 0, 0), memory_space=pltpu.SMEM),
                  pl.BlockSpec((rows, d), lambda i: (i, 0)), pl.BlockSpec((rows, LANES), lambda i: (i, 0)),
                  pl.BlockSpec((1, d), lambda i: (0, 0)), pl.BlockSpec(memory_space=pl.ANY)],
        out_specs=[pl.BlockSpec((rows, d), lambda i: (i, 0)), pl.BlockSpec((rows, d), lambda i: (i, 0))],
        out_shape=[jax.ShapeDtypeStruct((m, d), F32), jax.ShapeDtypeStruct((m, d), BF16)],
        scratch_shapes=[pltpu.VMEM((TOP_K, rows, d), F32), pltpu.SemaphoreType.DMA(())],
        compiler_params=_params(("arbitrary",)),
        name="moe_combine",
    )(dest.reshape(m // rows, 1, TOP_K * rows), h, info, g.reshape(1, d), ys)


def _moe(h, g_ffn, router, w_gate, w_up, w_down, g_next):
    m, d = h.shape
    e = router.shape[1]
    hn, info, cnt = _router(h, g_ffn, router)
    counts = cnt[0, :e].astype(I32)
    padded = (counts + MOE_ROWS - 1) // MOE_ROWS * MOE_ROWS
    pad_ends = jnp.cumsum(padded)
    pad_starts = pad_ends - padded
    n_blocks = -(-(m * TOP_K) // MOE_ROWS) + e
    ids = info[:, 0:TOP_K].astype(I32)
    dest = jnp.sum(jnp.where(ids[:, :, None] == jnp.arange(e)[None, None, :], pad_starts[None, None, :], 0),
                   axis=-1) + info[:, 4:4 + TOP_K].astype(I32)
    blk_start = jnp.arange(n_blocks, dtype=I32) * MOE_ROWS
    blk_expert = jnp.minimum(jnp.sum(blk_start[:, None] >= pad_ends[None, :], axis=1), e - 1).astype(I32)
    n_used = (pad_ends[e - 1] // MOE_ROWS).astype(I32).reshape(1)
    xs = _dispatch(hn, dest, n_blocks * MOE_ROWS)
    ys = _ffn(xs, w_gate, w_up, w_down, blk_expert, n_used, MOE_ROWS)
    return _combine(h, info, dest, ys, g_next)


N_MAIN = 2 * LRU_WIDTH + 2 * ATT_HEADS * ATT_HEAD_DIM
O_V = N_MAIN
O_IDX = O_V + ATT_HEADS * ATT_HEAD_DIM
N_IDX = IDX_HEADS * IDX_HEAD_DIM + IDX_HEAD_DIM + IDX_HEADS
O_MLA = O_IDX + N_IDX
N_MLA = MLA_Q_LORA + MLA_KV_LORA + MLA_ROPE
O_GATE = O_MLA + N_MLA


def _split_w_in(w):
    d = w.shape[0]
    wt = jnp.swapaxes(w, 0, 1)

    def rows(lo, hi):
        return jnp.pad(wt[lo:hi], ((0, -(hi - lo) % LANES), (0, 0))).astype(BF16)

    gates = rows(O_GATE, O_GATE + N_BRANCH * d).reshape(N_BRANCH, d, d)
    return rows(0, O_V), rows(O_V, O_IDX), rows(O_IDX, O_MLA), rows(O_MLA, O_GATE), gates


def kernel(x, p, positions, ln_mix, w_in, conv_w, conv_b, lru_wa, lru_ba, lru_wx, lru_bx, lru_lambda,
           att_q_norm, att_k_norm, mla_qa_norm, mla_kva_norm, mla_w_uq, mla_w_ukv, mla_q_norm,
           mla_k_norm, w_branch, w_out, ln_ffn, dense_w_gate, dense_w_up, dense_w_down, moe_router,
           moe_w_gate, moe_w_up, moe_w_down, ple_norm, ple_w_gate, ple_w_proj):
    b, t, d = x.shape
    m = b * t
    depth = w_in.shape[0]
    n_sel = min(TOPK_MAX, t // 4)
    pos = positions.reshape(m, 1).astype(I32)
    h = x.reshape(m, d)
    idx_w_off = IDX_HEADS * IDX_HEAD_DIM + IDX_HEAD_DIM
    for i in range(depth):
        w_main, w_v, w_idx, w_mla, w_gates = _split_w_in(w_in[i])
        hn = _rmsnorm(h, ln_mix[i])
        proj = _matmul(hn, w_main, F32, transposed=True, name="in_proj_main")
        vt = _values_t(hn, w_v, b, t, ATT_HEADS)
        idx = _matmul(hn, w_idx, F32, tn_cap=w_idx.shape[0], transposed=True, name="in_proj_idx")
        mla = _matmul(hn, w_mla, F32, tn_cap=w_mla.shape[0], transposed=True, name="in_proj_mla")

        out_a = _rglru(proj, b, t, conv_w[i], conv_b[i], lru_wa[i], lru_ba[i], lru_wx[i], lru_bx[i],
                       lru_lambda[i])

        q, k = _qk_prep(proj, pos, b, t, att_q_norm[i], att_k_norm[i])
        qi, ki = _idx_prep(idx, pos, b, t)
        wt = idx[:, idx_w_off:idx_w_off + IDX_HEADS].reshape(b, t, IDX_HEADS).transpose(0, 2, 1)
        out_b = _attention(q, k, vt, b, t, ATT_HEAD_DIM ** -0.5, sparse_inputs=(qi, ki, wt), n_sel=n_sel)

        mq, mk, mv = _mla_prep(mla, pos, b, t, mla_qa_norm[i], mla_kva_norm[i], mla_w_uq[i],
                               mla_w_ukv[i], mla_q_norm[i], mla_k_norm[i])
        out_c = _attention(mq, mk, mv, b, t, MLA_QK ** -0.5)

        merged = _merge(hn, (out_a, out_b, out_c), w_gates, w_branch[i].astype(BF16))
        h = _matmul(merged, w_out[i].astype(BF16), F32, res=h, name="out_proj")

        if i % 2 == 0:
            j = i // 2
            hn2 = _rmsnorm(h, ln_ffn[i])
            tm = _tile(m, 512, SUBLANES)
            h = _ffn(hn2, dense_w_gate[j:j + 1].astype(BF16), dense_w_up[j:j + 1].astype(BF16),
                     dense_w_down[j:j + 1].astype(BF16), jnp.zeros((m // tm,), I32),
                     jnp.full((1,), m // tm, I32), tm, res=h)
            hn3 = _rmsnorm(h, ple_norm[i])
        else:
            j = i // 2
            h, hn3 = _moe(h, ln_ffn[i], moe_router[j], moe_w_gate[j], moe_w_up[j], moe_w_down[j],
                          ple_norm[i])
        h = _ple(h, hn3, p[i].reshape(m, -1), ple_w_gate[i].astype(BF16), ple_w_proj[i].astype(BF16))
    return h.reshape(b, t, d)
```

```python
import functools

import jax
import jax.numpy as jnp
from jax import lax
from jax.experimental import pallas as pl
from jax.experimental.pallas import tpu as pltpu

F32, BF16, I32 = jnp.float32, jnp.bfloat16, jnp.int32

EPS = 1e-6
ROPE_THETA = 10000.0
LRU_C = 8.0
LRU_WIDTH = 1024
LRU_BLOCKS = 8
ATT_HEADS = 8
ATT_HEAD_DIM = 128
IDX_HEADS = 16
IDX_HEAD_DIM = 64
TOPK_MAX = 256
MLA_HEADS = 8
MLA_Q_LORA = 768
MLA_KV_LORA = 512
MLA_NOPE = 128
MLA_ROPE = 64
MLA_QK = MLA_NOPE + MLA_ROPE
MLA_V = 128
N_BRANCH = 3
N_EXPERTS = 8
TOP_K = 2

LANES = 128
SUBLANES = 8
VMEM_LIMIT_BYTES = 56 << 20
MOE_ROWS = 768
INT_MIN = -(2 ** 31)
NEG_INF_KEY = 0x807FFFFF - 2 ** 32

NT_DIMS = (((1,), (1,)), ((), ()))
LOG2_E = 1.4426950408889634
QK_LOOKAHEAD = 4


def _params(sem):
    return pltpu.CompilerParams(dimension_semantics=sem, vmem_limit_bytes=VMEM_LIMIT_BYTES)


def _tile(n, cap, unit=LANES):
    if n <= cap:
        return n
    best = None
    for t in range(unit, cap + 1, unit):
        if n % t == 0:
            best = t
    assert best is not None, (n, cap)
    return best


def _rmsnorm_kernel(x_ref, g_ref, o_ref):
    x = x_ref[...]
    ms = jnp.mean(x * x, axis=-1, keepdims=True)
    o_ref[...] = (x * lax.rsqrt(ms + EPS) * g_ref[...]).astype(o_ref.dtype)


def _rmsnorm(x, g):
    m, d = x.shape
    tm = _tile(m, 512, SUBLANES)
    return pl.pallas_call(
        _rmsnorm_kernel,
        grid=(m // tm,),
        in_specs=[pl.BlockSpec((tm, d), lambda i: (i, 0)), pl.BlockSpec((1, d), lambda i: (0, 0))],
        out_specs=pl.BlockSpec((tm, d), lambda i: (i, 0)),
        out_shape=jax.ShapeDtypeStruct((m, d), BF16),
        compiler_params=_params(("parallel",)),
        name="rmsnorm",
    )(x, g.reshape(1, d))


def _mm_kernel(x_ref, w_ref, o_ref):
    o_ref[...] = jnp.dot(x_ref[...], w_ref[...], preferred_element_type=F32).astype(o_ref.dtype)


def _mm_nt_kernel(x_ref, wt_ref, o_ref):
    o_ref[...] = lax.dot_general(x_ref[...], wt_ref[...], NT_DIMS,
                                 preferred_element_type=F32).astype(o_ref.dtype)


def _mm_res_kernel(x_ref, w_ref, r_ref, o_ref):
    o_ref[...] = r_ref[...] + jnp.dot(x_ref[...], w_ref[...], preferred_element_type=F32)


def _matmul(x, w, out_dtype, res=None, tm_cap=1024, tn_cap=1024, transposed=False, name="matmul"):
    m, k = x.shape
    n = w.shape[0] if transposed else w.shape[1]
    tm = _tile(m, tm_cap, SUBLANES)
    tn = _tile(n, tn_cap)
    if transposed:
        w_spec = pl.BlockSpec((tn, k), lambda i, j: (j, 0))
    else:
        w_spec = pl.BlockSpec((k, tn), lambda i, j: (0, j))
    in_specs = [pl.BlockSpec((tm, k), lambda i, j: (i, 0)), w_spec]
    args = [x, w]
    body = _mm_nt_kernel if transposed else _mm_kernel
    if res is not None:
        assert not transposed
        in_specs.append(pl.BlockSpec((tm, tn), lambda i, j: (i, j)))
        args.append(res)
        body = _mm_res_kernel
    return pl.pallas_call(
        body,
        grid=(m // tm, n // tn),
        in_specs=in_specs,
        out_specs=pl.BlockSpec((tm, tn), lambda i, j: (i, j)),
        out_shape=jax.ShapeDtypeStruct((m, n), out_dtype),
        compiler_params=_params(("parallel", "arbitrary")),
        name=name,
    )(*args)


def _store_heads_t(o_ref, rt):
    n_heads, dv = o_ref.shape[1], o_ref.shape[3]
    for h in range(n_heads):
        o_ref[0, h, 0] = rt[h * dv:(h + 1) * dv, :]


def _values_t_kernel(x_ref, wt_ref, o_ref):
    _store_heads_t(o_ref, lax.dot_general(wt_ref[...], x_ref[...], NT_DIMS,
                                          preferred_element_type=F32).astype(o_ref.dtype))


def _values_t(x, wt, b, t, n_heads):
    k = x.shape[1]
    dv = wt.shape[0] // n_heads
    tk = _tile(t, 256, LANES)
    nk = t // tk
    return pl.pallas_call(
        _values_t_kernel,
        grid=(b, nk),
        in_specs=[pl.BlockSpec((tk, k), lambda bi, ci: (bi * nk + ci, 0)),
                  pl.BlockSpec(wt.shape, lambda bi, ci: (0, 0))],
        out_specs=pl.BlockSpec((1, n_heads, 1, dv, tk), lambda bi, ci: (bi, 0, ci, 0, 0)),
        out_shape=jax.ShapeDtypeStruct((b, n_heads, nk, dv, tk), BF16),
        compiler_params=_params(("parallel", "parallel")),
        name="in_proj_v",
    )(x, wt)


EXPM1_SERIES_BOUND = 0.25
EXPM1_SERIES_TERMS = 7


def _expm1(y):
    fact = 1.0
    for n in range(2, EXPM1_SERIES_TERMS + 1):
        fact *= n
    poly = jnp.full_like(y, 1.0 / fact)
    for n in range(EXPM1_SERIES_TERMS, 1, -1):
        fact /= n
        poly = poly * y + 1.0 / fact
    return jnp.where(jnp.abs(y) < EXPM1_SERIES_BOUND, poly * y, jnp.exp(y) - 1.0)


def _rglru_kernel(x_ref, g_ref, cw_ref, cb_ref, wa_ref, ba_ref, wx_ref, bx_ref, lam_ref, o_ref,
                  xs_scr, a_scr, b_scr, h_scr, *, tt):
    c = x_ref.shape[1]
    t = pl.program_id(1)

    @pl.when(t == 0)
    def _():
        xs_scr[0:SUBLANES, :] = jnp.zeros((SUBLANES, c), F32)
        h_scr[...] = jnp.zeros_like(h_scr)

    x = x_ref[...]
    xs_scr[SUBLANES:SUBLANES + tt, :] = x
    cw = cw_ref[...]
    xc = (xs_scr[SUBLANES - 3:SUBLANES - 3 + tt, :] * cw[0:1, :]
          + xs_scr[SUBLANES - 2:SUBLANES - 2 + tt, :] * cw[1:2, :]
          + xs_scr[SUBLANES - 1:SUBLANES - 1 + tt, :] * cw[2:3, :]
          + x * cw[3:4, :]) + cb_ref[...]
    xs_scr[0:SUBLANES, :] = x[tt - SUBLANES:tt, :]

    xcb = xc.astype(BF16)
    bw = c // LRU_BLOCKS
    ra = jnp.concatenate(
        [jnp.dot(xcb[:, n * bw:(n + 1) * bw], wa_ref[n], preferred_element_type=F32)
         for n in range(LRU_BLOCKS)], axis=1) + ba_ref[...]
    rx = jnp.concatenate(
        [jnp.dot(xcb[:, n * bw:(n + 1) * bw], wx_ref[n], preferred_element_type=F32)
         for n in range(LRU_BLOCKS)], axis=1) + bx_ref[...]
    r = jax.nn.sigmoid(ra)
    gi = jax.nn.sigmoid(rx)
    nlam = -lam_ref[...]
    softplus = jnp.maximum(nlam, 0.0) + jnp.log1p(jnp.exp(-jnp.abs(nlam)))
    log_a = (-LRU_C) * r * softplus
    a_scr[...] = jnp.exp(log_a)
    b_scr[...] = jnp.sqrt(-_expm1(2.0 * log_a)) * (gi * xc)

    row = lax.broadcasted_iota(I32, (SUBLANES, c), 0)

    def group(gidx, h):
        off = pl.multiple_of(gidx * SUBLANES, SUBLANES)
        a8 = a_scr[pl.ds(off, SUBLANES), :]
        b8 = b_scr[pl.ds(off, SUBLANES), :]
        for s in (1, 2, 4):
            keep = row >= s
            a_sh = jnp.where(keep, pltpu.roll(a8, s, 0), 1.0)
            b_sh = jnp.where(keep, pltpu.roll(b8, s, 0), 0.0)
            b8 = a8 * b_sh + b8
            a8 = a8 * a_sh
        h8 = a8 * h + b8
        b_scr[pl.ds(off, SUBLANES), :] = h8
        return h8[SUBLANES - 1:SUBLANES, :]

    h_scr[...] = lax.fori_loop(0, tt // SUBLANES, group, h_scr[...])
    o_ref[...] = (b_scr[...] * jax.nn.gelu(g_ref[...], approximate=True)).astype(o_ref.dtype)


def _rglru(proj, b, t, conv_w, conv_b, wa, ba, wx, bx, lam):
    c = LRU_WIDTH
    tt = _tile(t, 256, SUBLANES)
    nt = t // tt
    row = lambda v: v.reshape(1, c)
    wspec = pl.BlockSpec(wa.shape, lambda bi, ti: (0, 0, 0))
    vspec = pl.BlockSpec((1, c), lambda bi, ti: (0, 0))
    return pl.pallas_call(
        functools.partial(_rglru_kernel, tt=tt),
        grid=(b, nt),
        in_specs=[pl.BlockSpec((tt, c), lambda bi, ti: (bi * nt + ti, 0)),
                  pl.BlockSpec((tt, c), lambda bi, ti: (bi * nt + ti, 1)),
                  pl.BlockSpec(conv_w.shape, lambda bi, ti: (0, 0)), vspec,
                  wspec, vspec, wspec, vspec, vspec],
        out_specs=pl.BlockSpec((tt, c), lambda bi, ti: (bi * nt + ti, 0)),
        out_shape=jax.ShapeDtypeStruct((b * t, c), BF16),
        scratch_shapes=[pltpu.VMEM((tt + SUBLANES, c), F32), pltpu.VMEM((tt, c), F32),
                        pltpu.VMEM((tt, c), F32), pltpu.VMEM((1, c), F32)],
        compiler_params=_params(("parallel", "arbitrary")),
        name="rglru",
    )(proj, proj, conv_w, row(conv_b), wa.astype(BF16), row(ba), wx.astype(BF16), row(bx), row(lam))


def _inv_freq_lanes(d):
    f = ROPE_THETA ** (-jnp.arange(0, d, 2, dtype=F32) / d)
    return jnp.tile(jnp.concatenate([f, f]), LANES // d).reshape(1, LANES)


def _rope_tables(pos_ref, invf_ref, half):
    ang = pos_ref[...].astype(F32) * invf_ref[...]
    lane = lax.broadcasted_iota(I32, ang.shape, 1)
    first = (lane & (2 * half - 1)) < half
    return jnp.cos(ang), jnp.where(first, -jnp.sin(ang), jnp.sin(ang)), first


def _swap_halves(x, half, first):
    if 2 * half == LANES:
        return pltpu.roll(x, half, 1)
    return jnp.where(first, pltpu.roll(x, LANES - half, 1), pltpu.roll(x, half, 1))


def _qk_prep_kernel(x_ref, pos_ref, invf_ref, gq_ref, gk_ref, q_ref, k_ref):
    cosf, sinf, first = _rope_tables(pos_ref, invf_ref, ATT_HEAD_DIM // 2)
    for which, g_ref, o_ref in ((0, gq_ref, q_ref), (1, gk_ref, k_ref)):
        for h in range(ATT_HEADS):
            lo = (which * ATT_HEADS + h) * ATT_HEAD_DIM
            s = x_ref[:, lo:lo + ATT_HEAD_DIM]
            y = s * lax.rsqrt(jnp.mean(s * s, axis=-1, keepdims=True) + EPS) * g_ref[...]
            y = y * cosf + _swap_halves(y, ATT_HEAD_DIM // 2, first) * sinf
            o_ref[0, h] = y.astype(o_ref.dtype)


def _qk_prep(proj, pos, b, t, gq, gk):
    tm = _tile(t, 256, SUBLANES)
    nt = t // tm
    hd = ATT_HEADS * ATT_HEAD_DIM
    ospec = pl.BlockSpec((1, ATT_HEADS, tm, ATT_HEAD_DIM), lambda bi, ti: (bi, 0, ti, 0))
    oshape = jax.ShapeDtypeStruct((b, ATT_HEADS, t, ATT_HEAD_DIM), BF16)
    vspec = pl.BlockSpec((1, LANES), lambda bi, ti: (0, 0))
    return pl.pallas_call(
        _qk_prep_kernel,
        grid=(b, nt),
        in_specs=[pl.BlockSpec((tm, 2 * hd), lambda bi, ti: (bi * nt + ti, 1)),
                  pl.BlockSpec((tm, 1), lambda bi, ti: (bi * nt + ti, 0)), vspec, vspec, vspec],
        out_specs=[ospec, ospec],
        out_shape=[oshape, oshape],
        compiler_params=_params(("parallel", "parallel")),
        name="qk_prep",
    )(proj, pos, _inv_freq_lanes(ATT_HEAD_DIM), gq.reshape(1, LANES), gk.reshape(1, LANES))


def _idx_prep_kernel(x_ref, pos_ref, invf_ref, qi_ref, ki_ref):
    half = IDX_HEAD_DIM // 2
    cosf, sinf, first = _rope_tables(pos_ref, invf_ref, half)
    lane = lax.broadcasted_iota(I32, cosf.shape, 1)
    left = lane < IDX_HEAD_DIM

    def rope(x):
        return x * cosf + _swap_halves(x, half, first) * sinf

    def split(y):
        hi = y.astype(BF16).astype(F32)
        return hi, y - hi

    for j in range(IDX_HEADS // 2):
        hi, lo = split(rope(x_ref[:, j * LANES:(j + 1) * LANES]))
        even = jnp.where(left, hi, pltpu.roll(lo, IDX_HEAD_DIM, 1)).astype(BF16)
        odd = jnp.where(left, pltpu.roll(hi, IDX_HEAD_DIM, 1), lo).astype(BF16)
        for h, v in ((2 * j, even), (2 * j + 1, odd)):
            qi_ref[0, h, :, 0:LANES] = v
            qi_ref[0, h, :, LANES:2 * LANES] = v
    kcol = IDX_HEADS * IDX_HEAD_DIM
    khi, klo = split(rope(x_ref[:, kcol:kcol + LANES]))
    ki_ref[0, :, 0:LANES] = jnp.where(left, khi, pltpu.roll(khi, IDX_HEAD_DIM, 1)).astype(BF16)
    ki_ref[0, :, LANES:2 * LANES] = jnp.where(left, klo, pltpu.roll(klo, IDX_HEAD_DIM, 1)).astype(BF16)


def _idx_prep(idx, pos, b, t):
    tm = _tile(t, 256, SUBLANES)
    nt = t // tm
    w = idx.shape[1]
    return pl.pallas_call(
        _idx_prep_kernel,
        grid=(b, nt),
        in_specs=[pl.BlockSpec((tm, w), lambda bi, ti: (bi * nt + ti, 0)),
                  pl.BlockSpec((tm, 1), lambda bi, ti: (bi * nt + ti, 0)),
                  pl.BlockSpec((1, LANES), lambda bi, ti: (0, 0))],
        out_specs=[pl.BlockSpec((1, IDX_HEADS, tm, 2 * LANES), lambda bi, ti: (bi, 0, ti, 0)),
                   pl.BlockSpec((1, tm, 2 * LANES), lambda bi, ti: (bi, ti, 0))],
        out_shape=[jax.ShapeDtypeStruct((b, IDX_HEADS, t, 2 * LANES), BF16),
                   jax.ShapeDtypeStruct((b, t, 2 * LANES), BF16)],
        compiler_params=_params(("parallel", "parallel")),
        name="idx_prep",
    )(idx, pos, _inv_freq_lanes(IDX_HEAD_DIM))


def _mla_prep_kernel(m_ref, pos_ref, invf_ref, qa_ref, kva_ref, wuq_ref, wuk_ref, wuvt_ref,
                     qnn_ref, qnr_ref, knn_ref, knr_ref, q_ref, k_ref, vt_ref):
    half = MLA_ROPE // 2
    cosf, sinf, first = _rope_tables(pos_ref, invf_ref, half)
    lane = lax.broadcasted_iota(I32, cosf.shape, 1)
    left = lane < MLA_ROPE

    def rope(x):
        return x * cosf + _swap_halves(x, half, first) * sinf

    def norm(x, g_ref):
        return (x * lax.rsqrt(jnp.mean(x * x, axis=-1, keepdims=True) + EPS) * g_ref[...]).astype(BF16)

    cq = norm(m_ref[:, 0:MLA_Q_LORA], qa_ref)
    ckv = norm(m_ref[:, MLA_Q_LORA:MLA_Q_LORA + MLA_KV_LORA], kva_ref)
    kr = m_ref[:, MLA_Q_LORA + MLA_KV_LORA:MLA_Q_LORA + MLA_KV_LORA + LANES]
    qf = jnp.dot(cq, wuq_ref[...], preferred_element_type=F32)
    kvf = jnp.dot(ckv, wuk_ref[...], preferred_element_type=F32)
    nope_w = MLA_HEADS * MLA_NOPE
    _store_heads_t(vt_ref, lax.dot_general(wuvt_ref[...], ckv, NT_DIMS,
                                           preferred_element_type=F32).astype(vt_ref.dtype))

    for j in range(MLA_HEADS // 2):
        rs = qf[:, nope_w + j * LANES:nope_w + (j + 1) * LANES]
        sq = rs * rs
        ss_pair = (jnp.sum(jnp.where(left, sq, 0.0), axis=-1, keepdims=True),
                   jnp.sum(jnp.where(left, 0.0, sq), axis=-1, keepdims=True))
        for par in range(2):
            h = 2 * j + par
            nope = qf[:, h * MLA_NOPE:(h + 1) * MLA_NOPE]
            ms = (jnp.sum(nope * nope, axis=-1, keepdims=True) + ss_pair[par]) * (1.0 / MLA_QK)
            rsq = lax.rsqrt(ms + EPS)
            q_ref[0, h, :, 0:LANES] = (nope * rsq * qnn_ref[...]).astype(q_ref.dtype)
            rr = rope(rs * rsq * qnr_ref[...])
            if par == 1:
                rr = pltpu.roll(rr, MLA_ROPE, 1)
            q_ref[0, h, :, LANES:2 * LANES] = jnp.where(left, rr, 0.0).astype(q_ref.dtype)

    ss_kr = jnp.sum(jnp.where(left, kr * kr, 0.0), axis=-1, keepdims=True)
    base = jnp.where(left, rope(kr * knr_ref[...]), 0.0)
    for h in range(MLA_HEADS):
        nope = kvf[:, h * MLA_NOPE:(h + 1) * MLA_NOPE]
        ms = (jnp.sum(nope * nope, axis=-1, keepdims=True) + ss_kr) * (1.0 / MLA_QK)
        rsq = lax.rsqrt(ms + EPS)
        k_ref[0, h, :, 0:LANES] = (nope * rsq * knn_ref[...]).astype(k_ref.dtype)
        k_ref[0, h, :, LANES:2 * LANES] = (base * rsq).astype(k_ref.dtype)


def _mla_prep(mla, pos, b, t, qa, kva, w_uq, w_ukv, qn, kn):
    tm = _tile(t, 256, SUBLANES)
    nt = t // tm
    wq = w_uq.reshape(MLA_Q_LORA, MLA_HEADS, MLA_QK)
    wq = jnp.concatenate([wq[:, :, :MLA_NOPE].reshape(MLA_Q_LORA, -1),
                          wq[:, :, MLA_NOPE:].reshape(MLA_Q_LORA, -1)], axis=1).astype(BF16)
    wkv = w_ukv.reshape(MLA_KV_LORA, MLA_HEADS, MLA_NOPE + MLA_V)
    wk = wkv[:, :, :MLA_NOPE].reshape(MLA_KV_LORA, -1).astype(BF16)
    wvt = wkv[:, :, MLA_NOPE:].reshape(MLA_KV_LORA, -1).T.astype(BF16)
    dup = lambda g: jnp.tile(g[MLA_NOPE:], 2).reshape(1, LANES)
    full = lambda a: pl.BlockSpec(a.shape, lambda bi, ti: (0,) * a.ndim)
    consts = [_inv_freq_lanes(MLA_ROPE), qa.reshape(1, -1), kva.reshape(1, -1), wq, wk, wvt,
              qn[:MLA_NOPE].reshape(1, LANES), dup(qn), kn[:MLA_NOPE].reshape(1, LANES), dup(kn)]
    hspec = pl.BlockSpec((1, MLA_HEADS, tm, 2 * LANES), lambda bi, ti: (bi, 0, ti, 0))
    hshape = jax.ShapeDtypeStruct((b, MLA_HEADS, t, 2 * LANES), BF16)
    return pl.pallas_call(
        _mla_prep_kernel,
        grid=(b, nt),
        in_specs=[pl.BlockSpec((tm, mla.shape[1]), lambda bi, ti: (bi * nt + ti, 0)),
                  pl.BlockSpec((tm, 1), lambda bi, ti: (bi * nt + ti, 0))] + [full(a) for a in consts],
        out_specs=[hspec, hspec,
                   pl.BlockSpec((1, MLA_HEADS, 1, MLA_V, tm), lambda bi, ti: (bi, 0, ti, 0, 0))],
        out_shape=[hshape, hshape, jax.ShapeDtypeStruct((b, MLA_HEADS, nt, MLA_V, tm), BF16)],
        compiler_params=_params(("parallel", "parallel")),
        name="mla_prep",
    )(mla, pos, *consts)


def _attn_kernel(*refs, sparse, n_sel, tq, scale, idx_scale, t_total):
    if sparse:
        (q_ref, k_ref, vt_ref, qi_ref, ki_ref, wt_ref, o_ref, m_scr, l_scr, acc_scr,
         score_scr, bias_scr, thr_scr, tie_scr) = refs
    else:
        q_ref, k_ref, vt_ref, o_ref, m_scr, l_scr, acc_scr = refs
    n_heads, dv = vt_ref.shape[1], vt_ref.shape[3]
    qb = pl.program_id(1)
    nkc = qb + 1
    kio = lax.broadcasted_iota(I32, (tq, tq), 0)
    qio = lax.broadcasted_iota(I32, (tq, tq), 1)
    neg_inf = -jnp.inf

    def chunk_off(c):
        return pl.multiple_of(c * tq, tq)

    def causal(c):
        return (c * tq + kio) <= (qb * tq + qio)

    if sparse:
        n_idx_heads = qi_ref.shape[1]

        def score_chunk(c, carry):
            koff = chunk_off(c)
            ki = ki_ref[0, pl.ds(koff, tq), :]
            acc = jnp.zeros((tq, tq), F32)
            for h in range(n_idx_heads):
                s = lax.dot_general(ki, qi_ref[0, h], NT_DIMS, preferred_element_type=F32)
                acc = acc + jnp.maximum(s, 0.0) * wt_ref[0, h:h + 1, :]
            score_scr[pl.ds(koff, tq), :] = jnp.where(causal(c), acc * idx_scale, neg_inf)
            return carry

        lax.fori_loop(0, nkc, score_chunk, 0)

        def key_to_float(key):
            val = pltpu.bitcast(jnp.where(key < 0, key ^ 0x7FFFFFFF, key), F32)
            return jnp.where(key < NEG_INF_KEY, neg_inf, val)

        def count(pred):
            def hits(c):
                x = score_scr[pl.ds(chunk_off(c), tq), :]
                m = jnp.where(pred(x, c), 1.0, 0.0)
                return jnp.sum(m.reshape(tq // SUBLANES, SUBLANES, tq), axis=0)

            def body(pair, acc):
                c1 = 2 * pair
                has_second = c1 + 1 < nkc
                second = hits(jnp.where(has_second, c1 + 1, c1))
                return acc + hits(c1) + jnp.where(has_second, 1.0, 0.0) * second
            acc = lax.fori_loop(0, (nkc + 1) // 2, body, jnp.zeros((SUBLANES, tq), F32))
            return jnp.sum(acc, axis=0, keepdims=True)

        def count_ge(key):
            cand = key_to_float(key)
            return count(lambda x, c: x >= cand)

        k_sel = jnp.float32(n_sel)
        zero = jnp.zeros((1, tq), I32)
        thr_key = jnp.where(count_ge(zero) >= k_sel, zero, jnp.full((1, tq), INT_MIN, I32))

        def bit_step(i, key):
            cand = key | lax.shift_left(jnp.int32(1), jnp.int32(30) - i)
            return jnp.where(count_ge(cand) >= k_sel, cand, key)

        thr_key = lax.fori_loop(0, 31, bit_step, thr_key)
        thr = key_to_float(thr_key)
        n_ge = count(lambda x, c: x >= thr)
        thr_scr[0:1, :] = thr
        tie_scr[0:1, :] = jnp.full((1, tq), t_total, I32)

        @pl.when(jnp.max(n_ge) > k_sel)
        def _():
            need = k_sel - count(lambda x, c: x > thr)

            def below(x, c, m):
                return jnp.logical_and(x == thr, (c * tq + kio) < m)

            def idx_step(i, lo):
                cand = lo | lax.shift_left(jnp.int32(1), jnp.int32(t_total.bit_length() - 1) - i)
                return jnp.where(count(lambda x, c: below(x, c, cand)) < need, cand, lo)

            lo = lax.fori_loop(0, t_total.bit_length(), idx_step, zero)
            tie_scr[0:1, :] = lo + 1

        thr = thr_scr[0:1, :]
        tie_end = tie_scr[0:1, :]

        def bias_chunk(c, carry):
            koff = chunk_off(c)
            x = score_scr[pl.ds(koff, tq), :]
            tie_ok = jnp.logical_and(x == thr, (c * tq + kio) < tie_end)
            keep = jnp.logical_and(jnp.logical_or(x > thr, tie_ok), causal(c))
            bias_scr[pl.ds(koff, tq), :] = jnp.where(keep, 0.0, neg_inf)
            return carry

        lax.fori_loop(0, nkc, bias_chunk, 0)

    m_scr[...] = jnp.full(m_scr.shape, neg_inf, F32)
    l_scr[...] = jnp.zeros(l_scr.shape, F32)
    acc_scr[...] = jnp.zeros(acc_scr.shape, F32)

    def chunk_step(c, masked):
        koff = chunk_off(c)
        if sparse:
            bias = bias_scr[pl.ds(koff, tq), :]
        elif masked:
            allowed = causal(c)
        def qk(h):
            return lax.dot_general(k_ref[0, h, pl.ds(koff, tq), :], q_ref[0, h], NT_DIMS,
                                   preferred_element_type=F32)

        queued = [qk(h) for h in range(QK_LOOKAHEAD)]
        for h in range(n_heads):
            s = queued.pop(0) * (scale * LOG2_E)
            if h + QK_LOOKAHEAD < n_heads:
                queued.append(qk(h + QK_LOOKAHEAD))
            if sparse:
                s = s + bias
            elif masked:
                s = jnp.where(allowed, s, neg_inf)
            m_old = m_scr[h]
            m_new = jnp.maximum(m_old, jnp.max(s, axis=0, keepdims=True))
            m_ref = jnp.where(m_new == neg_inf, 0.0, m_new)
            alpha = jnp.exp2(m_old - m_ref)
            p = jnp.exp2(s - m_ref)
            l_scr[h] = alpha * l_scr[h] + jnp.sum(p, axis=0, keepdims=True)
            acc_scr[h] = alpha * acc_scr[h] + jnp.dot(vt_ref[0, h, c], p.astype(BF16),
                                                      preferred_element_type=F32)
            m_scr[h] = m_new

    def off_diagonal(c, carry):
        chunk_step(c, False)
        return carry

    if sparse:
        lax.fori_loop(0, nkc, off_diagonal, 0)
    else:
        lax.fori_loop(0, qb, off_diagonal, 0)
        chunk_step(qb, True)
    for h in range(n_heads):
        o_ref[:, h * dv:(h + 1) * dv] = (acc_scr[h] / l_scr[h]).T.astype(o_ref.dtype)


def _attention(q, k, vt, b, t, scale, sparse_inputs=None, n_sel=0):
    h, dq = q.shape[1], q.shape[3]
    nq, dv, tq = vt.shape[2:]
    assert nq * tq == t
    in_specs = [pl.BlockSpec((1, h, tq, dq), lambda bi, qi: (bi, 0, qi, 0)),
                pl.BlockSpec((1, h, t, dq), lambda bi, qi: (bi, 0, 0, 0)),
                pl.BlockSpec((1, h, nq, dv, tq), lambda bi, qi: (bi, 0, 0, 0, 0))]
    args = [q, k, vt]
    scratch = [pltpu.VMEM((h, 1, tq), F32), pltpu.VMEM((h, 1, tq), F32), pltpu.VMEM((h, dv, tq), F32)]
    sparse = sparse_inputs is not None
    if sparse:
        qi_, ki_, wt_ = sparse_inputs
        hi, dk = qi_.shape[1], qi_.shape[3]
        in_specs += [pl.BlockSpec((1, hi, tq, dk), lambda bi, qi: (bi, 0, qi, 0)),
                     pl.BlockSpec((1, t, dk), lambda bi, qi: (bi, 0, 0)),
                     pl.BlockSpec((1, hi, tq), lambda bi, qi: (bi, 0, qi))]
        args += [qi_, ki_, wt_]
        scratch += [pltpu.VMEM((t, tq), F32), pltpu.VMEM((t, tq), F32),
                    pltpu.VMEM((SUBLANES, tq), F32), pltpu.VMEM((SUBLANES, tq), I32)]
    return pl.pallas_call(
        functools.partial(_attn_kernel, sparse=sparse, n_sel=n_sel, tq=tq, scale=scale,
                          idx_scale=IDX_HEADS ** -0.5 * IDX_HEAD_DIM ** -0.5, t_total=t),
        grid=(b, nq),
        in_specs=in_specs,
        out_specs=pl.BlockSpec((tq, h * dv), lambda bi, qi: (bi * nq + qi, 0)),
        out_shape=jax.ShapeDtypeStruct((b * t, h * dv), BF16),
        scratch_shapes=scratch,
        compiler_params=_params(("parallel", "arbitrary")),
        name="sparse_attention" if sparse else "dense_attention",
    )(*args)


def _merge_kernel(hn_ref, a_ref, b_ref, c_ref, wg_ref, wb_ref, o_ref):
    hn = hn_ref[...]
    acc = None
    for n, br_ref in enumerate((a_ref, b_ref, c_ref)):
        gate = jax.nn.sigmoid(lax.dot_general(hn, wg_ref[n], NT_DIMS, preferred_element_type=F32))
        term = gate * jnp.dot(br_ref[...], wb_ref[n], preferred_element_type=F32)
        acc = term if acc is None else acc + term
    o_ref[...] = acc.astype(o_ref.dtype)


def _merge(hn, branches, w_gate, w_branch):
    m, d = hn.shape
    bw = w_branch.shape[1]
    tm = _tile(m, 512, SUBLANES)
    tn = _tile(d, 512)
    bspec = pl.BlockSpec((tm, bw), lambda i, j: (i, 0))
    return pl.pallas_call(
        _merge_kernel,
        grid=(m // tm, d // tn),
        in_specs=[pl.BlockSpec((tm, d), lambda i, j: (i, 0)), bspec, bspec, bspec,
                  pl.BlockSpec((N_BRANCH, tn, d), lambda i, j: (0, j, 0)),
                  pl.BlockSpec((N_BRANCH, bw, tn), lambda i, j: (0, 0, j))],
        out_specs=pl.BlockSpec((tm, tn), lambda i, j: (i, j)),
        out_shape=jax.ShapeDtypeStruct((m, d), BF16),
        compiler_params=_params(("parallel", "arbitrary")),
        name="gated_merge",
    )(hn, *branches, w_gate, w_branch)


def _ple_kernel(h_ref, hn_ref, p_ref, wg_ref, wp_ref, o_ref):
    gate = jax.nn.sigmoid(jnp.dot(hn_ref[...], wg_ref[...], preferred_element_type=F32))
    emb = jnp.dot(p_ref[...].astype(BF16), wp_ref[...], preferred_element_type=F32)
    o_ref[...] = h_ref[...] + emb * gate


def _ple(h, hn, p, w_gate, w_proj):
    m, d = h.shape
    pd = p.shape[1]
    tm = _tile(m, 1024, SUBLANES)
    tn = _tile(d, 1024)
    return pl.pallas_call(
        _ple_kernel,
        grid=(m // tm, d // tn),
        in_specs=[pl.BlockSpec((tm, tn), lambda i, j: (i, j)), pl.BlockSpec((tm, d), lambda i, j: (i, 0)),
                  pl.BlockSpec((tm, pd), lambda i, j: (i, 0)), pl.BlockSpec((d, tn), lambda i, j: (0, j)),
                  pl.BlockSpec((pd, tn), lambda i, j: (0, j))],
        out_specs=pl.BlockSpec((tm, tn), lambda i, j: (i, j)),
        out_shape=jax.ShapeDtypeStruct((m, d), F32),
        compiler_params=_params(("parallel", "arbitrary")),
        name="ple",
    )(h, hn, p, w_gate, w_proj)


def _ffn_kernel(be_ref, nu_ref, x_ref, wg_ref, wu_ref, wd_ref, *rest, has_res):
    if has_res:
        res_ref, o_ref = rest
    else:
        (o_ref,) = rest
    i, j = pl.program_id(0), pl.program_id(1)

    @pl.when(j == 0)
    def _():
        o_ref[...] = res_ref[...] if has_res else jnp.zeros_like(o_ref)

    @pl.when(i < nu_ref[0])
    def _():
        x = x_ref[...].astype(BF16)
        g = jnp.dot(x, wg_ref[0].astype(BF16), preferred_element_type=F32)
        u = jnp.dot(x, wu_ref[0].astype(BF16), preferred_element_type=F32)
        act = (g * jax.nn.sigmoid(g) * u).astype(BF16)
        o_ref[...] += jnp.dot(act, wd_ref[0].astype(BF16), preferred_element_type=F32)


def _ffn(x, w_gate, w_up, w_down, blk_expert, n_used, tm, res=None):
    r, d = x.shape
    f = w_gate.shape[2]
    tf = _tile(f, 512)
    nf = f // tf
    assert nf >= 2

    def jj(i, j, nu):
        return jnp.where(i < nu[0], j, nf - 1)

    rows_mode = pl.Buffered(1) if w_gate.dtype == F32 else None
    in_specs = [pl.BlockSpec((tm, d), lambda i, j, be, nu: (i, 0), pipeline_mode=rows_mode),
                pl.BlockSpec((1, d, tf), lambda i, j, be, nu: (be[i], 0, jj(i, j, nu))),
                pl.BlockSpec((1, d, tf), lambda i, j, be, nu: (be[i], 0, jj(i, j, nu))),
                pl.BlockSpec((1, tf, d), lambda i, j, be, nu: (be[i], jj(i, j, nu), 0))]
    args = [x, w_gate, w_up, w_down]
    if res is not None:
        in_specs.append(pl.BlockSpec((tm, d), lambda i, j, be, nu: (i, 0)))
        args.append(res)
    return pl.pallas_call(
        functools.partial(_ffn_kernel, has_res=res is not None),
        grid_spec=pltpu.PrefetchScalarGridSpec(
            num_scalar_prefetch=2,
            grid=(r // tm, nf),
            in_specs=in_specs,
            out_specs=pl.BlockSpec((tm, d), lambda i, j, be, nu: (i, 0), pipeline_mode=rows_mode)),
        out_shape=jax.ShapeDtypeStruct((r, d), F32),
        compiler_params=_params(("arbitrary", "arbitrary")),
        name="swiglu",
    )(blk_expert, n_used, *args)


def _router_kernel(h_ref, g_ref, rhi_ref, rlo_ref, hn_ref, info_ref, cnt_ref, run_scr, *, n_experts):
    i = pl.program_id(0)
    tm = h_ref.shape[0]

    @pl.when(i == 0)
    def _():
        run_scr[...] = jnp.zeros_like(run_scr)

    x = h_ref[...]
    xn = x * lax.rsqrt(jnp.mean(x * x, axis=-1, keepdims=True) + EPS) * g_ref[...]
    hi = xn.astype(BF16)
    hn_ref[...] = xn
    lo = (xn - hi.astype(F32)).astype(BF16)
    logits = (jnp.dot(hi, rhi_ref[...], preferred_element_type=F32)
              + jnp.dot(hi, rlo_ref[...], preferred_element_type=F32)
              + jnp.dot(lo, rhi_ref[...], preferred_element_type=F32))
    lane = lax.broadcasted_iota(I32, logits.shape, 1)
    lane_f = lane.astype(F32)
    logits = jnp.where(lane < n_experts, logits, -jnp.inf)

    def top(vals):
        v = jnp.max(vals, axis=-1, keepdims=True)
        idx = jnp.min(jnp.where(vals == v, lane_f, float(LANES)), axis=-1, keepdims=True)
        return v, idx

    v1, i1 = top(logits)
    v2, i2 = top(jnp.where(lane_f == i1, -jnp.inf, logits))
    e2 = jnp.exp(v2 - v1)
    g1 = 1.0 / (1.0 + e2)
    g2 = e2 / (1.0 + e2)

    oh1 = lane_f == i1
    oh2 = lane_f == i2
    both = jnp.where(jnp.logical_or(oh1, oh2), 1.0, 0.0)
    r_io = lax.broadcasted_iota(I32, (tm, tm), 0)
    c_io = lax.broadcasted_iota(I32, (tm, tm), 1)
    strict_lower = jnp.where(c_io < r_io, 1.0, 0.0).astype(BF16)
    before = jnp.dot(strict_lower, both.astype(BF16), preferred_element_type=F32) + run_scr[0:1, :]
    rank1 = jnp.sum(jnp.where(oh1, before, 0.0), axis=-1, keepdims=True)
    rank2 = jnp.sum(jnp.where(oh2, before, 0.0), axis=-1, keepdims=True)
    run_scr[0:1, :] = run_scr[0:1, :] + jnp.sum(both, axis=0, keepdims=True)

    info = jnp.zeros(logits.shape, F32)
    for col, val in enumerate((i1, i2, g1, g2, rank1, rank2)):
        info = jnp.where(lane == col, val, info)
    info_ref[...] = info
    cnt_ref[...] = jnp.broadcast_to(run_scr[0:1, :], cnt_ref.shape)


def _router(h, g, router):
    m, d = h.shape
    e = router.shape[1]
    tm = _tile(m, 256, SUBLANES)
    rp = jnp.zeros((d, LANES), F32).at[:, :e].set(router)
    rhi = rp.astype(BF16)
    rlo = (rp - rhi.astype(F32)).astype(BF16)
    return pl.pallas_call(
        functools.partial(_router_kernel, n_experts=e),
        grid=(m // tm,),
        in_specs=[pl.BlockSpec((tm, d), lambda i: (i, 0)), pl.BlockSpec((1, d), lambda i: (0, 0)),
                  pl.BlockSpec((d, LANES), lambda i: (0, 0)), pl.BlockSpec((d, LANES), lambda i: (0, 0))],
        out_specs=[pl.BlockSpec((tm, d), lambda i: (i, 0)), pl.BlockSpec((tm, LANES), lambda i: (i, 0)),
                   pl.BlockSpec((SUBLANES, LANES), lambda i: (0, 0))],
        out_shape=[jax.ShapeDtypeStruct((m, d), F32), jax.ShapeDtypeStruct((m, LANES), F32),
                   jax.ShapeDtypeStruct((SUBLANES, LANES), F32)],
        scratch_shapes=[pltpu.VMEM((SUBLANES, LANES), F32)],
        compiler_params=_params(("arbitrary",)),
        name="router",
    )(h, g.reshape(1, d), rhi, rlo)


def _row_copy(src, dst, sem, s, d):
    return pltpu.make_async_copy(src.at[pl.ds(s, 1), :], dst.at[pl.ds(d, 1), :], sem)


def _dispatch_kernel(dest_ref, x_ref, xs_in_hbm, xs_hbm, sem, *, rows):
    del xs_in_hbm

    def issue(r, carry):
        for kk in range(TOP_K):
            _row_copy(x_ref, xs_hbm, sem, r, dest_ref[0, 0, TOP_K * r + kk]).start()
        return carry

    lax.fori_loop(0, rows, issue, 0, unroll=8)
    for kk in range(TOP_K):
        pltpu.make_async_copy(x_ref, xs_hbm.at[pl.ds(0, rows), :], sem).wait()


def _dispatch(x, dest, n_rows):
    m, d = x.shape
    rows = _tile(m, 512, SUBLANES)
    return pl.pallas_call(
        functools.partial(_dispatch_kernel, rows=rows),
        grid=(m // rows,),
        in_specs=[pl.BlockSpec((1, 1, TOP_K * rows), lambda i: (i, 0, 0), memory_space=pltpu.SMEM),
                  pl.BlockSpec((rows, d), lambda i: (i, 0)), pl.BlockSpec(memory_space=pl.ANY)],
        out_specs=pl.BlockSpec(memory_space=pl.ANY),
        out_shape=jax.ShapeDtypeStruct((n_rows, d), x.dtype),
        scratch_shapes=[pltpu.SemaphoreType.DMA(())],
        input_output_aliases={2: 0},
        compiler_params=_params(("arbitrary",)),
        name="moe_dispatch",
    )(dest.reshape(m // rows, 1, TOP_K * rows), x, jnp.zeros((n_rows, d), x.dtype))


def _combine_kernel(dest_ref, h_ref, info_ref, g_ref, ys_hbm, o_ref, on_ref, buf, sem, *, rows):
    def issue(r, carry):
        for kk in range(TOP_K):
            pltpu.make_async_copy(ys_hbm.at[pl.ds(dest_ref[0, 0, TOP_K * r + kk], 1), :],
                                  buf.at[kk, pl.ds(r, 1), :], sem).start()
        return carry

    lax.fori_loop(0, rows, issue, 0, unroll=8)
    for kk in range(TOP_K):
        pltpu.make_async_copy(ys_hbm.at[pl.ds(0, rows), :], buf.at[kk], sem).wait()
    info = info_ref[...]
    out = h_ref[...] + (info[:, 2:3] * buf[0] + info[:, 3:4] * buf[1])
    o_ref[...] = out
    on_ref[...] = (out * lax.rsqrt(jnp.mean(out * out, axis=-1, keepdims=True) + EPS)
                   * g_ref[...]).astype(on_ref.dtype)


def _combine(h, info, dest, ys, g):
    m, d = h.shape
    rows = _tile(m, 512, SUBLANES)
    return pl.pallas_call(
        functools.partial(_combine_kernel, rows=rows),
        grid=(m // rows,),
        in_specs=[pl.BlockSpec((1, 1, TOP_K * rows), lambda i: (i, 0, 0), memory_space=pltpu.SMEM),
                  pl.BlockSpec((rows, d), lambda i: (i, 0)), pl.BlockSpec((rows, LANES), lambda i: (i, 0)),
                  pl.BlockSpec((1, d), lambda i: (0, 0)), pl.BlockSpec(memory_space=pl.ANY)],
        out_specs=[pl.BlockSpec((rows, d), lambda i: (i, 0)), pl.BlockSpec((rows, d), lambda i: (i, 0))],
        out_shape=[jax.ShapeDtypeStruct((m, d), F32), jax.ShapeDtypeStruct((m, d), BF16)],
        scratch_shapes=[pltpu.VMEM((TOP_K, rows, d), F32), pltpu.SemaphoreType.DMA(())],
        compiler_params=_params(("arbitrary",)),
        name="moe_combine",
    )(dest.reshape(m // rows, 1, TOP_K * rows), h, info, g.reshape(1, d), ys)


def _moe(h, g_ffn, router, w_gate, w_up, w_down, g_next):
    m, d = h.shape
    e = router.shape[1]
    hn, info, cnt = _router(h, g_ffn, router)
    counts = cnt[0, :e].astype(I32)
    padded = (counts + MOE_ROWS - 1) // MOE_ROWS * MOE_ROWS
    pad_ends = jnp.cumsum(padded)
    pad_starts = pad_ends - padded
    n_blocks = -(-(m * TOP_K) // MOE_ROWS) + e
    ids = info[:, 0:TOP_K].astype(I32)
    dest = jnp.sum(jnp.where(ids[:, :, None] == jnp.arange(e)[None, None, :], pad_starts[None, None, :], 0),
                   axis=-1) + info[:, 4:4 + TOP_K].astype(I32)
    blk_start = jnp.arange(n_blocks, dtype=I32) * MOE_ROWS
    blk_expert = jnp.minimum(jnp.sum(blk_start[:, None] >= pad_ends[None, :], axis=1), e - 1).astype(I32)
    n_used = (pad_ends[e - 1] // MOE_ROWS).astype(I32).reshape(1)
    xs = _dispatch(hn, dest, n_blocks * MOE_ROWS)
    ys = _ffn(xs, w_gate, w_up, w_down, blk_expert, n_used, MOE_ROWS)
    return _combine(h, info, dest, ys, g_next)


N_MAIN = 2 * LRU_WIDTH + 2 * ATT_HEADS * ATT_HEAD_DIM
O_V = N_MAIN
O_IDX = O_V + ATT_HEADS * ATT_HEAD_DIM
N_IDX = IDX_HEADS * IDX_HEAD_DIM + IDX_HEAD_DIM + IDX_HEADS
O_MLA = O_IDX + N_IDX
N_MLA = MLA_Q_LORA + MLA_KV_LORA + MLA_ROPE
O_GATE = O_MLA + N_MLA


def _split_w_in(w):
    d = w.shape[0]
    wt = jnp.swapaxes(w, 0, 1)

    def rows(lo, hi):
        return jnp.pad(wt[lo:hi], ((0, -(hi - lo) % LANES), (0, 0))).astype(BF16)

    gates = rows(O_GATE, O_GATE + N_BRANCH * d).reshape(N_BRANCH, d, d)
    return rows(0, O_V), rows(O_V, O_IDX), rows(O_IDX, O_MLA), rows(O_MLA, O_GATE), gates


def kernel(x, p, positions, ln_mix, w_in, conv_w, conv_b, lru_wa, lru_ba, lru_wx, lru_bx, lru_lambda,
           att_q_norm, att_k_norm, mla_qa_norm, mla_kva_norm, mla_w_uq, mla_w_ukv, mla_q_norm,
           mla_k_norm, w_branch, w_out, ln_ffn, dense_w_gate, dense_w_up, dense_w_down, moe_router,
           moe_w_gate, moe_w_up, moe_w_down, ple_norm, ple_w_gate, ple_w_proj):
    b, t, d = x.shape
    m = b * t
    depth = w_in.shape[0]
    n_sel = min(TOPK_MAX, t // 4)
    pos = positions.reshape(m, 1).astype(I32)
    h = x.reshape(m, d)
    idx_w_off = IDX_HEADS * IDX_HEAD_DIM + IDX_HEAD_DIM
    for i in range(depth):
        w_main, w_v, w_idx, w_mla, w_gates = _split_w_in(w_in[i])
        hn = _rmsnorm(h, ln_mix[i])
        proj = _matmul(hn, w_main, F32, transposed=True, name="in_proj_main")
        vt = _values_t(hn, w_v, b, t, ATT_HEADS)
        idx = _matmul(hn, w_idx, F32, tn_cap=w_idx.shape[0], transposed=True, name="in_proj_idx")
        mla = _matmul(hn, w_mla, F32, tn_cap=w_mla.shape[0], transposed=True, name="in_proj_mla")

        out_a = _rglru(proj, b, t, conv_w[i], conv_b[i], lru_wa[i], lru_ba[i], lru_wx[i], lru_bx[i],
                       lru_lambda[i])

        q, k = _qk_prep(proj, pos, b, t, att_q_norm[i], att_k_norm[i])
        qi, ki = _idx_prep(idx, pos, b, t)
        wt = idx[:, idx_w_off:idx_w_off + IDX_HEADS].reshape(b, t, IDX_HEADS).transpose(0, 2, 1)
        out_b = _attention(q, k, vt, b, t, ATT_HEAD_DIM ** -0.5, sparse_inputs=(qi, ki, wt), n_sel=n_sel)

        mq, mk, mv = _mla_prep(mla, pos, b, t, mla_qa_norm[i], mla_kva_norm[i], mla_w_uq[i],
                               mla_w_ukv[i], mla_q_norm[i], mla_k_norm[i])
        out_c = _attention(mq, mk, mv, b, t, MLA_QK ** -0.5)

        merged = _merge(hn, (out_a, out_b, out_c), w_gates, w_branch[i].astype(BF16))
        h = _matmul(merged, w_out[i].astype(BF16), F32, res=h, name="out_proj")

        if i % 2 == 0:
            j = i // 2
            hn2 = _rmsnorm(h, ln_ffn[i])
            tm = _tile(m, 512, SUBLANES)
            h = _ffn(hn2, dense_w_gate[j:j + 1].astype(BF16), dense_w_up[j:j + 1].astype(BF16),
                     dense_w_down[j:j + 1].astype(BF16), jnp.zeros((m // tm,), I32),
                     jnp.full((1,), m // tm, I32), tm, res=h)
            hn3 = _rmsnorm(h, ple_norm[i])
        else:
            j = i // 2
            h, hn3 = _moe(h, ln_ffn[i], moe_router[j], moe_w_gate[j], moe_w_up[j], moe_w_down[j],
                          ple_norm[i])
        h = _ple(h, hn3, p[i].reshape(m, -1), ple_w_gate[i].astype(BF16), ple_w_proj[i].astype(BF16))
    return h.reshape(b, t, d)
```
